```python
import jax, jax.numpy as jnp
from jax import lax
import numpy as np

D_MODEL = 1024
BATCH = 2
SEQ = 8192
DEPTH = 1

D_MIX = D_MODEL
ROPE_THETA = 500000.0
Q_BLOCK = 128
NSA_HEADS = 8
NSA_KV_GROUPS = 2
NSA_HPG = NSA_HEADS // NSA_KV_GROUPS
NSA_HEAD_DIM = 64
NSA_ROPE_DIM = NSA_HEAD_DIM // 4
CMP_BLOCK = 32
CMP_STRIDE = 16
CMP_HIDDEN = 4 * NSA_HEAD_DIM
SLC_BLOCK = 64
SLC_TOPK = 16
WINDOW = 512
FORCE_SCORE = 1e9
MLA_HEADS = 8
MLA_NOPE_DIM = 64
MLA_ROPE_DIM = 32
MLA_V_DIM = 64
MLA_Q_RANK = 384
MLA_KV_RANK = 256
D_FF = 2816
ALPHA = (2.0 * DEPTH) ** 0.25
BETA = (8.0 * DEPTH) ** -0.25
N_MOD = 9
EPS = 1e-5

NSA_Q = NSA_HEADS * NSA_HEAD_DIM
NSA_KV = NSA_KV_GROUPS * NSA_HEAD_DIM
NSA_GATE = 3 * NSA_HEADS
IN_SIZES = [NSA_Q] + [NSA_KV] * 6 + [NSA_GATE, MLA_Q_RANK, MLA_KV_RANK, MLA_ROPE_DIM]
D_IN = sum(IN_SIZES)

kernel_name = "hymba_nsa_mla_macaron_deepnorm_adaln"


def layer_norm(x, g, b):
    xf = x.astype(jnp.float32)
    mu = jnp.mean(xf, -1, keepdims=True)
    var = jnp.mean(jnp.square(xf - mu), -1, keepdims=True)
    return ((xf - mu) * lax.rsqrt(var + EPS) * g + b).astype(x.dtype)


def rms_norm(x, g):
    xf = x.astype(jnp.float32)
    return (xf * lax.rsqrt(jnp.mean(xf * xf, -1, keepdims=True) + EPS) * g).astype(x.dtype)


def rope(x, pos, dim):
    half = dim // 2
    inv = ROPE_THETA ** (-jnp.arange(half, dtype=jnp.float32) / half)
    ang = pos.astype(jnp.float32)[:, None] * inv[None, :]
    cos = jnp.cos(ang).astype(x.dtype)
    sin = jnp.sin(ang).astype(x.dtype)
    x1, x2, rest = x[..., :half], x[..., half:dim], x[..., dim:]
    return jnp.concatenate([x1 * cos - x2 * sin, x2 * cos + x1 * sin, rest], axis=-1)


def masked_softmax(s, mask):
    s = jnp.where(mask, s.astype(jnp.float32), -jnp.inf)
    m = jnp.max(s, -1, keepdims=True)
    m = jnp.where(jnp.isfinite(m), m, 0.0)
    e = jnp.where(mask, jnp.exp(s - m), 0.0)
    return e / jnp.maximum(jnp.sum(e, -1, keepdims=True), 1e-30)


def swiglu(u, wg, wu, wd):
    return (jax.nn.silu(u @ wg) * (u @ wu)) @ wd


def compress(kv, pos_emb, w1, w2):
    T = kv.shape[2]
    n_cmp = (T - CMP_BLOCK) // CMP_STRIDE + 1
    idx = np.arange(n_cmp)[:, None] * CMP_STRIDE + np.arange(CMP_BLOCK)[None, :]
    blocks = kv[:, :, idx] + pos_emb
    flat = blocks.reshape(blocks.shape[:3] + (CMP_BLOCK * NSA_HEAD_DIM,))
    return jax.nn.gelu(flat @ w1) @ w2


def gather_blocks(blocks, idx):
    return jax.vmap(jax.vmap(lambda kb, ix: kb[ix]))(blocks, idx)


def nsa_mixer(q, kc, vc, ks, vs, kw, vw, gate_logits, pos,
              cmp_k_pos, cmp_k_w1, cmp_k_w2, cmp_v_pos, cmp_v_w1, cmp_v_w2):
    B, T, _ = q.shape
    nb = T // Q_BLOCK
    scale = NSA_HEAD_DIM ** -0.5
    q = q.reshape(B, T, NSA_KV_GROUPS, NSA_HPG, NSA_HEAD_DIM).transpose(0, 2, 3, 1, 4)
    q = rope(q, pos, NSA_ROPE_DIM)
    gates = jax.nn.sigmoid(gate_logits.reshape(B, T, NSA_KV_GROUPS, NSA_HPG, 3).transpose(0, 2, 3, 1, 4))

    def kv_heads(a):
        return a.reshape(B, T, NSA_KV_GROUPS, NSA_HEAD_DIM).transpose(0, 2, 1, 3)

    kc, vc, ks, vs, kw, vw = [kv_heads(a) for a in (kc, vc, ks, vs, kw, vw)]
    kc, ks, kw = rope(kc, pos, NSA_ROPE_DIM), rope(ks, pos, NSA_ROPE_DIM), rope(kw, pos, NSA_ROPE_DIM)

    k_cmp = compress(kc, cmp_k_pos, cmp_k_w1, cmp_k_w2)
    v_cmp = compress(vc, cmp_v_pos, cmp_v_w1, cmp_v_w2)
    n_cmp = k_cmp.shape[2]
    cmp_start = np.arange(n_cmp) * CMP_STRIDE
    cmp_end = jnp.asarray(cmp_start + CMP_BLOCK - 1)
    n_slc = T // SLC_BLOCK
    n_sel = min(SLC_TOPK, n_slc)
    slc_start = np.arange(n_slc) * SLC_BLOCK
    overlap = jnp.asarray(((cmp_start[:, None] < slc_start[None, :] + SLC_BLOCK) &
                           (cmp_start[:, None] + CMP_BLOCK > slc_start[None, :])).astype(np.float32))
    ks_blk = ks.reshape(B, NSA_KV_GROUPS, n_slc, SLC_BLOCK, NSA_HEAD_DIM)
    vs_blk = vs.reshape(B, NSA_KV_GROUPS, n_slc, SLC_BLOCK, NSA_HEAD_DIM)
    pad = ((0, 0), (0, 0), (WINDOW, 0), (0, 0))
    kw_pad, vw_pad = jnp.pad(kw, pad), jnp.pad(vw, pad)
    jj = jnp.arange(n_slc)

    def block(args):
        i, qb, gb = args
        t = i * Q_BLOCK + jnp.arange(Q_BLOCK)
        s = jnp.einsum('bghqd,bgnd->bghqn', qb, k_cmp) * scale
        p_cmp = masked_softmax(s, cmp_end[None, :] <= t[:, None])
        o_cmp = jnp.einsum('bghqn,bgnd->bghqd', p_cmp.astype(v_cmp.dtype), v_cmp)
        imp = jnp.einsum('bghqn,nj->bgqj', p_cmp, overlap)
        cur = t // SLC_BLOCK
        causal_blk = (jj[None, :] * SLC_BLOCK) <= t[:, None]
        forced = (jj[None, :] == 0) | (jj[None, :] == cur[:, None]) | (jj[None, :] == cur[:, None] - 1)
        imp = jnp.where(forced & causal_blk, FORCE_SCORE, jnp.where(causal_blk, imp, -1.0))
        _, sel = lax.top_k(imp, n_sel)
        k_sel = gather_blocks(ks_blk, sel).reshape(B, NSA_KV_GROUPS, Q_BLOCK, n_sel * SLC_BLOCK, NSA_HEAD_DIM)
        v_sel = gather_blocks(vs_blk, sel).reshape(B, NSA_KV_GROUPS, Q_BLOCK, n_sel * SLC_BLOCK, NSA_HEAD_DIM)
        sel_pos = (sel[..., None] * SLC_BLOCK + jnp.arange(SLC_BLOCK)).reshape(B, NSA_KV_GROUPS, Q_BLOCK, n_sel * SLC_BLOCK)
        m_sel = (sel_pos <= t[None, None, :, None])[:, :, None]
        s = jnp.einsum('bghqd,bgqkd->bghqk', qb, k_sel) * scale
        p_sel = masked_softmax(s, m_sel)
        o_sel = jnp.einsum('bghqk,bgqkd->bghqd', p_sel.astype(v_sel.dtype), v_sel)
        kwb = lax.dynamic_slice_in_dim(kw_pad, i * Q_BLOCK, WINDOW + Q_BLOCK, axis=2)
        vwb = lax.dynamic_slice_in_dim(vw_pad, i * Q_BLOCK, WINDOW + Q_BLOCK, axis=2)
        s_pos = i * Q_BLOCK - WINDOW + jnp.arange(WINDOW + Q_BLOCK)
        diff = t[:, None] - s_pos[None, :]
        m_win = (s_pos[None, :] >= 0) & (diff >= 0) & (diff < WINDOW)
        s = jnp.einsum('bghqd,bgkd->bghqk', qb, kwb) * scale
        p_win = masked_softmax(s, m_win)
        o_win = jnp.einsum('bghqk,bgkd->bghqd', p_win.astype(vwb.dtype), vwb)
        gb = gb.astype(qb.dtype)
        return gb[..., 0:1] * o_cmp + gb[..., 1:2] * o_sel + gb[..., 2:3] * o_win

    qbs = jnp.moveaxis(q.reshape(B, NSA_KV_GROUPS, NSA_HPG, nb, Q_BLOCK, NSA_HEAD_DIM), 3, 0)
    gbs = jnp.moveaxis(gates.reshape(B, NSA_KV_GROUPS, NSA_HPG, nb, Q_BLOCK, 3), 3, 0)
    out = lax.map(block, (jnp.arange(nb), qbs, gbs))
    out = jnp.moveaxis(out, 0, 3).reshape(B, NSA_KV_GROUPS, NSA_HPG, T, NSA_HEAD_DIM)
    return out.transpose(0, 3, 1, 2, 4).reshape(B, T, NSA_HEADS * NSA_HEAD_DIM)


def mla_mixer(cq, ckv, k_rope, pos, q_norm_g, kv_norm_g, w_uq, w_ukv):
    B, T, _ = cq.shape
    nb = T // Q_BLOCK
    scale = (MLA_NOPE_DIM + MLA_ROPE_DIM) ** -0.5
    q = (rms_norm(cq, q_norm_g) @ w_uq).reshape(B, T, MLA_HEADS, MLA_NOPE_DIM + MLA_ROPE_DIM).transpose(0, 2, 1, 3)
    q_nope, q_rope = q[..., :MLA_NOPE_DIM], rope(q[..., MLA_NOPE_DIM:], pos, MLA_ROPE_DIM)
    kv = (rms_norm(ckv, kv_norm_g) @ w_ukv).reshape(B, T, MLA_HEADS, MLA_NOPE_DIM + MLA_V_DIM).transpose(0, 2, 1, 3)
    k_nope, v = kv[..., :MLA_NOPE_DIM], kv[..., MLA_NOPE_DIM:]
    k_rope = rope(k_rope, pos, MLA_ROPE_DIM)
    key_pos = jnp.arange(T)

    def block(args):
        i, qn, qr = args
        t = i * Q_BLOCK + jnp.arange(Q_BLOCK)
        s = (jnp.einsum('bhqd,bhkd->bhqk', qn, k_nope) + jnp.einsum('bhqd,bkd->bhqk', qr, k_rope)) * scale
        p = masked_softmax(s, key_pos[None, :] <= t[:, None])
        return jnp.einsum('bhqk,bhkd->bhqd', p.astype(v.dtype), v)

    qn_b = jnp.moveaxis(q_nope.reshape(B, MLA_HEADS, nb, Q_BLOCK, MLA_NOPE_DIM), 2, 0)
    qr_b = jnp.moveaxis(q_rope.reshape(B, MLA_HEADS, nb, Q_BLOCK, MLA_ROPE_DIM), 2, 0)
    out = lax.map(block, (jnp.arange(nb), qn_b, qr_b))
    out = jnp.moveaxis(out, 0, 2).reshape(B, MLA_HEADS, T, MLA_V_DIM)
    return out.transpose(0, 2, 1, 3).reshape(B, T, MLA_HEADS * MLA_V_DIM)


def setup_inputs(seed: int = 0) -> dict:
    key = jax.random.key(seed)
    ks = jax.random.split(key, 24)
    n = lambda k, shape, s: jax.random.normal(k, shape, jnp.float32) * s
    L = DEPTH
    return {
        "x": n(ks[0], (BATCH, SEQ, D_MODEL), 1.0),
        "c": n(ks[1], (BATCH, D_MODEL), 1.0),
        "w_ada": n(ks[2], (L, D_MODEL, N_MOD * D_MODEL), 0.25 * D_MODEL ** -0.5),
        "b_ada": n(ks[3], (L, N_MOD * D_MODEL), 0.01),
        "ln_g": 1.0 + n(ks[4], (L, 3, D_MODEL), 0.02),
        "ln_b": n(ks[5], (L, 3, D_MODEL), 0.02),
        "ffn1_wg": n(ks[6], (L, D_MODEL, D_FF), D_MODEL ** -0.5),
        "ffn1_wu": n(ks[7], (L, D_MODEL, D_FF), D_MODEL ** -0.5),
        "ffn1_wd": n(ks[8], (L, D_FF, D_MODEL), BETA * D_FF ** -0.5),
        "w_in": n(ks[9], (L, D_MODEL, D_IN), D_MODEL ** -0.5),
        "cmp_k_pos": n(ks[10], (L, CMP_BLOCK, NSA_HEAD_DIM), 0.02),
        "cmp_k_w1": n(ks[11], (L, CMP_BLOCK * NSA_HEAD_DIM, CMP_HIDDEN), (CMP_BLOCK * NSA_HEAD_DIM) ** -0.5),
        "cmp_k_w2": n(ks[12], (L, CMP_HIDDEN, NSA_HEAD_DIM), CMP_HIDDEN ** -0.5),
        "cmp_v_pos": n(ks[13], (L, CMP_BLOCK, NSA_HEAD_DIM), 0.02),
        "cmp_v_w1": n(ks[14], (L, CMP_BLOCK * NSA_HEAD_DIM, CMP_HIDDEN), (CMP_BLOCK * NSA_HEAD_DIM) ** -0.5),
        "cmp_v_w2": n(ks[15], (L, CMP_HIDDEN, NSA_HEAD_DIM), CMP_HIDDEN ** -0.5),
        "mla_q_norm": 1.0 + n(ks[16], (L, MLA_Q_RANK), 0.02),
        "mla_kv_norm": 1.0 + n(ks[17], (L, MLA_KV_RANK), 0.02),
        "mla_w_uq": n(ks[18], (L, MLA_Q_RANK, MLA_HEADS * (MLA_NOPE_DIM + MLA_ROPE_DIM)), MLA_Q_RANK ** -0.5),
        "mla_w_ukv": n(ks[19], (L, MLA_KV_RANK, MLA_HEADS * (MLA_NOPE_DIM + MLA_V_DIM)), MLA_KV_RANK ** -0.5),
        "w_out": n(ks[20], (L, D_MIX, D_MODEL), BETA * D_MIX ** -0.5),
        "ffn2_wg": n(ks[21], (L, D_MODEL, D_FF), D_MODEL ** -0.5),
        "ffn2_wu": n(ks[22], (L, D_MODEL, D_FF), D_MODEL ** -0.5),
        "ffn2_wd": n(ks[23], (L, D_FF, D_MODEL), BETA * D_FF ** -0.5),
    }


def reference(x, c, w_ada, b_ada, ln_g, ln_b, ffn1_wg, ffn1_wu, ffn1_wd, w_in,
              cmp_k_pos, cmp_k_w1, cmp_k_w2, cmp_v_pos, cmp_v_w1, cmp_v_w2,
              mla_q_norm, mla_kv_norm, mla_w_uq, mla_w_ukv, w_out,
              ffn2_wg, ffn2_wu, ffn2_wd):
    T = x.shape[1]
    pos = jnp.arange(T)
    split_at = [int(v) for v in np.cumsum(IN_SIZES)[:-1]]
    for l in range(DEPTH):
        mod = jax.nn.silu(c) @ w_ada[l] + b_ada[l]
        sh1, sc1, g1, sh2, sc2, g2, sh3, sc3, g3 = jnp.split(mod[:, None, :], N_MOD, axis=-1)
        u = x * (1.0 + sc1) + sh1
        x = layer_norm(ALPHA * x + 0.5 * (1.0 + g1) * swiglu(u, ffn1_wg[l], ffn1_wu[l], ffn1_wd[l]), ln_g[l, 0], ln_b[l, 0])
        u = x * (1.0 + sc2) + sh2
        h = u @ w_in[l]
        q_a, kc, vc, ksl, vsl, kwn, vwn, gate_logits, cq, ckv, k_rope = jnp.split(h, split_at, axis=-1)
        o_nsa = nsa_mixer(q_a, kc, vc, ksl, vsl, kwn, vwn, gate_logits, pos,
                          cmp_k_pos[l], cmp_k_w1[l], cmp_k_w2[l], cmp_v_pos[l], cmp_v_w1[l], cmp_v_w2[l])
        o_mla = mla_mixer(cq, ckv, k_rope, pos, mla_q_norm[l], mla_kv_norm[l], mla_w_uq[l], mla_w_ukv[l])
        y = jnp.concatenate([o_nsa, o_mla], axis=-1) @ w_out[l]
        x = layer_norm(ALPHA * x + (1.0 + g2) * y, ln_g[l, 1], ln_b[l, 1])
        u = x * (1.0 + sc3) + sh3
        x = layer_norm(ALPHA * x + 0.5 * (1.0 + g3) * swiglu(u, ffn2_wg[l], ffn2_wu[l], ffn2_wd[l]), ln_g[l, 2], ln_b[l, 2])
    return x
```

```python
import functools

import numpy as np
import jax
import jax.numpy as jnp
from jax import lax
from jax.experimental import pallas as pl
from jax.experimental.pallas import tpu as pltpu

F32 = jnp.float32
BF16 = jnp.bfloat16

ROPE_THETA = 500000.0
NSA_HEADS = 8
NSA_KV_GROUPS = 2
NSA_HPG = NSA_HEADS // NSA_KV_GROUPS
NSA_HEAD_DIM = 64
NSA_ROPE_DIM = 16
CMP_BLOCK = 32
CMP_STRIDE = 16
CMP_HIDDEN = 256
SLC_BLOCK = 64
SLC_TOPK = 16
WINDOW = 512
FORCE_SCORE = 1e9
MLA_HEADS = 8
MLA_NOPE_DIM = 64
MLA_ROPE_DIM = 32
MLA_V_DIM = 64
MLA_Q_RANK = 384
MLA_KV_RANK = 256
EPS = 1e-5
N_MOD = 9

LANES = 128
NEG = -1e30
PICKED = -3e38
VMEM_LIMIT = 48 * 1024 * 1024

NSA_TQ = 128
NSA_TK = 512
MLA_T = 512

OFF_Q, OFF_KV, OFF_GATE, OFF_CQ, OFF_CKV, OFF_KR, PROJ_W = 0, 1024, 1792, 2048, 2432, 2688, 2816


def _cparams(sem):
    return pltpu.CompilerParams(dimension_semantics=sem, vmem_limit_bytes=VMEM_LIMIT)


def _layer_norm(z, g, b):
    mu = jnp.mean(z, -1, keepdims=True)
    zc = z - mu
    var = jnp.mean(zc * zc, -1, keepdims=True)
    return zc * lax.rsqrt(var + EPS) * g + b


def _dot(a, b):
    return jnp.dot(a, b, preferred_element_type=F32)


def _dot_t(a, b):
    return lax.dot_general(a, b, (((1,), (1,)), ((), ())), preferred_element_type=F32)


def _rep_rows(a, n):
    return jnp.concatenate([a] * n, axis=0)


def _mod_body(c_ref, w_ref, b_ref, o_ref):
    c = c_ref[...]
    s = (c * jax.nn.sigmoid(c)).astype(BF16)
    o_ref[...] = _dot(s, w_ref[...].astype(BF16)) + b_ref[...]


def _mod_call(c, w_ada, b_ada):
    B, D = c.shape
    rows = 8
    c8 = jnp.zeros((rows, D), F32).at[:B].set(c)
    out = pl.pallas_call(
        _mod_body,
        grid=(N_MOD,),
        in_specs=[pl.BlockSpec((rows, D), lambda j: (0, 0)),
                  pl.BlockSpec((D, D), lambda j: (0, j)),
                  pl.BlockSpec((1, D), lambda j: (0, j))],
        out_specs=pl.BlockSpec((rows, D), lambda j: (0, j)),
        out_shape=jax.ShapeDtypeStruct((rows, N_MOD * D), F32),
        compiler_params=_cparams(("parallel",)),
        name="mod",
    )(c8, w_ada, b_ada.reshape(1, N_MOD * D))
    return out[:B].reshape(B, N_MOD, 1, D)


def _ffn_body(x_ref, sh_ref, sc_ref, g_ref, wg_ref, wu_ref, wd_ref, lng_ref, lnb_ref, o_ref,
              u_scr, acc_scr, *, alpha, coef, n_ff):
    j = pl.program_id(2)

    @pl.when(j == 0)
    def _():
        u_scr[...] = (x_ref[...] * (1.0 + sc_ref[...]) + sh_ref[...]).astype(BF16)
        acc_scr[...] = jnp.zeros_like(acc_scr)

    u = u_scr[...]
    a = _dot(u, wg_ref[...])
    b = _dot(u, wu_ref[...])
    h = (a * jax.nn.sigmoid(a)) * b
    acc_scr[...] += _dot(h.astype(BF16), wd_ref[...])

    @pl.when(j == n_ff - 1)
    def _():
        z = alpha * x_ref[...] + coef * (1.0 + g_ref[...]) * acc_scr[...]
        o_ref[...] = _layer_norm(z, lng_ref[...], lnb_ref[...])


def _ffn_call(x, sh, sc, g, wg, wu, wd, lng, lnb, alpha, coef):
    B, T, D = x.shape
    FF = wg.shape[1]
    tm = min(T, 1024)
    tf = 256
    assert T % tm == 0 and FF % tf == 0
    n_ff = FF // tf
    row = pl.BlockSpec((None, 1, D), lambda b, i, j: (b, 0, 0))
    vec = pl.BlockSpec((1, D), lambda b, i, j: (0, 0))
    return pl.pallas_call(
        functools.partial(_ffn_body, alpha=alpha, coef=coef, n_ff=n_ff),
        grid=(B, T // tm, n_ff),
        in_specs=[pl.BlockSpec((None, tm, D), lambda b, i, j: (b, i, 0)), row, row, row,
                  pl.BlockSpec((D, tf), lambda b, i, j: (0, j)),
                  pl.BlockSpec((D, tf), lambda b, i, j: (0, j)),
                  pl.BlockSpec((tf, D), lambda b, i, j: (j, 0)), vec, vec],
        out_specs=pl.BlockSpec((None, tm, D), lambda b, i, j: (b, i, 0)),
        out_shape=jax.ShapeDtypeStruct((B, T, D), F32),
        scratch_shapes=[pltpu.VMEM((tm, D), BF16), pltpu.VMEM((tm, D), F32)],
        compiler_params=_cparams(("parallel", "parallel", "arbitrary")),
        name="ffn",
    )(x, sh, sc, g, wg, wu, wd, lng.reshape(1, D), lnb.reshape(1, D))


def _rope_slot(x, c, sa, sb, half):
    return x * c + pltpu.roll(x, LANES - half, 1) * sa + pltpu.roll(x, half, 1) * sb


def _rms_norm(x, g):
    return x * lax.rsqrt(jnp.mean(x * x, -1, keepdims=True) + EPS) * g


def _proj_body(x_ref, sh_ref, sc_ref, w_ref, gq_ref, gkv_ref, wuq_ref, wuk_ref, wuv_ref,
               cn_ref, sna_ref, snb_ref, cm_ref, sma_ref, smb_ref,
               qn_ref, kvc_ref, kvs_ref, kvw_ref, gate_ref, qm_ref, km_ref, vm_ref,
               *, nsa_scale, mla_scale):
    u = (x_ref[...] * (1.0 + sc_ref[...]) + sh_ref[...]).astype(BF16)
    h = _dot(u, w_ref[...])
    cn, sna, snb = cn_ref[...], sna_ref[...], snb_ref[...]
    cm, sma, smb = cm_ref[...], sma_ref[...], smb_ref[...]
    nh = NSA_ROPE_DIM // 2
    mh = MLA_ROPE_DIM // 2

    for s in range(NSA_HEADS):
        lo = OFF_Q + s * LANES
        qn_ref[:, s * LANES:(s + 1) * LANES] = (
            _rope_slot(h[:, lo:lo + LANES], cn, sna, snb, nh) * nsa_scale).astype(BF16)
    for idx, ref in enumerate((kvc_ref, kvs_ref, kvw_ref)):
        for g in range(NSA_KV_GROUPS):
            lo = OFF_KV + (idx * NSA_KV_GROUPS + g) * LANES
            ref[g] = _rope_slot(h[:, lo:lo + LANES], cn, sna, snb, nh).astype(BF16)
    gate_ref[...] = jax.nn.sigmoid(h[:, OFF_GATE:OFF_GATE + NSA_KV_GROUPS * LANES])

    cqn = _rms_norm(h[:, OFF_CQ:OFF_CQ + MLA_Q_RANK], gq_ref[...]).astype(BF16)
    qm = _dot(cqn, wuq_ref[...])
    ckvn = _rms_norm(h[:, OFF_CKV:OFF_CKV + MLA_KV_RANK], gkv_ref[...]).astype(BF16)
    kn = _dot(ckvn, wuk_ref[...])
    vm_ref[...] = _dot(ckvn, wuv_ref[...]).astype(BF16)
    kr = _rope_slot(h[:, OFF_KR:OFF_KR + LANES], cm, sma, smb, mh)
    for s in range(MLA_HEADS):
        sl = slice(s * LANES, (s + 1) * LANES)
        qm_ref[:, sl] = (_rope_slot(qm[:, sl], cm, sma, smb, mh) * mla_scale).astype(BF16)
        km_ref[:, sl] = (kn[:, sl] + kr).astype(BF16)


def _proj_call(x, sh, sc, w_all, gq, gkv, wuq, wuk, wuv, tabs):
    B, T, D = x.shape
    tm = min(T, 256)
    G = NSA_KV_GROUPS
    row = pl.BlockSpec((None, 1, D), lambda b, i: (b, 0, 0))

    def full(a):
        return pl.BlockSpec(a.shape, lambda b, i: (0,) * a.ndim)

    tab = pl.BlockSpec((tm, LANES), lambda b, i: (i, 0))
    tok = lambda w: pl.BlockSpec((None, tm, w), lambda b, i: (b, i, 0))
    grp = pl.BlockSpec((G, None, tm, LANES), lambda b, i: (0, b, i, 0))
    sd = jax.ShapeDtypeStruct
    return pl.pallas_call(
        functools.partial(_proj_body, nsa_scale=NSA_HEAD_DIM ** -0.5,
                          mla_scale=(MLA_NOPE_DIM + MLA_ROPE_DIM) ** -0.5),
        grid=(B, T // tm),
        in_specs=[tok(D), row, row, full(w_all), full(gq), full(gkv), full(wuq), full(wuk), full(wuv)]
                 + [tab] * 6,
        out_specs=[tok(NSA_HEADS * LANES), grp, grp, grp, tok(G * LANES),
                   tok(MLA_HEADS * LANES), tok(MLA_HEADS * LANES), tok(MLA_HEADS * MLA_V_DIM)],
        out_shape=[sd((B, T, NSA_HEADS * LANES), BF16), sd((G, B, T, LANES), BF16),
                   sd((G, B, T, LANES), BF16), sd((G, B, T, LANES), BF16), sd((B, T, G * LANES), F32),
                   sd((B, T, MLA_HEADS * LANES), BF16), sd((B, T, MLA_HEADS * LANES), BF16),
                   sd((B, T, MLA_HEADS * MLA_V_DIM), BF16)],
        compiler_params=_cparams(("parallel", "parallel")),
        name="proj",
    )(x, sh, sc, w_all, gq, gkv, wuq, wuk, wuv, *tabs)


def _gelu_tanh(x):
    return x * (0.5 * (1.0 + jnp.tanh(0.7978845608028654 * (x + 0.044715 * (x * x * x)))))


def _cmp_body(c_ref, w1a_ref, w1b_ref, pa_ref, pb_ref, w2_ref, o_ref):
    c = c_ref[...]
    nc = c.shape[0]
    w1a, w1b = w1a_ref[...], w1b_ref[...]
    a = _dot(c, w1a)
    b = _dot(c, w1b)
    bias = _dot(pa_ref[...], w1a) + _dot(pb_ref[...], w1b)
    pre = a + pltpu.roll(b, nc - 1, 0) + bias[0:1]
    o_ref[...] = _dot(_gelu_tanh(pre).astype(BF16), w2_ref[...]).astype(BF16)


def _cmp_call(kvc, w1a, w1b, pa, pb, w2):
    G, B, T, _ = kvc.shape
    nc = T // CMP_STRIDE
    c = kvc.reshape(G, B, nc, CMP_STRIDE * LANES)

    def full(a):
        return pl.BlockSpec(a.shape, lambda g, b: (0,) * a.ndim)

    return pl.pallas_call(
        _cmp_body,
        grid=(G, B),
        in_specs=[pl.BlockSpec((None, None, nc, CMP_STRIDE * LANES), lambda g, b: (g, b, 0, 0)),
                  full(w1a), full(w1b), full(pa), full(pb), full(w2)],
        out_specs=pl.BlockSpec((None, None, nc, LANES), lambda g, b: (g, b, 0, 0)),
        out_shape=jax.ShapeDtypeStruct((G, B, nc, LANES), BF16),
        compiler_params=_cparams(("parallel", "parallel")),
        name="cmp",
    )(c, w1a, w1b, pa, pb, w2)


def _nsa_body(q_ref, gate_ref, kvc_ref, kvs_ref, kvw_ref, ov_ref, o_ref, m_scr, l_scr, acc_scr, *, n_slc):
    i = pl.program_id(2)
    nc = kvc_ref.shape[0]
    hpg, tq_n, rows = NSA_HPG, NSA_TQ, NSA_HPG * NSA_TQ
    t0 = i * tq_n
    q = q_ref[...]
    qs = jnp.concatenate([q[:, h * LANES:(h + 1) * LANES] for h in range(hpg)], axis=0)
    iota = lambda shape, d: lax.broadcasted_iota(jnp.int32, shape, d)

    kvc = kvc_ref[...]
    s = _dot_t(qs, kvc)
    tq = t0 + iota((tq_n, nc), 0)
    n = iota((tq_n, nc), 1)
    valid = (n * CMP_STRIDE + (CMP_BLOCK - 1) <= tq) & (n < nc - 1)
    s = s + _rep_rows(jnp.where(valid, 0.0, NEG), hpg)
    m = jnp.max(s, -1, keepdims=True)
    e = jnp.exp(s - m) * _rep_rows(jnp.where(valid, 1.0, 0.0), hpg)
    p = e / jnp.maximum(jnp.sum(e, -1, keepdims=True), 1e-30)
    o_cmp = _dot(p.astype(BF16), kvc)

    ps = p[0:tq_n] + p[tq_n:2 * tq_n] + p[2 * tq_n:3 * tq_n] + p[3 * tq_n:4 * tq_n]
    ph = ps.astype(BF16)
    plo = (ps - ph.astype(F32)).astype(BF16)
    ov = ov_ref[...]
    imp_t = (_dot(ph, ov) + _dot(plo, ov)).T

    jr = iota((LANES, tq_n), 0)
    tl = t0 + iota((LANES, tq_n), 1)
    cur = lax.shift_right_logical(tl, 6)
    causal = jr * SLC_BLOCK <= tl
    forced = (jr == 0) | (jr == cur) | (jr == cur - 1)
    v = jnp.where(forced & causal, FORCE_SCORE, jnp.where(causal, imp_t, -1.0))
    if n_slc < LANES:
        v = jnp.where(jr < n_slc, v, PICKED)
    jf = jr.astype(F32)
    sel_t = jnp.zeros((LANES, tq_n), F32)
    for _ in range(SLC_TOPK):
        mx = jnp.max(v, axis=0, keepdims=True)
        idx = jnp.min(jnp.where(v == mx, jf, float(LANES)), axis=0, keepdims=True)
        hit = jf == idx
        sel_t = jnp.where(hit, 1.0, sel_t)
        v = jnp.where(hit, PICKED, v)
    sel = sel_t.T.astype(BF16)

    m_scr[...] = jnp.full(m_scr.shape, NEG, F32)
    l_scr[...] = jnp.zeros(l_scr.shape, F32)
    acc_scr[...] = jnp.zeros(acc_scr.shape, F32)
    n_kt = lax.shift_right_logical(t0 + tq_n - 1, 9) + 1

    def body(kt, carry):
        k0 = pl.multiple_of(kt * NSA_TK, NSA_TK)
        kv = kvs_ref[pl.ds(k0, NSA_TK), :]
        sc = _dot_t(qs, kv)
        key = k0 + iota((tq_n, NSA_TK), 1)
        expand = jnp.where(lax.shift_right_logical(k0 + iota((LANES, NSA_TK), 1), 6) == iota((LANES, NSA_TK), 0),
                           1.0, 0.0).astype(BF16)
        picked = _dot(sel, expand)
        ok = (picked > 0.5) & (key <= t0 + iota((tq_n, NSA_TK), 0))
        sc = sc + _rep_rows(jnp.where(ok, 0.0, NEG), hpg)
        m_prev = m_scr[...]
        m_new = jnp.maximum(m_prev, jnp.max(sc, -1, keepdims=True))
        alpha = jnp.exp(m_prev - m_new)
        pp = jnp.exp(sc - jnp.concatenate([m_new] * (NSA_TK // LANES), axis=1))
        l_scr[...] = alpha * l_scr[...] + jnp.sum(pp, -1, keepdims=True)
        acc_scr[...] = alpha * acc_scr[...] + _dot(pp.astype(BF16), kv)
        m_scr[...] = m_new
        return carry

    lax.fori_loop(0, n_kt, body, 0)
    o_sel = acc_scr[...] / l_scr[...]

    wk = WINDOW + tq_n
    start = pl.multiple_of(jnp.maximum(t0 - WINDOW, 0), tq_n)
    kv = kvw_ref[pl.ds(start, wk), :]
    s = _dot_t(qs, kv)
    diff = (t0 + iota((tq_n, wk), 0)) - (start + iota((tq_n, wk), 1))
    s = s + _rep_rows(jnp.where((diff >= 0) & (diff < WINDOW), 0.0, NEG), hpg)
    e = jnp.exp(s - jnp.max(s, -1, keepdims=True))
    o_win = _dot(e.astype(BF16), kv) / jnp.sum(e, -1, keepdims=True)

    gt = gate_ref[...]
    gcol = lambda br: jnp.concatenate([gt[:, h * 3 + br:h * 3 + br + 1] for h in range(hpg)], axis=0)
    out = gcol(0) * o_cmp + gcol(1) * o_sel + gcol(2) * o_win
    out = jnp.where(iota((rows, LANES), 1) >= NSA_HEAD_DIM, out, 0.0).astype(BF16)
    for h in range(hpg):
        o_ref[:, h * LANES:(h + 1) * LANES] = out[h * tq_n:(h + 1) * tq_n]


def _nsa_call(qn, gates, kvcmp, kvs, kvw, ov):
    B, T, _ = qn.shape
    G = NSA_KV_GROUPS
    nc = kvcmp.shape[2]
    n_slc = T // SLC_BLOCK
    assert SLC_TOPK <= n_slc <= LANES and nc % LANES == 0 and T % NSA_TK == 0 and T >= WINDOW + NSA_TQ
    gw = NSA_HPG * LANES
    rows = NSA_HPG * NSA_TQ
    whole = lambda r: pl.BlockSpec((None, None, r, LANES), lambda b, g, i: (g, b, 0, 0))
    return pl.pallas_call(
        functools.partial(_nsa_body, n_slc=n_slc),
        grid=(B, G, T // NSA_TQ),
        in_specs=[pl.BlockSpec((None, NSA_TQ, gw), lambda b, g, i: (b, i, g)),
                  pl.BlockSpec((None, NSA_TQ, LANES), lambda b, g, i: (b, i, g)),
                  whole(nc), whole(T), whole(T),
                  pl.BlockSpec(ov.shape, lambda b, g, i: (0, 0))],
        out_specs=pl.BlockSpec((None, NSA_TQ, gw), lambda b, g, i: (b, i, g)),
        out_shape=jax.ShapeDtypeStruct((B, T, NSA_HEADS * LANES), BF16),
        scratch_shapes=[pltpu.VMEM((rows, LANES), F32)] * 3,
        compiler_params=_cparams(("parallel", "parallel", "arbitrary")),
        name="nsa",
    )(qn, gates, kvcmp, kvs, kvw, ov)


def _mla_body(q_ref, k_ref, v_ref, o_ref, m_scr, l_scr, acc_scr):
    i = pl.program_id(2)
    tt = MLA_T
    m_scr[...] = jnp.full(m_scr.shape, NEG, F32)
    l_scr[...] = jnp.zeros(l_scr.shape, F32)
    acc_scr[...] = jnp.zeros(acc_scr.shape, F32)

    def step(kt, masked):
        k0 = pl.multiple_of(kt * tt, tt)
        vt = v_ref[pl.ds(k0, tt), :]
        if masked:
            bias = jnp.where(lax.broadcasted_iota(jnp.int32, (tt, tt), 1)
                             <= lax.broadcasted_iota(jnp.int32, (tt, tt), 0), 0.0, NEG)
        for hh in range(2):
            sl = slice(hh * LANES, (hh + 1) * LANES)
            s = _dot_t(q_ref[:, sl], k_ref[pl.ds(k0, tt), sl])
            if masked:
                s = s + bias
            m_prev = m_scr[hh]
            m_new = jnp.maximum(m_prev, jnp.max(s, -1, keepdims=True))
            alpha = jnp.exp(m_prev - m_new)
            p = jnp.exp(s - jnp.concatenate([m_new] * (tt // LANES), axis=1))
            l_scr[hh] = alpha * l_scr[hh] + jnp.sum(p, -1, keepdims=True)
            acc_scr[hh] = alpha * acc_scr[hh] + _dot(p.astype(BF16), vt)
            m_scr[hh] = m_new

    def body(kt, carry):
        step(kt, False)
        return carry

    lax.fori_loop(0, i, body, 0)
    step(i, True)
    o_a = acc_scr[0] / l_scr[0]
    o_b = acc_scr[1] / l_scr[1]
    lane = lax.broadcasted_iota(jnp.int32, (tt, LANES), 1)
    o_ref[...] = jnp.where(lane < MLA_V_DIM, o_a, o_b).astype(BF16)


def _mla_call(qm, km, vm):
    B, T, _ = qm.shape
    assert T % MLA_T == 0
    pairs = MLA_HEADS // 2
    return pl.pallas_call(
        _mla_body,
        grid=(B, pairs, T // MLA_T),
        in_specs=[pl.BlockSpec((None, MLA_T, 2 * LANES), lambda b, p, i: (b, i, p)),
                  pl.BlockSpec((None, T, 2 * LANES), lambda b, p, i: (b, 0, p)),
                  pl.BlockSpec((None, T, LANES), lambda b, p, i: (b, 0, p))],
        out_specs=pl.BlockSpec((None, MLA_T, LANES), lambda b, p, i: (b, i, p)),
        out_shape=jax.ShapeDtypeStruct((B, T, MLA_HEADS * MLA_V_DIM), BF16),
        scratch_shapes=[pltpu.VMEM((2, MLA_T, LANES), F32)] * 3,
        compiler_params=_cparams(("parallel", "parallel", "arbitrary")),
        name="mla",
    )(qm, km, vm)


def _out_body(x_ref, on_ref, om_ref, g_ref, wn_ref, wm_ref, lng_ref, lnb_ref, o_ref, *, alpha):
    y = _dot(on_ref[...], wn_ref[...]) + _dot(om_ref[...], wm_ref[...])
    z = alpha * x_ref[...] + (1.0 + g_ref[...]) * y
    o_ref[...] = _layer_norm(z, lng_ref[...], lnb_ref[...])


def _out_call(x, o_nsa, o_mla, g, w_on, w_om, lng, lnb, alpha):
    B, T, D = x.shape
    tm = min(T, 512)
    tok = lambda w: pl.BlockSpec((None, tm, w), lambda b, i: (b, i, 0))
    full = lambda a: pl.BlockSpec(a.shape, lambda b, i: (0,) * a.ndim)
    lng, lnb = lng.reshape(1, D), lnb.reshape(1, D)
    return pl.pallas_call(
        functools.partial(_out_body, alpha=alpha),
        grid=(B, T // tm),
        in_specs=[tok(D), tok(o_nsa.shape[-1]), tok(o_mla.shape[-1]),
                  pl.BlockSpec((None, 1, D), lambda b, i: (b, 0, 0)),
                  full(w_on), full(w_om), full(lng), full(lnb)],
        out_specs=tok(D),
        out_shape=jax.ShapeDtypeStruct((B, T, D), F32),
        compiler_params=_cparams(("parallel", "parallel")),
        name="out",
    )(x, o_nsa, o_mla, g, w_on, w_om, lng, lnb)


def _rope_tables(T, lane0, dim):
    half = dim // 2
    inv = ROPE_THETA ** (-jnp.arange(half, dtype=F32) / half)
    ang = jnp.arange(T).astype(F32)[:, None] * inv[None, :]
    cos, sin = jnp.cos(ang), jnp.sin(ang)
    c = jnp.ones((T, LANES), F32).at[:, lane0:lane0 + half].set(cos).at[:, lane0 + half:lane0 + dim].set(cos)
    z = jnp.zeros((T, LANES), F32)
    return c, z.at[:, lane0:lane0 + half].set(-sin), z.at[:, lane0 + half:lane0 + dim].set(sin)


def _pad_slots(w, n, width):
    K = w.shape[0]
    return jnp.pad(w.reshape(K, n, width), ((0, 0), (0, 0), (0, LANES - width))).reshape(K, n * LANES)


def _proj_weights(w_in):
    D = w_in.shape[0]
    hd, G = NSA_HEAD_DIM, NSA_KV_GROUPS
    nq, nkv = NSA_HEADS * hd, G * hd
    off = np.cumsum([0, nq] + [nkv] * 6 + [3 * NSA_HEADS, MLA_Q_RANK, MLA_KV_RANK, MLA_ROPE_DIM])
    cols = [_pad_slots(w_in[:, :nq], NSA_HEADS, hd)]
    for pair in range(3):
        ko, vo = off[1 + 2 * pair], off[2 + 2 * pair]
        for g in range(G):
            cols += [w_in[:, ko + g * hd:ko + (g + 1) * hd], w_in[:, vo + g * hd:vo + (g + 1) * hd]]
    per_g = 3 * NSA_HPG
    for g in range(G):
        cols.append(jnp.pad(w_in[:, off[7] + g * per_g:off[7] + (g + 1) * per_g], ((0, 0), (0, LANES - per_g))))
    cols += [w_in[:, off[8]:off[9]], w_in[:, off[9]:off[10]]]
    cols.append(jnp.pad(w_in[:, off[10]:off[11]],
                        ((0, 0), (MLA_NOPE_DIM, LANES - MLA_NOPE_DIM - MLA_ROPE_DIM))))
    w_all = jnp.concatenate(cols, axis=1).astype(BF16)
    assert w_all.shape == (D, PROJ_W)
    return w_all


def _cmp_weights(k_pos, k_w1, k_w2, v_pos, v_w1, v_w2):
    hd, st, hid = NSA_HEAD_DIM, CMP_STRIDE, CMP_HIDDEN

    def half(lo):
        w = jnp.zeros((st, LANES, 2 * hid), F32)
        w = w.at[:, :hd, :hid].set(k_w1[lo * hd:(lo + st) * hd].reshape(st, hd, hid))
        w = w.at[:, hd:, hid:].set(v_w1[lo * hd:(lo + st) * hd].reshape(st, hd, hid))
        pos = jnp.concatenate([k_pos[lo:lo + st], v_pos[lo:lo + st]], axis=1).reshape(1, st * LANES)
        return w.reshape(st * LANES, 2 * hid).astype(BF16), jnp.broadcast_to(pos, (8, st * LANES)).astype(BF16)

    w1a, pa = half(0)
    w1b, pb = half(st)
    w2 = jnp.zeros((2 * hid, LANES), F32).at[:hid, :hd].set(k_w2).at[hid:, hd:].set(v_w2).astype(BF16)
    return w1a, w1b, pa, pb, w2


def _overlap(T):
    nc, n_slc = T // CMP_STRIDE, T // SLC_BLOCK
    cs = np.arange(nc)[:, None] * CMP_STRIDE
    ss = np.arange(LANES)[None, :] * SLC_BLOCK
    ov = (cs < ss + SLC_BLOCK) & (cs + CMP_BLOCK > ss) & (np.arange(nc)[:, None] < nc - 1) & (np.arange(LANES)[None, :] < n_slc)
    return jnp.asarray(ov.astype(np.float32), BF16)


def kernel(x, c, w_ada, b_ada, ln_g, ln_b, ffn1_wg, ffn1_wu, ffn1_wd, w_in, cmp_k_pos, cmp_k_w1, cmp_k_w2,
           cmp_v_pos, cmp_v_w1, cmp_v_w2, mla_q_norm, mla_kv_norm, mla_w_uq, mla_w_ukv, w_out,
           ffn2_wg, ffn2_wu, ffn2_wd):
    B, T, D = x.shape
    depth = w_ada.shape[0]
    alpha = (2.0 * depth) ** 0.25
    tabs = _rope_tables(T, 0, NSA_ROPE_DIM) + _rope_tables(T, MLA_NOPE_DIM, MLA_ROPE_DIM)
    ov = _overlap(T)
    for l in range(depth):
        mod = _mod_call(c, w_ada[l], b_ada[l])
        sh1, sc1, g1, sh2, sc2, g2, sh3, sc3, g3 = [mod[:, k] for k in range(N_MOD)]
        x = _ffn_call(x, sh1, sc1, g1, ffn1_wg[l].astype(BF16), ffn1_wu[l].astype(BF16), ffn1_wd[l].astype(BF16),
                      ln_g[l, 0], ln_b[l, 0], alpha, 0.5)

        wuq = _pad_slots(mla_w_uq[l], MLA_HEADS, MLA_NOPE_DIM + MLA_ROPE_DIM).astype(BF16)
        wukv = mla_w_ukv[l].reshape(MLA_KV_RANK, MLA_HEADS, MLA_NOPE_DIM + MLA_V_DIM)
        wuk = _pad_slots(wukv[:, :, :MLA_NOPE_DIM].reshape(MLA_KV_RANK, -1), MLA_HEADS, MLA_NOPE_DIM).astype(BF16)
        wuv = wukv[:, :, MLA_NOPE_DIM:].reshape(MLA_KV_RANK, -1).astype(BF16)
        qn, kvc, kvs, kvw, gates, qm, km, vm = _proj_call(
            x, sh2, sc2, _proj_weights(w_in[l]), mla_q_norm[l].reshape(1, -1), mla_kv_norm[l].reshape(1, -1),
            wuq, wuk, wuv, tabs)
        kvcmp = _cmp_call(kvc, *_cmp_weights(cmp_k_pos[l], cmp_k_w1[l], cmp_k_w2[l],
                                             cmp_v_pos[l], cmp_v_w1[l], cmp_v_w2[l]))
        o_nsa = _nsa_call(qn, gates, kvcmp, kvs, kvw, ov)
        o_mla = _mla_call(qm, km, vm)
        n_nsa = NSA_HEADS * NSA_HEAD_DIM
        w_on = jnp.pad(w_out[l, :n_nsa].reshape(NSA_HEADS, NSA_HEAD_DIM, D),
                       ((0, 0), (LANES - NSA_HEAD_DIM, 0), (0, 0))).reshape(NSA_HEADS * LANES, D).astype(BF16)
        x = _out_call(x, o_nsa, o_mla, g2, w_on, w_out[l, n_nsa:].astype(BF16), ln_g[l, 1], ln_b[l, 1], alpha)

        x = _ffn_call(x, sh3, sc3, g3, ffn2_wg[l].astype(BF16), ffn2_wu[l].astype(BF16), ffn2_wd[l].astype(BF16),
                      ln_g[l, 2], ln_b[l, 2], alpha, 0.5)
    return x
```

```python
import functools

import numpy as np
import jax
import jax.numpy as jnp
from jax import lax
from jax.experimental import pallas as pl
from jax.experimental.pallas import tpu as pltpu

F32 = jnp.float32
BF16 = jnp.bfloat16

ROPE_THETA = 500000.0
NSA_HEADS = 8
NSA_KV_GROUPS = 2
NSA_HPG = NSA_HEADS // NSA_KV_GROUPS
NSA_HEAD_DIM = 64
NSA_ROPE_DIM = 16
CMP_BLOCK = 32
CMP_STRIDE = 16
CMP_HIDDEN = 256
SLC_BLOCK = 64
SLC_TOPK = 16
WINDOW = 512
FORCE_SCORE = 1e9
MLA_HEADS = 8
MLA_NOPE_DIM = 64
MLA_ROPE_DIM = 32
MLA_V_DIM = 64
MLA_Q_RANK = 384
MLA_KV_RANK = 256
EPS = 1e-5
N_MOD = 9

LANES = 128
NEG = -1e30
M_INIT = -1e29
PICKED = -3e38
ROW_CHUNK = 32
LOG2E = 1.4426950408889634
VMEM_LIMIT = 48 * 1024 * 1024
MLA_VMEM_LIMIT = 56 * 1024 * 1024

NSA_TQ = 256
NSA_TK = 1024
MLA_T = 512
MLA_HPS = 4

OFF_Q, OFF_KV, OFF_GATE, OFF_CQ, OFF_CKV, OFF_KR, PROJ_W = 0, 1024, 1792, 2048, 2432, 2688, 2816


def _cparams(sem, vmem=VMEM_LIMIT):
    return pltpu.CompilerParams(dimension_semantics=sem, vmem_limit_bytes=vmem)


def _layer_norm(z, g, b):
    mu = jnp.mean(z, -1, keepdims=True)
    zc = z - mu
    var = jnp.mean(zc * zc, -1, keepdims=True)
    return zc * lax.rsqrt(var + EPS) * g + b


def _dot(a, b):
    return jnp.dot(a, b, preferred_element_type=F32)


def _dot_t(a, b):
    return lax.dot_general(a, b, (((1,), (1,)), ((), ())), preferred_element_type=F32)


def _rep_rows(a, n):
    return jnp.concatenate([a] * n, axis=0)


def _rep_lanes(a, n):
    return jnp.concatenate([a] * n, axis=1)


def _iota(shape, d):
    return lax.broadcasted_iota(jnp.int32, shape, d)


def _mod_body(c_ref, w_ref, b_ref, o_ref):
    c = c_ref[...]
    s = (c * jax.nn.sigmoid(c)).astype(BF16)
    o_ref[...] = _dot(s, w_ref[...].astype(BF16)) + b_ref[...]


def _mod_call(c, w_ada, b_ada):
    B, D = c.shape
    rows = 8
    c8 = jnp.zeros((rows, D), F32).at[:B].set(c)
    out = pl.pallas_call(
        _mod_body,
        grid=(N_MOD,),
        in_specs=[pl.BlockSpec((rows, D), lambda j: (0, 0)),
                  pl.BlockSpec((D, D), lambda j: (0, j)),
                  pl.BlockSpec((1, D), lambda j: (0, j))],
        out_specs=pl.BlockSpec((rows, D), lambda j: (0, j)),
        out_shape=jax.ShapeDtypeStruct((rows, N_MOD * D), F32),
        compiler_params=_cparams(("parallel",)),
        name="mod",
    )(c8, w_ada, b_ada.reshape(1, N_MOD * D))
    return out[:B].reshape(B, N_MOD, 1, D)


def _ffn_body(x_ref, sh_ref, sc_ref, g_ref, wg_ref, wu_ref, wd_ref, lng_ref, lnb_ref, o_ref,
              u_scr, acc_scr, *, alpha, coef, n_ff):
    j = pl.program_id(2)

    @pl.when(j == 0)
    def _():
        u_scr[...] = (x_ref[...] * (1.0 + sc_ref[...]) + sh_ref[...]).astype(BF16)
        acc_scr[...] = jnp.zeros_like(acc_scr)

    u = u_scr[...]
    a = _dot(u, wg_ref[...])
    b = _dot(u, wu_ref[...])
    h = (a * jax.nn.sigmoid(a)) * b
    acc_scr[...] += _dot(h.astype(BF16), wd_ref[...])

    @pl.when(j == n_ff - 1)
    def _():
        z = alpha * x_ref[...] + coef * (1.0 + g_ref[...]) * acc_scr[...]
        o_ref[...] = _layer_norm(z, lng_ref[...], lnb_ref[...])


def _ffn_call(x, sh, sc, g, wg, wu, wd, lng, lnb, alpha, coef):
    B, T, D = x.shape
    FF = wg.shape[1]
    tm = min(T, 1024)
    tf = 256
    assert T % tm == 0 and FF % tf == 0
    n_ff = FF // tf
    row = pl.BlockSpec((None, 1, D), lambda b, i, j: (b, 0, 0))
    vec = pl.BlockSpec((1, D), lambda b, i, j: (0, 0))
    return pl.pallas_call(
        functools.partial(_ffn_body, alpha=alpha, coef=coef, n_ff=n_ff),
        grid=(B, T // tm, n_ff),
        in_specs=[pl.BlockSpec((None, tm, D), lambda b, i, j: (b, i, 0)), row, row, row,
                  pl.BlockSpec((D, tf), lambda b, i, j: (0, j)),
                  pl.BlockSpec((D, tf), lambda b, i, j: (0, j)),
                  pl.BlockSpec((tf, D), lambda b, i, j: (j, 0)), vec, vec],
        out_specs=pl.BlockSpec((None, tm, D), lambda b, i, j: (b, i, 0)),
        out_shape=jax.ShapeDtypeStruct((B, T, D), F32),
        scratch_shapes=[pltpu.VMEM((tm, D), BF16), pltpu.VMEM((tm, D), F32)],
        compiler_params=_cparams(("parallel", "parallel", "arbitrary")),
        name="ffn",
    )(x, sh, sc, g, wg, wu, wd, lng.reshape(1, D), lnb.reshape(1, D))


def _rope_slot(x, c, sa, sb, half):
    return x * c + pltpu.roll(x, LANES - half, 1) * sa + pltpu.roll(x, half, 1) * sb


def _rms_norm(x, g):
    return x * lax.rsqrt(jnp.mean(x * x, -1, keepdims=True) + EPS) * g


def _proj_body(x_ref, sh_ref, sc_ref, w_ref, gq_ref, gkv_ref, wuq_ref, wuk_ref, wuv_ref,
               cn_ref, sna_ref, snb_ref, cm_ref, sma_ref, smb_ref,
               qn_ref, kvc_ref, kvs_ref, kvw_ref, gate_ref, qm_ref, km_ref, vm_ref,
               *, nsa_scale, mla_scale):
    u = (x_ref[...] * (1.0 + sc_ref[...]) + sh_ref[...]).astype(BF16)
    h = _dot(u, w_ref[...])
    cn, sna, snb = cn_ref[...], sna_ref[...], snb_ref[...]
    cm, sma, smb = cm_ref[...], sma_ref[...], smb_ref[...]
    nh = NSA_ROPE_DIM // 2
    mh = MLA_ROPE_DIM // 2

    for s in range(NSA_HEADS):
        lo = OFF_Q + s * LANES
        qn_ref[:, s * LANES:(s + 1) * LANES] = (
            _rope_slot(h[:, lo:lo + LANES], cn, sna, snb, nh) * nsa_scale).astype(BF16)
    for idx, ref in enumerate((kvc_ref, kvs_ref, kvw_ref)):
        for g in range(NSA_KV_GROUPS):
            lo = OFF_KV + (idx * NSA_KV_GROUPS + g) * LANES
            ref[g] = _rope_slot(h[:, lo:lo + LANES], cn, sna, snb, nh).astype(BF16)
    gate_ref[...] = jax.nn.sigmoid(h[:, OFF_GATE:OFF_GATE + NSA_KV_GROUPS * LANES])

    cqn = _rms_norm(h[:, OFF_CQ:OFF_CQ + MLA_Q_RANK], gq_ref[...]).astype(BF16)
    qm = _dot(cqn, wuq_ref[...])
    ckvn = _rms_norm(h[:, OFF_CKV:OFF_CKV + MLA_KV_RANK], gkv_ref[...]).astype(BF16)
    kn = _dot(ckvn, wuk_ref[...])
    vm_ref[...] = _dot(ckvn, wuv_ref[...]).astype(BF16)
    kr = _rope_slot(h[:, OFF_KR:OFF_KR + LANES], cm, sma, smb, mh)
    for s in range(MLA_HEADS):
        sl = slice(s * LANES, (s + 1) * LANES)
        qm_ref[:, sl] = (_rope_slot(qm[:, sl], cm, sma, smb, mh) * mla_scale).astype(BF16)
        km_ref[:, sl] = (kn[:, sl] + kr).astype(BF16)


def _proj_call(x, sh, sc, w_all, gq, gkv, wuq, wuk, wuv, tabs):
    B, T, D = x.shape
    tm = min(T, 256)
    G = NSA_KV_GROUPS
    row = pl.BlockSpec((None, 1, D), lambda b, i: (b, 0, 0))

    def full(a):
        return pl.BlockSpec(a.shape, lambda b, i: (0,) * a.ndim)

    tab = pl.BlockSpec((tm, LANES), lambda b, i: (i, 0))
    tok = lambda w: pl.BlockSpec((None, tm, w), lambda b, i: (b, i, 0))
    grp = pl.BlockSpec((G, None, tm, LANES), lambda b, i: (0, b, i, 0))
    sd = jax.ShapeDtypeStruct
    return pl.pallas_call(
        functools.partial(_proj_body, nsa_scale=NSA_HEAD_DIM ** -0.5 * LOG2E,
                          mla_scale=(MLA_NOPE_DIM + MLA_ROPE_DIM) ** -0.5 * LOG2E),
        grid=(B, T // tm),
        in_specs=[tok(D), row, row, full(w_all), full(gq), full(gkv), full(wuq), full(wuk), full(wuv)]
                 + [tab] * 6,
        out_specs=[tok(NSA_HEADS * LANES), grp, grp, grp, tok(G * LANES),
                   tok(MLA_HEADS * LANES), tok(MLA_HEADS * LANES), tok(MLA_HEADS * MLA_V_DIM)],
        out_shape=[sd((B, T, NSA_HEADS * LANES), BF16), sd((G, B, T, LANES), BF16),
                   sd((G, B, T, LANES), BF16), sd((G, B, T, LANES), BF16), sd((B, T, G * LANES), F32),
                   sd((B, T, MLA_HEADS * LANES), BF16), sd((B, T, MLA_HEADS * LANES), BF16),
                   sd((B, T, MLA_HEADS * MLA_V_DIM), BF16)],
        compiler_params=_cparams(("parallel", "parallel")),
        name="proj",
    )(x, sh, sc, w_all, gq, gkv, wuq, wuk, wuv, *tabs)


def _gelu_tanh(x):
    return x * (0.5 * (1.0 + jnp.tanh(0.7978845608028654 * (x + 0.044715 * (x * x * x)))))


def _cmp_body(c_ref, w1a_ref, w1b_ref, pa_ref, pb_ref, w2_ref, o_ref):
    c = c_ref[...]
    nc = c.shape[0]
    w1a, w1b = w1a_ref[...], w1b_ref[...]
    a = _dot(c, w1a)
    b = _dot(c, w1b)
    bias = _dot(pa_ref[...], w1a) + _dot(pb_ref[...], w1b)
    pre = a + pltpu.roll(b, nc - 1, 0) + bias[0:1]
    o_ref[...] = _dot(_gelu_tanh(pre).astype(BF16), w2_ref[...]).astype(BF16)


def _cmp_call(kvc, w1a, w1b, pa, pb, w2):
    G, B, T, _ = kvc.shape
    nc = T // CMP_STRIDE
    c = kvc.reshape(G, B, nc, CMP_STRIDE * LANES)

    def full(a):
        return pl.BlockSpec(a.shape, lambda g, b: (0,) * a.ndim)

    return pl.pallas_call(
        _cmp_body,
        grid=(G, B),
        in_specs=[pl.BlockSpec((None, None, nc, CMP_STRIDE * LANES), lambda g, b: (g, b, 0, 0)),
                  full(w1a), full(w1b), full(pa), full(pb), full(w2)],
        out_specs=pl.BlockSpec((None, None, nc, LANES), lambda g, b: (g, b, 0, 0)),
        out_shape=jax.ShapeDtypeStruct((G, B, nc, LANES), BF16),
        compiler_params=_cparams(("parallel", "parallel")),
        name="cmp",
    )(c, w1a, w1b, pa, pb, w2)


def _nsa_body(q_ref, gate_ref, kvc_ref, kvs_ref, kvw_ref, ov_ref, et_ref, o_ref,
              m_scr, l_scr, acc_scr, part_scr, *, n_slc):
    i = pl.program_id(2)
    nc = kvc_ref.shape[0]
    hpg, tq_n, rows, tk = NSA_HPG, NSA_TQ, NSA_HPG * NSA_TQ, NSA_TK
    t0 = pl.multiple_of(i * tq_n, tq_n)
    q = q_ref[...]
    qs = jnp.concatenate([q[:, h * LANES:(h + 1) * LANES] for h in range(hpg)], axis=0)

    wk = WINDOW + tq_n
    start = pl.multiple_of(jnp.maximum(t0 - WINDOW, 0), tq_n)
    kv = kvw_ref[pl.ds(start, wk), :]
    s = _dot_t(qs, kv)
    diff = (t0 + _iota((tq_n, wk), 0)) - (start + _iota((tq_n, wk), 1))
    s = s + _rep_rows(jnp.where((diff >= 0) & (diff < WINDOW), 0.0, NEG), hpg)
    e = jnp.exp2(s - jnp.max(s, -1, keepdims=True))
    o_win = _dot(e.astype(BF16), kv) / jnp.sum(e, -1, keepdims=True)

    kvc = kvc_ref[...]
    s = _dot_t(qs, kvc)
    tq = t0 + _iota((tq_n, nc), 0)
    n = _iota((tq_n, nc), 1)
    valid = (n * CMP_STRIDE + (CMP_BLOCK - 1) <= tq) & (n < nc - 1)
    s = s + _rep_rows(jnp.where(valid, 0.0, NEG), hpg)
    e = jnp.exp2(s - jnp.max(s, -1, keepdims=True)) * _rep_rows(jnp.where(valid, 1.0, 0.0), hpg)
    p = e / jnp.maximum(jnp.sum(e, -1, keepdims=True), 1e-30)
    o_cmp = _dot(p.astype(BF16), kvc)
    gt = gate_ref[...]
    gcol = lambda br: jnp.concatenate([gt[:, h * 3 + br:h * 3 + br + 1] for h in range(hpg)], axis=0)
    part_scr[...] = gcol(0) * o_cmp + gcol(2) * o_win

    ps = p[0:tq_n] + p[tq_n:2 * tq_n] + p[2 * tq_n:3 * tq_n] + p[3 * tq_n:4 * tq_n]
    ph = ps.astype(BF16)
    plo = (ps - ph.astype(F32)).astype(BF16)
    ov = ov_ref[...]
    imp_t = (_dot(ph, ov) + _dot(plo, ov)).T

    jr = _iota((LANES, tq_n), 0)
    tl = t0 + _iota((LANES, tq_n), 1)
    cur = lax.shift_right_logical(tl, 6)
    causal = jr * SLC_BLOCK <= tl
    forced = (jr == 0) | (jr == cur) | (jr == cur - 1)
    v = jnp.where(forced & causal, FORCE_SCORE, jnp.where(causal, imp_t, -1.0))
    if n_slc < LANES:
        v = jnp.where(jr < n_slc, v, PICKED)
    jf = jr.astype(F32)
    sel_t = jnp.zeros((LANES, tq_n), F32)
    for _ in range(SLC_TOPK):
        mx = jnp.max(v, axis=0, keepdims=True)
        idx = jnp.min(jnp.where(v == mx, jf, float(LANES)), axis=0, keepdims=True)
        hit = jf == idx
        sel_t = jnp.where(hit, 1.0, sel_t)
        v = jnp.where(hit, PICKED, v)
    sel = sel_t.T

    upto = _iota((tq_n, LANES), 1) < lax.shift_right_logical(t0 + tq_n, 6)
    lhs = jnp.concatenate([qs, _rep_rows(jnp.where((sel > 0.5) & upto, 0.0, NEG).astype(BF16), hpg)], axis=1)
    n_kt = lax.shift_right_logical(t0 + tq_n + tk - 1, tk.bit_length() - 1)

    def keys(kt):
        return kvs_ref[pl.ds(pl.multiple_of(kt * tk, tk), tk), :]

    def scores(kt):
        k0 = pl.multiple_of(kt * tk, tk)
        return _dot_t(lhs, jnp.concatenate([kvs_ref[pl.ds(k0, tk), :], et_ref[pl.ds(k0, tk), :]], axis=1))

    m_scr[...] = jnp.full(m_scr.shape, M_INIT, F32)
    l_scr[...] = jnp.zeros(l_scr.shape, F32)
    acc_scr[...] = jnp.zeros(acc_scr.shape, F32)
    rc = ROW_CHUNK

    def tile(k, causal):
        s_all = scores(k)
        if causal:
            kpos = k * tk + _iota((rc, tk), 1)
        ps, als = [], []
        for r in range(0, rows, rc):
            sl = slice(r, r + rc)
            sc = s_all[sl]
            if causal:
                sc = sc + jnp.where(kpos <= t0 + (r % tq_n) + _iota((rc, tk), 0), 0.0, NEG)
            m_prev = m_scr[sl]
            m_new = jnp.maximum(m_prev, jnp.max(sc, -1, keepdims=True))
            alpha = jnp.exp2(m_prev - m_new)
            pp = jnp.exp2(sc - _rep_lanes(m_new, tk // LANES))
            l_scr[sl] = alpha * l_scr[sl] + jnp.sum(pp, -1, keepdims=True)
            m_scr[sl] = m_new
            ps.append(pp.astype(BF16))
            als.append(alpha)
        acc_scr[...] = jnp.concatenate(als, axis=0) * acc_scr[...] + _dot(jnp.concatenate(ps, axis=0), keys(k))

    def body(k, carry):
        tile(k, False)
        return carry

    lax.fori_loop(0, n_kt - 1, body, 0)
    tile(n_kt - 1, True)
    acc = acc_scr[...]
    out = part_scr[...] + gcol(1) * (acc / l_scr[...])
    out = jnp.where(_iota((rows, LANES), 1) >= NSA_HEAD_DIM, out, 0.0).astype(BF16)
    for h in range(hpg):
        o_ref[:, h * LANES:(h + 1) * LANES] = out[h * tq_n:(h + 1) * tq_n]


def _nsa_call(qn, gates, kvcmp, kvs, kvw, ov, et):
    B, T, _ = qn.shape
    G = NSA_KV_GROUPS
    nc = kvcmp.shape[2]
    n_slc = T // SLC_BLOCK
    assert SLC_TOPK <= n_slc <= LANES and nc % LANES == 0 and T % NSA_TK == 0 and T >= WINDOW + NSA_TQ
    assert NSA_TK & (NSA_TK - 1) == 0 and NSA_TQ % SLC_BLOCK == 0 and NSA_TK % NSA_TQ == 0
    gw = NSA_HPG * LANES
    rows = NSA_HPG * NSA_TQ
    whole = lambda r: pl.BlockSpec((None, None, r, LANES), lambda b, g, i: (g, b, 0, 0))
    const = lambda a: pl.BlockSpec(a.shape, lambda b, g, i: (0, 0))
    return pl.pallas_call(
        functools.partial(_nsa_body, n_slc=n_slc),
        grid=(B, G, T // NSA_TQ),
        in_specs=[pl.BlockSpec((None, NSA_TQ, gw), lambda b, g, i: (b, i, g)),
                  pl.BlockSpec((None, NSA_TQ, LANES), lambda b, g, i: (b, i, g)),
                  whole(nc), whole(T), whole(T), const(ov), const(et)],
        out_specs=pl.BlockSpec((None, NSA_TQ, gw), lambda b, g, i: (b, i, g)),
        out_shape=jax.ShapeDtypeStruct((B, T, NSA_HEADS * LANES), BF16),
        scratch_shapes=[pltpu.VMEM((rows, LANES), F32)] * 4,
        compiler_params=_cparams(("parallel", "parallel", "arbitrary")),
        name="nsa",
    )(qn, gates, kvcmp, kvs, kvw, ov, et)


def _mla_body(q_ref, k_ref, v_ref, o_ref, m_scr, l_scr, acc_scr):
    i = pl.program_id(2)
    tt, nh = MLA_T, MLA_HPS
    head = lambda hh: slice(hh * LANES, (hh + 1) * LANES)
    pair = lambda hh: slice((hh // 2) * LANES, (hh // 2 + 1) * LANES)

    def scores(kt, hh):
        k0 = pl.multiple_of(kt * tt, tt)
        return _dot_t(q_ref[:, head(hh)], k_ref[pl.ds(k0, tt), head(hh)])

    def values(kt, hh):
        return v_ref[pl.ds(pl.multiple_of(kt * tt, tt), tt), pair(hh)]

    m_scr[...] = jnp.full(m_scr.shape, M_INIT, F32)
    l_scr[...] = jnp.zeros(l_scr.shape, F32)
    acc_scr[...] = jnp.zeros(acc_scr.shape, F32)
    rc = ROW_CHUNK

    def tile(k, diagonal):
        for hh in range(nh):
            s_all = scores(k, hh)
            ps, als = [], []
            for r0 in range(0, tt, rc):
                sl = slice(r0, r0 + rc)
                sc = s_all[sl]
                if diagonal:
                    sc = sc + jnp.where(_iota((rc, tt), 1) <= r0 + _iota((rc, tt), 0), 0.0, NEG)
                m_prev = m_scr[hh, sl]
                m_new = jnp.maximum(m_prev, jnp.max(sc, -1, keepdims=True))
                alpha = jnp.exp2(m_prev - m_new)
                p = jnp.exp2(sc - _rep_lanes(m_new, tt // LANES))
                l_scr[hh, sl] = alpha * l_scr[hh, sl] + jnp.sum(p, -1, keepdims=True)
                m_scr[hh, sl] = m_new
                ps.append(p.astype(BF16))
                als.append(alpha)
            acc_scr[hh] = jnp.concatenate(als, axis=0) * acc_scr[hh] + _dot(jnp.concatenate(ps, axis=0), values(k, hh))

    def body(k, carry):
        tile(k, False)
        return carry

    lax.fori_loop(0, i, body, 0)
    tile(i, True)
    outs = [acc_scr[hh] / l_scr[hh] for hh in range(nh)]
    first = _iota((tt, LANES), 1) < MLA_V_DIM
    for pp in range(nh // 2):
        o_ref[:, pp * LANES:(pp + 1) * LANES] = jnp.where(first, outs[2 * pp], outs[2 * pp + 1]).astype(BF16)


def _mla_call(qm, km, vm):
    B, T, _ = qm.shape
    nh = MLA_HPS
    assert T % MLA_T == 0 and MLA_HEADS % nh == 0 and nh % 2 == 0
    vw = nh * MLA_V_DIM
    return pl.pallas_call(
        _mla_body,
        grid=(B, MLA_HEADS // nh, T // MLA_T),
        in_specs=[pl.BlockSpec((None, MLA_T, nh * LANES), lambda b, p, i: (b, i, p)),
                  pl.BlockSpec((None, T, nh * LANES), lambda b, p, i: (b, 0, p)),
                  pl.BlockSpec((None, T, vw), lambda b, p, i: (b, 0, p))],
        out_specs=pl.BlockSpec((None, MLA_T, vw), lambda b, p, i: (b, i, p)),
        out_shape=jax.ShapeDtypeStruct((B, T, MLA_HEADS * MLA_V_DIM), BF16),
        scratch_shapes=[pltpu.VMEM((nh, MLA_T, LANES), F32)] * 3,
        compiler_params=_cparams(("parallel", "parallel", "arbitrary"), MLA_VMEM_LIMIT),
        name="mla",
    )(qm, km, vm)


def _out_body(x_ref, on_ref, om_ref, g_ref, wn_ref, wm_ref, lng_ref, lnb_ref, o_ref, *, alpha):
    y = _dot(on_ref[...], wn_ref[...]) + _dot(om_ref[...], wm_ref[...])
    z = alpha * x_ref[...] + (1.0 + g_ref[...]) * y
    o_ref[...] = _layer_norm(z, lng_ref[...], lnb_ref[...])


def _out_call(x, o_nsa, o_mla, g, w_on, w_om, lng, lnb, alpha):
    B, T, D = x.shape
    tm = min(T, 512)
    tok = lambda w: pl.BlockSpec((None, tm, w), lambda b, i: (b, i, 0))
    full = lambda a: pl.BlockSpec(a.shape, lambda b, i: (0,) * a.ndim)
    lng, lnb = lng.reshape(1, D), lnb.reshape(1, D)
    return pl.pallas_call(
        functools.partial(_out_body, alpha=alpha),
        grid=(B, T // tm),
        in_specs=[tok(D), tok(o_nsa.shape[-1]), tok(o_mla.shape[-1]),
                  pl.BlockSpec((None, 1, D), lambda b, i: (b, 0, 0)),
                  full(w_on), full(w_om), full(lng), full(lnb)],
        out_specs=tok(D),
        out_shape=jax.ShapeDtypeStruct((B, T, D), F32),
        compiler_params=_cparams(("parallel", "parallel")),
        name="out",
    )(x, o_nsa, o_mla, g, w_on, w_om, lng, lnb)


def _rope_tables(T, lane0, dim):
    half = dim // 2
    inv = ROPE_THETA ** (-jnp.arange(half, dtype=F32) / half)
    ang = jnp.arange(T).astype(F32)[:, None] * inv[None, :]
    cos, sin = jnp.cos(ang), jnp.sin(ang)
    c = jnp.ones((T, LANES), F32).at[:, lane0:lane0 + half].set(cos).at[:, lane0 + half:lane0 + dim].set(cos)
    z = jnp.zeros((T, LANES), F32)
    return c, z.at[:, lane0:lane0 + half].set(-sin), z.at[:, lane0 + half:lane0 + dim].set(sin)


def _pad_slots(w, n, width):
    K = w.shape[0]
    return jnp.pad(w.reshape(K, n, width), ((0, 0), (0, 0), (0, LANES - width))).reshape(K, n * LANES)


def _proj_weights(w_in):
    D = w_in.shape[0]
    hd, G = NSA_HEAD_DIM, NSA_KV_GROUPS
    nq, nkv = NSA_HEADS * hd, G * hd
    off = np.cumsum([0, nq] + [nkv] * 6 + [3 * NSA_HEADS, MLA_Q_RANK, MLA_KV_RANK, MLA_ROPE_DIM])
    cols = [_pad_slots(w_in[:, :nq], NSA_HEADS, hd)]
    for pair in range(3):
        ko, vo = off[1 + 2 * pair], off[2 + 2 * pair]
        for g in range(G):
            cols += [w_in[:, ko + g * hd:ko + (g + 1) * hd], w_in[:, vo + g * hd:vo + (g + 1) * hd]]
    per_g = 3 * NSA_HPG
    for g in range(G):
        cols.append(jnp.pad(w_in[:, off[7] + g * per_g:off[7] + (g + 1) * per_g], ((0, 0), (0, LANES - per_g))))
    cols += [w_in[:, off[8]:off[9]], w_in[:, off[9]:off[10]]]
    cols.append(jnp.pad(w_in[:, off[10]:off[11]],
                        ((0, 0), (MLA_NOPE_DIM, LANES - MLA_NOPE_DIM - MLA_ROPE_DIM))))
    w_all = jnp.concatenate(cols, axis=1).astype(BF16)
    assert w_all.shape == (D, PROJ_W)
    return w_all


def _cmp_weights(k_pos, k_w1, k_w2, v_pos, v_w1, v_w2):
    hd, st, hid = NSA_HEAD_DIM, CMP_STRIDE, CMP_HIDDEN

    def half(lo):
        w = jnp.zeros((st, LANES, 2 * hid), F32)
        w = w.at[:, :hd, :hid].set(k_w1[lo * hd:(lo + st) * hd].reshape(st, hd, hid))
        w = w.at[:, hd:, hid:].set(v_w1[lo * hd:(lo + st) * hd].reshape(st, hd, hid))
        pos = jnp.concatenate([k_pos[lo:lo + st], v_pos[lo:lo + st]], axis=1).reshape(1, st * LANES)
        return w.reshape(st * LANES, 2 * hid).astype(BF16), jnp.broadcast_to(pos, (8, st * LANES)).astype(BF16)

    w1a, pa = half(0)
    w1b, pb = half(st)
    w2 = jnp.zeros((2 * hid, LANES), F32).at[:hid, :hd].set(k_w2).at[hid:, hd:].set(v_w2).astype(BF16)
    return w1a, w1b, pa, pb, w2


def _block_tables(T):
    nc, n_slc = T // CMP_STRIDE, T // SLC_BLOCK
    cs = np.arange(nc)[:, None] * CMP_STRIDE
    blk = np.arange(LANES)[None, :]
    ov = (cs < blk * SLC_BLOCK + SLC_BLOCK) & (cs + CMP_BLOCK > blk * SLC_BLOCK) & (np.arange(nc)[:, None] < nc - 1) & (blk < n_slc)
    et = (np.arange(T)[:, None] // SLC_BLOCK) == blk
    return jnp.asarray(ov.astype(np.float32), BF16), jnp.asarray(et.astype(np.float32), BF16)


def kernel(x, c, w_ada, b_ada, ln_g, ln_b, ffn1_wg, ffn1_wu, ffn1_wd, w_in, cmp_k_pos, cmp_k_w1, cmp_k_w2,
           cmp_v_pos, cmp_v_w1, cmp_v_w2, mla_q_norm, mla_kv_norm, mla_w_uq, mla_w_ukv, w_out,
           ffn2_wg, ffn2_wu, ffn2_wd):
    B, T, D = x.shape
    depth = w_ada.shape[0]
    alpha = (2.0 * depth) ** 0.25
    tabs = _rope_tables(T, 0, NSA_ROPE_DIM) + _rope_tables(T, MLA_NOPE_DIM, MLA_ROPE_DIM)
    ov, et = _block_tables(T)
    for l in range(depth):
        mod = _mod_call(c, w_ada[l], b_ada[l])
        sh1, sc1, g1, sh2, sc2, g2, sh3, sc3, g3 = [mod[:, k] for k in range(N_MOD)]
        x = _ffn_call(x, sh1, sc1, g1, ffn1_wg[l].astype(BF16), ffn1_wu[l].astype(BF16), ffn1_wd[l].astype(BF16),
                      ln_g[l, 0], ln_b[l, 0], alpha, 0.5)

        wuq = _pad_slots(mla_w_uq[l], MLA_HEADS, MLA_NOPE_DIM + MLA_ROPE_DIM).astype(BF16)
        wukv = mla_w_ukv[l].reshape(MLA_KV_RANK, MLA_HEADS, MLA_NOPE_DIM + MLA_V_DIM)
        wuk = _pad_slots(wukv[:, :, :MLA_NOPE_DIM].reshape(MLA_KV_RANK, -1), MLA_HEADS, MLA_NOPE_DIM).astype(BF16)
        wuv = wukv[:, :, MLA_NOPE_DIM:].reshape(MLA_KV_RANK, -1).astype(BF16)
        qn, kvc, kvs, kvw, gates, qm, km, vm = _proj_call(
            x, sh2, sc2, _proj_weights(w_in[l]), mla_q_norm[l].reshape(1, -1), mla_kv_norm[l].reshape(1, -1),
            wuq, wuk, wuv, tabs)
        kvcmp = _cmp_call(kvc, *_cmp_weights(cmp_k_pos[l], cmp_k_w1[l], cmp_k_w2[l],
                                             cmp_v_pos[l], cmp_v_w1[l], cmp_v_w2[l]))
        o_nsa = _nsa_call(qn, gates, kvcmp, kvs, kvw, ov, et)
        o_mla = _mla_call(qm, km, vm)
        n_nsa = NSA_HEADS * NSA_HEAD_DIM
        w_on = jnp.pad(w_out[l, :n_nsa].reshape(NSA_HEADS, NSA_HEAD_DIM, D),
                       ((0, 0), (LANES - NSA_HEAD_DIM, 0), (0, 0))).reshape(NSA_HEADS * LANES, D).astype(BF16)
        x = _out_call(x, o_nsa, o_mla, g2, w_on, w_out[l, n_nsa:].astype(BF16), ln_g[l, 1], ln_b[l, 1], alpha)

        x = _ffn_call(x, sh3, sc3, g3, ffn2_wg[l].astype(BF16), ffn2_wu[l].astype(BF16), ffn2_wd[l].astype(BF16),
                      ln_g[l, 2], ln_b[l, 2], alpha, 0.5)
    return x
```

```python
import functools

import numpy as np
import jax
import jax.numpy as jnp
from jax import lax
from jax.experimental import pallas as pl
from jax.experimental.pallas import tpu as pltpu

F32 = jnp.float32
BF16 = jnp.bfloat16

ROPE_THETA = 500000.0
NSA_HEADS = 8
NSA_KV_GROUPS = 2
NSA_HPG = NSA_HEADS // NSA_KV_GROUPS
NSA_HEAD_DIM = 64
NSA_ROPE_DIM = 16
CMP_BLOCK = 32
CMP_STRIDE = 16
CMP_HIDDEN = 256
SLC_BLOCK = 64
SLC_TOPK = 16
WINDOW = 512
FORCE_SCORE = 1e9
MLA_HEADS = 8
MLA_NOPE_DIM = 64
MLA_ROPE_DIM = 32
MLA_V_DIM = 64
MLA_Q_RANK = 384
MLA_KV_RANK = 256
EPS = 1e-5
N_MOD = 9

LANES = 128
NEG = -1e30
M_INIT = -1e29
PICKED = -3e38
ROW_CHUNK = 128
LOG2E = 1.4426950408889634
VMEM_LIMIT = 48 * 1024 * 1024
MLA_VMEM_LIMIT = 56 * 1024 * 1024

NSA_TQ = 256
NSA_TK = 1024
MLA_T = 512
MLA_HPS = 4

OFF_Q, OFF_KV, OFF_GATE, OFF_CQ, OFF_CKV, OFF_KR, PROJ_W = 0, 512, 1280, 1536, 1920, 2176, 2304


def _cparams(sem, vmem=VMEM_LIMIT):
    return pltpu.CompilerParams(dimension_semantics=sem, vmem_limit_bytes=vmem)


def _layer_norm(z, g, b):
    mu = jnp.mean(z, -1, keepdims=True)
    zc = z - mu
    var = jnp.mean(zc * zc, -1, keepdims=True)
    return zc * lax.rsqrt(var + EPS) * g + b


def _dot(a, b):
    return jnp.dot(a, b, preferred_element_type=F32)


def _dot_t(a, b):
    return lax.dot_general(a, b, (((1,), (1,)), ((), ())), preferred_element_type=F32)


def _rep_rows(a, n):
    return jnp.concatenate([a] * n, axis=0)


def _rep_lanes(a, n):
    return jnp.concatenate([a] * n, axis=1)


def _iota(shape, d):
    return lax.broadcasted_iota(jnp.int32, shape, d)


def _mod_body(c_ref, w_ref, b_ref, o_ref):
    c = c_ref[...]
    s = (c * jax.nn.sigmoid(c)).astype(BF16)
    o_ref[...] = _dot(s, w_ref[...].astype(BF16)) + b_ref[...]


def _mod_call(c, w_ada, b_ada):
    B, D = c.shape
    rows = 8
    c8 = jnp.zeros((rows, D), F32).at[:B].set(c)
    out = pl.pallas_call(
        _mod_body,
        grid=(N_MOD,),
        in_specs=[pl.BlockSpec((rows, D), lambda j: (0, 0)),
                  pl.BlockSpec((D, D), lambda j: (0, j)),
                  pl.BlockSpec((1, D), lambda j: (0, j))],
        out_specs=pl.BlockSpec((rows, D), lambda j: (0, j)),
        out_shape=jax.ShapeDtypeStruct((rows, N_MOD * D), F32),
        compiler_params=_cparams(("parallel",)),
        name="mod",
    )(c8, w_ada, b_ada.reshape(1, N_MOD * D))
    return out[:B].reshape(B, N_MOD, 1, D)


def _ffn_body(x_ref, sh_ref, sc_ref, g_ref, wg_ref, wu_ref, wd_ref, lng_ref, lnb_ref, o_ref,
              u_scr, acc_scr, *, alpha, coef, n_ff):
    j = pl.program_id(2)

    @pl.when(j == 0)
    def _():
        u_scr[...] = (x_ref[...] * (1.0 + sc_ref[...]) + sh_ref[...]).astype(BF16)
        acc_scr[...] = jnp.zeros_like(acc_scr)

    u = u_scr[...]
    a = _dot(u, wg_ref[...])
    b = _dot(u, wu_ref[...])
    h = (a * jax.nn.sigmoid(a)) * b
    acc_scr[...] += _dot(h.astype(BF16), wd_ref[...])

    @pl.when(j == n_ff - 1)
    def _():
        z = alpha * x_ref[...] + coef * (1.0 + g_ref[...]) * acc_scr[...]
        o_ref[...] = _layer_norm(z, lng_ref[...], lnb_ref[...])


def _ffn_call(x, sh, sc, g, wg, wu, wd, lng, lnb, alpha, coef):
    B, T, D = x.shape
    FF = wg.shape[1]
    tm = min(T, 1024)
    tf = 256
    assert T % tm == 0 and FF % tf == 0
    n_ff = FF // tf
    row = pl.BlockSpec((None, 1, D), lambda b, i, j: (b, 0, 0))
    vec = pl.BlockSpec((1, D), lambda b, i, j: (0, 0))
    return pl.pallas_call(
        functools.partial(_ffn_body, alpha=alpha, coef=coef, n_ff=n_ff),
        grid=(B, T // tm, n_ff),
        in_specs=[pl.BlockSpec((None, tm, D), lambda b, i, j: (b, i, 0)), row, row, row,
                  pl.BlockSpec((D, tf), lambda b, i, j: (0, j)),
                  pl.BlockSpec((D, tf), lambda b, i, j: (0, j)),
                  pl.BlockSpec((tf, D), lambda b, i, j: (j, 0)), vec, vec],
        out_specs=pl.BlockSpec((None, tm, D), lambda b, i, j: (b, i, 0)),
        out_shape=jax.ShapeDtypeStruct((B, T, D), F32),
        scratch_shapes=[pltpu.VMEM((tm, D), BF16), pltpu.VMEM((tm, D), F32)],
        compiler_params=_cparams(("parallel", "parallel", "arbitrary")),
        name="ffn",
    )(x, sh, sc, g, wg, wu, wd, lng.reshape(1, D), lnb.reshape(1, D))


def _rope_slot(x, c, sa, sb, half):
    return x * c + pltpu.roll(x, LANES - half, 1) * sa + pltpu.roll(x, half, 1) * sb


def _rms_norm(x, g):
    return x * lax.rsqrt(jnp.mean(x * x, -1, keepdims=True) + EPS) * g


def _proj_body(x_ref, sh_ref, sc_ref, w_ref, gq_ref, gkv_ref, wuq_ref, wuk_ref, wuv_ref,
               cq_ref, sqa_ref, sqb_ref, cn_ref, sna_ref, snb_ref, cm_ref, sma_ref, smb_ref,
               qn_ref, kvc_ref, kvs_ref, kvw_ref, gate_ref, qm_ref, km_ref, vm_ref,
               *, nsa_scale, mla_scale):
    u = (x_ref[...] * (1.0 + sc_ref[...]) + sh_ref[...]).astype(BF16)
    h = _dot(u, w_ref[...])
    cq, sqa, sqb = cq_ref[...], sqa_ref[...], sqb_ref[...]
    cn, sna, snb = cn_ref[...], sna_ref[...], snb_ref[...]
    cm, sma, smb = cm_ref[...], sma_ref[...], smb_ref[...]
    nh = NSA_ROPE_DIM // 2
    mh = MLA_ROPE_DIM // 2

    for s in range(NSA_HEADS * NSA_HEAD_DIM // LANES):
        lo = OFF_Q + s * LANES
        qn_ref[:, s * LANES:(s + 1) * LANES] = (
            _rope_slot(h[:, lo:lo + LANES], cq, sqa, sqb, nh) * nsa_scale).astype(BF16)
    for idx, ref in enumerate((kvc_ref, kvs_ref, kvw_ref)):
        for g in range(NSA_KV_GROUPS):
            lo = OFF_KV + (idx * NSA_KV_GROUPS + g) * LANES
            ref[g] = _rope_slot(h[:, lo:lo + LANES], cn, sna, snb, nh).astype(BF16)
    gate_ref[...] = jax.nn.sigmoid(h[:, OFF_GATE:OFF_GATE + NSA_KV_GROUPS * LANES])

    cqn = _rms_norm(h[:, OFF_CQ:OFF_CQ + MLA_Q_RANK], gq_ref[...]).astype(BF16)
    qm = _dot(cqn, wuq_ref[...])
    ckvn = _rms_norm(h[:, OFF_CKV:OFF_CKV + MLA_KV_RANK], gkv_ref[...]).astype(BF16)
    kn = _dot(ckvn, wuk_ref[...])
    vm_ref[...] = _dot(ckvn, wuv_ref[...]).astype(BF16)
    kr = _rope_slot(h[:, OFF_KR:OFF_KR + LANES], cm, sma, smb, mh)
    for s in range(MLA_HEADS):
        sl = slice(s * LANES, (s + 1) * LANES)
        qm_ref[:, sl] = (_rope_slot(qm[:, sl], cm, sma, smb, mh) * mla_scale).astype(BF16)
        km_ref[:, sl] = (kn[:, sl] + kr).astype(BF16)


def _proj_call(x, sh, sc, w_all, gq, gkv, wuq, wuk, wuv, tabs):
    B, T, D = x.shape
    tm = min(T, 256)
    G = NSA_KV_GROUPS
    row = pl.BlockSpec((None, 1, D), lambda b, i: (b, 0, 0))

    def full(a):
        return pl.BlockSpec(a.shape, lambda b, i: (0,) * a.ndim)

    tab = pl.BlockSpec((tm, LANES), lambda b, i: (i, 0))
    tok = lambda w: pl.BlockSpec((None, tm, w), lambda b, i: (b, i, 0))
    grp = pl.BlockSpec((G, None, tm, LANES), lambda b, i: (0, b, i, 0))
    sd = jax.ShapeDtypeStruct
    return pl.pallas_call(
        functools.partial(_proj_body, nsa_scale=NSA_HEAD_DIM ** -0.5 * LOG2E,
                          mla_scale=(MLA_NOPE_DIM + MLA_ROPE_DIM) ** -0.5 * LOG2E),
        grid=(B, T // tm),
        in_specs=[tok(D), row, row, full(w_all), full(gq), full(gkv), full(wuq), full(wuk), full(wuv)]
                 + [tab] * len(tabs),
        out_specs=[tok(NSA_HEADS * NSA_HEAD_DIM), grp, grp, grp, tok(G * LANES),
                   tok(MLA_HEADS * LANES), tok(MLA_HEADS * LANES), tok(MLA_HEADS * MLA_V_DIM)],
        out_shape=[sd((B, T, NSA_HEADS * NSA_HEAD_DIM), BF16), sd((G, B, T, LANES), BF16),
                   sd((G, B, T, LANES), BF16), sd((G, B, T, LANES), BF16), sd((B, T, G * LANES), F32),
                   sd((B, T, MLA_HEADS * LANES), BF16), sd((B, T, MLA_HEADS * LANES), BF16),
                   sd((B, T, MLA_HEADS * MLA_V_DIM), BF16)],
        compiler_params=_cparams(("parallel", "parallel")),
        name="proj",
    )(x, sh, sc, w_all, gq, gkv, wuq, wuk, wuv, *tabs)


def _gelu_tanh(x):
    return x * (0.5 * (1.0 + jnp.tanh(0.7978845608028654 * (x + 0.044715 * (x * x * x)))))


def _cmp_body(c_ref, w1a_ref, w1b_ref, pa_ref, pb_ref, w2_ref, o_ref):
    c = c_ref[...]
    nc = c.shape[0]
    w1a, w1b = w1a_ref[...], w1b_ref[...]
    a = _dot(c, w1a)
    b = _dot(c, w1b)
    bias = _dot(pa_ref[...], w1a) + _dot(pb_ref[...], w1b)
    pre = a + pltpu.roll(b, nc - 1, 0) + bias[0:1]
    o_ref[...] = _dot(_gelu_tanh(pre).astype(BF16), w2_ref[...]).astype(BF16)


def _cmp_call(kvc, w1a, w1b, pa, pb, w2):
    G, B, T, _ = kvc.shape
    nc = T // CMP_STRIDE
    c = kvc.reshape(G, B, nc, CMP_STRIDE * LANES)

    def full(a):
        return pl.BlockSpec(a.shape, lambda g, b: (0,) * a.ndim)

    return pl.pallas_call(
        _cmp_body,
        grid=(G, B),
        in_specs=[pl.BlockSpec((None, None, nc, CMP_STRIDE * LANES), lambda g, b: (g, b, 0, 0)),
                  full(w1a), full(w1b), full(pa), full(pb), full(w2)],
        out_specs=pl.BlockSpec((None, None, nc, LANES), lambda g, b: (g, b, 0, 0)),
        out_shape=jax.ShapeDtypeStruct((G, B, nc, LANES), BF16),
        compiler_params=_cparams(("parallel", "parallel")),
        name="cmp",
    )(c, w1a, w1b, pa, pb, w2)


def _nsa_body(q_ref, gate_ref, kvc_ref, kvs_ref, kvw_ref, ov_ref, et_ref, o_ref,
              m_scr, l_scr, acc_scr, part_scr, *, n_slc):
    i = pl.program_id(2)
    nc = kvc_ref.shape[0]
    hpg, tq_n, rows, tk = NSA_HPG, NSA_TQ, NSA_HPG * NSA_TQ, NSA_TK
    t0 = pl.multiple_of(i * tq_n, tq_n)
    q = q_ref[...].astype(F32)
    low = _iota((tq_n, LANES), 1) < NSA_HEAD_DIM
    heads = []
    for h in range(hpg):
        tile_ = q[:, (h // 2) * LANES:(h // 2 + 1) * LANES]
        heads.append(jnp.where(low, tile_ if h % 2 == 0 else pltpu.roll(tile_, NSA_HEAD_DIM, 1), 0.0))
    qs = jnp.concatenate(heads, axis=0).astype(BF16)

    wk = WINDOW + tq_n
    start = pl.multiple_of(jnp.maximum(t0 - WINDOW, 0), tq_n)
    kvw = kvw_ref[pl.ds(start, wk), :]
    s_w = _dot_t(qs, kvw)
    diff = (t0 + _iota((tq_n, wk), 0)) - (start + _iota((tq_n, wk), 1))
    s_w = s_w + _rep_rows(jnp.where((diff >= 0) & (diff < WINDOW), 0.0, NEG), hpg)
    e_w = jnp.exp2(s_w - jnp.max(s_w, -1, keepdims=True))
    o_win = _dot(e_w.astype(BF16), kvw) / jnp.sum(e_w, -1, keepdims=True)

    kvc = kvc_ref[...]
    s = _dot_t(qs, kvc)
    tq = t0 + _iota((tq_n, nc), 0)
    n = _iota((tq_n, nc), 1)
    valid = (n * CMP_STRIDE + (CMP_BLOCK - 1) <= tq) & (n < nc - 1)
    s = s + _rep_rows(jnp.where(valid, 0.0, NEG), hpg)
    e = jnp.exp2(s - jnp.max(s, -1, keepdims=True)) * _rep_rows(jnp.where(valid, 1.0, 0.0), hpg)
    p = e / jnp.maximum(jnp.sum(e, -1, keepdims=True), 1e-30)
    o_cmp = _dot(p.astype(BF16), kvc)

    ps = p[0:tq_n] + p[tq_n:2 * tq_n] + p[2 * tq_n:3 * tq_n] + p[3 * tq_n:4 * tq_n]
    ph = ps.astype(BF16)
    plo = (ps - ph.astype(F32)).astype(BF16)
    ov = ov_ref[...]
    imp_t = (_dot(ph, ov) + _dot(plo, ov)).T

    jr = _iota((LANES, tq_n), 0)
    tl = t0 + _iota((LANES, tq_n), 1)
    cur = lax.shift_right_logical(tl, 6)
    causal = jr * SLC_BLOCK <= tl
    forced = ((jr == 0) | (jr == cur) | (jr == cur - 1)) & causal
    v = jnp.where(forced, PICKED, jnp.where(causal, imp_t, -1.0))
    if n_slc < LANES:
        v = jnp.where(jr < n_slc, v, PICKED)
    jf = jr.astype(F32)
    sel_t = jnp.where(forced, 1.0, 0.0)
    n_forced = 3
    for _ in range(SLC_TOPK - n_forced):
        mx = jnp.max(v, axis=0, keepdims=True)
        idx = jnp.min(jnp.where(v == mx, jf, float(LANES)), axis=0, keepdims=True)
        hit = jf == idx
        sel_t = jnp.where(hit, 1.0, sel_t)
        v = jnp.where(hit, PICKED, v)
    sel = sel_t.T

    gt = gate_ref[...]
    gcol = lambda br: jnp.concatenate([gt[:, h * 3 + br:h * 3 + br + 1] for h in range(hpg)], axis=0)
    part_scr[...] = gcol(0) * o_cmp + gcol(2) * o_win

    upto = _iota((tq_n, LANES), 1) < lax.shift_right_logical(t0 + tq_n, 6)
    lhs = jnp.concatenate([qs, _rep_rows(jnp.where((sel > 0.5) & upto, 0.0, NEG).astype(BF16), hpg)], axis=1)
    n_kt = lax.shift_right_logical(t0 + tq_n + tk - 1, tk.bit_length() - 1)

    def keys(kt):
        return kvs_ref[pl.ds(pl.multiple_of(kt * tk, tk), tk), :]

    def scores(kt):
        k0 = pl.multiple_of(kt * tk, tk)
        return _dot_t(lhs, jnp.concatenate([kvs_ref[pl.ds(k0, tk), :], et_ref[pl.ds(k0, tk), :]], axis=1))

    m_scr[...] = jnp.full(m_scr.shape, M_INIT, F32)
    l_scr[...] = jnp.zeros(l_scr.shape, F32)
    acc_scr[...] = jnp.zeros(acc_scr.shape, F32)
    rc = ROW_CHUNK

    def tile(k, causal):
        k0 = pl.multiple_of(k * tk, tk)
        kv = kvs_ref[pl.ds(k0, tk), :]
        rhs = jnp.concatenate([kv, et_ref[pl.ds(k0, tk), :]], axis=1)
        if causal:
            kpos = k * tk + _iota((rc, tk), 1)
        s_all = _dot_t(lhs, rhs)
        ps, als = [], []
        for r in range(0, rows, rc):
            sl = slice(r, r + rc)
            sc = s_all[sl]
            if causal:
                sc = sc + jnp.where(kpos <= t0 + (r % tq_n) + _iota((rc, tk), 0), 0.0, NEG)
            m_prev = m_scr[sl]
            m_new = jnp.maximum(m_prev, jnp.max(sc, -1, keepdims=True))
            alpha = jnp.exp2(m_prev - m_new)
            pp = jnp.exp2(sc - _rep_lanes(m_new, tk // LANES))
            l_scr[sl] = alpha * l_scr[sl] + jnp.sum(pp, -1, keepdims=True)
            m_scr[sl] = m_new
            ps.append(pp.astype(BF16))
            als.append(alpha)
        acc_scr[...] = jnp.concatenate(als, axis=0) * acc_scr[...] + _dot(jnp.concatenate(ps, axis=0), kv)

    def body(k, carry):
        tile(k, False)
        return carry

    lax.fori_loop(0, n_kt - 1, body, 0)
    tile(n_kt - 1, True)
    acc = acc_scr[...]
    out = part_scr[...] + gcol(1) * (acc / l_scr[...])
    out = jnp.where(_iota((rows, LANES), 1) >= NSA_HEAD_DIM, out, 0.0)
    for hp in range(hpg // 2):
        even = out[2 * hp * tq_n:(2 * hp + 1) * tq_n]
        odd = out[(2 * hp + 1) * tq_n:(2 * hp + 2) * tq_n]
        o_ref[:, hp * LANES:(hp + 1) * LANES] = (pltpu.roll(even, NSA_HEAD_DIM, 1) + odd).astype(BF16)


def _nsa_call(qn, gates, kvcmp, kvs, kvw, ov, et):
    B, T, _ = qn.shape
    G = NSA_KV_GROUPS
    nc = kvcmp.shape[2]
    n_slc = T // SLC_BLOCK
    assert SLC_TOPK <= n_slc <= LANES and nc % LANES == 0 and T % NSA_TK == 0 and T >= WINDOW + NSA_TQ
    assert NSA_TK & (NSA_TK - 1) == 0 and NSA_TQ % SLC_BLOCK == 0 and NSA_TK % NSA_TQ == 0
    gw = NSA_HPG * NSA_HEAD_DIM
    rows = NSA_HPG * NSA_TQ
    whole = lambda r: pl.BlockSpec((None, None, r, LANES), lambda b, g, i: (g, b, 0, 0))
    const = lambda a: pl.BlockSpec(a.shape, lambda b, g, i: (0, 0))
    return pl.pallas_call(
        functools.partial(_nsa_body, n_slc=n_slc),
        grid=(B, G, T // NSA_TQ),
        in_specs=[pl.BlockSpec((None, NSA_TQ, gw), lambda b, g, i: (b, i, g)),
                  pl.BlockSpec((None, NSA_TQ, LANES), lambda b, g, i: (b, i, g)),
                  whole(nc), whole(T), whole(T), const(ov), const(et)],
        out_specs=pl.BlockSpec((None, NSA_TQ, NSA_HPG * NSA_HEAD_DIM), lambda b, g, i: (b, i, g)),
        out_shape=jax.ShapeDtypeStruct((B, T, NSA_HEADS * NSA_HEAD_DIM), BF16),
        scratch_shapes=[pltpu.VMEM((rows, LANES), F32)] * 4,
        compiler_params=_cparams(("parallel", "parallel", "arbitrary")),
        name="nsa",
    )(qn, gates, kvcmp, kvs, kvw, ov, et)


def _mla_body(q_ref, k_ref, v_ref, o_ref, p_buf, a_buf, m_scr, l_scr, acc_scr):
    i = pl.program_id(2)
    tt, nh = MLA_T, MLA_HPS
    head = lambda hh: slice(hh * LANES, (hh + 1) * LANES)

    pair = lambda hh: slice((hh // 2) * LANES, (hh // 2 + 1) * LANES)

    def scores(kt, hh):
        k0 = pl.multiple_of(kt * tt, tt)
        return _dot_t(q_ref[:, head(hh)], k_ref[pl.ds(k0, tt), head(hh)])

    def values(kt, hh):
        return v_ref[pl.ds(pl.multiple_of(kt * tt, tt), tt), pair(hh)]

    m_scr[...] = jnp.full(m_scr.shape, M_INIT, F32)
    l_scr[...] = jnp.zeros(l_scr.shape, F32)
    acc_scr[...] = jnp.zeros(acc_scr.shape, F32)
    p_buf[...] = jnp.zeros(p_buf.shape, BF16)
    a_buf[...] = jnp.ones(a_buf.shape, F32)

    def stage(k, diagonal):
        kp = jnp.maximum(k - 1, 0)
        if diagonal:
            tri = jnp.where(_iota((tt, tt), 1) <= _iota((tt, tt), 0), 0.0, NEG)
        for hh in range(nh):
            s = scores(k, hh)
            if diagonal:
                s = s + tri
            acc_scr[hh] = a_buf[hh] * acc_scr[hh] + _dot(p_buf[hh], values(kp, hh))
            m_prev = m_scr[hh]
            m_new = jnp.maximum(m_prev, jnp.max(s, -1, keepdims=True))
            alpha = jnp.exp2(m_prev - m_new)
            p = jnp.exp2(s - _rep_lanes(m_new, tt // LANES))
            l_scr[hh] = alpha * l_scr[hh] + jnp.sum(p, -1, keepdims=True)
            m_scr[hh] = m_new
            a_buf[hh] = alpha
            p_buf[hh] = p.astype(BF16)

    def body(k, carry):
        stage(k, False)
        return carry

    lax.fori_loop(0, i, body, 0)
    stage(i, True)
    outs = [(a_buf[hh] * acc_scr[hh] + _dot(p_buf[hh], values(i, hh))) / l_scr[hh] for hh in range(nh)]
    first = _iota((tt, LANES), 1) < MLA_V_DIM
    for pp in range(nh // 2):
        o_ref[:, pp * LANES:(pp + 1) * LANES] = jnp.where(first, outs[2 * pp], outs[2 * pp + 1]).astype(BF16)


def _mla_call(qm, km, vm):
    B, T, _ = qm.shape
    nh = MLA_HPS
    assert T % MLA_T == 0 and MLA_HEADS % nh == 0 and nh % 2 == 0
    vw = nh * MLA_V_DIM
    return pl.pallas_call(
        _mla_body,
        grid=(B, MLA_HEADS // nh, T // MLA_T),
        in_specs=[pl.BlockSpec((None, MLA_T, nh * LANES), lambda b, p, i: (b, i, p)),
                  pl.BlockSpec((None, T, nh * LANES), lambda b, p, i: (b, 0, p)),
                  pl.BlockSpec((None, T, vw), lambda b, p, i: (b, 0, p))],
        out_specs=pl.BlockSpec((None, MLA_T, vw), lambda b, p, i: (b, i, p)),
        out_shape=jax.ShapeDtypeStruct((B, T, MLA_HEADS * MLA_V_DIM), BF16),
        scratch_shapes=[pltpu.VMEM((nh, MLA_T, MLA_T), BF16)] + [pltpu.VMEM((nh, MLA_T, LANES), F32)] * 4,
        compiler_params=_cparams(("parallel", "parallel", "arbitrary"), MLA_VMEM_LIMIT),
        name="mla",
    )(qm, km, vm)


def _out_body(x_ref, on_ref, om_ref, g_ref, wn_ref, wm_ref, lng_ref, lnb_ref, o_ref, *, alpha):
    y = _dot(on_ref[...], wn_ref[...]) + _dot(om_ref[...], wm_ref[...])
    z = alpha * x_ref[...] + (1.0 + g_ref[...]) * y
    o_ref[...] = _layer_norm(z, lng_ref[...], lnb_ref[...])


def _out_call(x, o_nsa, o_mla, g, w_on, w_om, lng, lnb, alpha):
    B, T, D = x.shape
    tm = min(T, 512)
    tok = lambda w: pl.BlockSpec((None, tm, w), lambda b, i: (b, i, 0))
    full = lambda a: pl.BlockSpec(a.shape, lambda b, i: (0,) * a.ndim)
    lng, lnb = lng.reshape(1, D), lnb.reshape(1, D)
    return pl.pallas_call(
        functools.partial(_out_body, alpha=alpha),
        grid=(B, T // tm),
        in_specs=[tok(D), tok(o_nsa.shape[-1]), tok(o_mla.shape[-1]),
                  pl.BlockSpec((None, 1, D), lambda b, i: (b, 0, 0)),
                  full(w_on), full(w_om), full(lng), full(lnb)],
        out_specs=tok(D),
        out_shape=jax.ShapeDtypeStruct((B, T, D), F32),
        compiler_params=_cparams(("parallel", "parallel")),
        name="out",
    )(x, o_nsa, o_mla, g, w_on, w_om, lng, lnb)


def _rope_tables(T, lane0s, dim):
    half = dim // 2
    inv = ROPE_THETA ** (-jnp.arange(half, dtype=F32) / half)
    ang = jnp.arange(T).astype(F32)[:, None] * inv[None, :]
    cos, sin = jnp.cos(ang), jnp.sin(ang)
    place = np.zeros((2, half, LANES), np.float32)
    for lane0 in lane0s:
        place[0, np.arange(half), lane0 + np.arange(half)] = 1.0
        place[1, np.arange(half), lane0 + half + np.arange(half)] = 1.0
    spread = lambda t, p: jnp.dot(t, jnp.asarray(p), precision=lax.Precision.HIGHEST)
    outside = jnp.asarray(1.0 - place.sum(axis=(0, 1)))[None, :]
    return spread(cos, place[0] + place[1]) + outside, spread(-sin, place[0]), spread(sin, place[1])


def _pad_slots(w, n, width):
    K = w.shape[0]
    return jnp.pad(w.reshape(K, n, width), ((0, 0), (0, 0), (0, LANES - width))).reshape(K, n * LANES)


def _proj_weights(w_in):
    D = w_in.shape[0]
    hd, G = NSA_HEAD_DIM, NSA_KV_GROUPS
    nq, nkv = NSA_HEADS * hd, G * hd
    off = np.cumsum([0, nq] + [nkv] * 6 + [3 * NSA_HEADS, MLA_Q_RANK, MLA_KV_RANK, MLA_ROPE_DIM])
    cols = [w_in[:, :nq]]
    for pair in range(3):
        ko, vo = off[1 + 2 * pair], off[2 + 2 * pair]
        for g in range(G):
            cols += [w_in[:, ko + g * hd:ko + (g + 1) * hd], w_in[:, vo + g * hd:vo + (g + 1) * hd]]
    per_g = 3 * NSA_HPG
    for g in range(G):
        cols.append(jnp.pad(w_in[:, off[7] + g * per_g:off[7] + (g + 1) * per_g], ((0, 0), (0, LANES - per_g))))
    cols += [w_in[:, off[8]:off[9]], w_in[:, off[9]:off[10]]]
    cols.append(jnp.pad(w_in[:, off[10]:off[11]],
                        ((0, 0), (MLA_NOPE_DIM, LANES - MLA_NOPE_DIM - MLA_ROPE_DIM))))
    w_all = jnp.concatenate(cols, axis=1).astype(BF16)
    assert w_all.shape == (D, PROJ_W)
    return w_all


def _cmp_weights(k_pos, k_w1, k_w2, v_pos, v_w1, v_w2):
    hd, st, hid = NSA_HEAD_DIM, CMP_STRIDE, CMP_HIDDEN

    def half(lo):
        w = jnp.zeros((st, LANES, 2 * hid), F32)
        w = w.at[:, :hd, :hid].set(k_w1[lo * hd:(lo + st) * hd].reshape(st, hd, hid))
        w = w.at[:, hd:, hid:].set(v_w1[lo * hd:(lo + st) * hd].reshape(st, hd, hid))
        pos = jnp.concatenate([k_pos[lo:lo + st], v_pos[lo:lo + st]], axis=1).reshape(1, st * LANES)
        return w.reshape(st * LANES, 2 * hid).astype(BF16), jnp.broadcast_to(pos, (8, st * LANES)).astype(BF16)

    w1a, pa = half(0)
    w1b, pb = half(st)
    w2 = jnp.zeros((2 * hid, LANES), F32).at[:hid, :hd].set(k_w2).at[hid:, hd:].set(v_w2).astype(BF16)
    return w1a, w1b, pa, pb, w2


def _block_tables(T):
    nc, n_slc = T // CMP_STRIDE, T // SLC_BLOCK
    cs = np.arange(nc)[:, None] * CMP_STRIDE
    blk = np.arange(LANES)[None, :]
    ov = (cs < blk * SLC_BLOCK + SLC_BLOCK) & (cs + CMP_BLOCK > blk * SLC_BLOCK) & (np.arange(nc)[:, None] < nc - 1) & (blk < n_slc)
    et = (np.arange(T)[:, None] // SLC_BLOCK) == blk
    return jnp.asarray(ov.astype(np.float32), BF16), jnp.asarray(et.astype(np.float32), BF16)


def kernel(x, c, w_ada, b_ada, ln_g, ln_b, ffn1_wg, ffn1_wu, ffn1_wd, w_in, cmp_k_pos, cmp_k_w1, cmp_k_w2,
           cmp_v_pos, cmp_v_w1, cmp_v_w2, mla_q_norm, mla_kv_norm, mla_w_uq, mla_w_ukv, w_out,
           ffn2_wg, ffn2_wu, ffn2_wd):
    B, T, D = x.shape
    depth = w_ada.shape[0]
    alpha = (2.0 * depth) ** 0.25
    tabs = (_rope_tables(T, (0, NSA_HEAD_DIM), NSA_ROPE_DIM) + _rope_tables(T, (0,), NSA_ROPE_DIM)
            + _rope_tables(T, (MLA_NOPE_DIM,), MLA_ROPE_DIM))
    ov, et = _block_tables(T)
    for l in range(depth):
        mod = _mod_call(c, w_ada[l], b_ada[l])
        sh1, sc1, g1, sh2, sc2, g2, sh3, sc3, g3 = [mod[:, k] for k in range(N_MOD)]
        x = _ffn_call(x, sh1, sc1, g1, ffn1_wg[l].astype(BF16), ffn1_wu[l].astype(BF16), ffn1_wd[l].astype(BF16),
                      ln_g[l, 0], ln_b[l, 0], alpha, 0.5)

        wuq = _pad_slots(mla_w_uq[l], MLA_HEADS, MLA_NOPE_DIM + MLA_ROPE_DIM).astype(BF16)
        wukv = mla_w_ukv[l].reshape(MLA_KV_RANK, MLA_HEADS, MLA_NOPE_DIM + MLA_V_DIM)
        wuk = _pad_slots(wukv[:, :, :MLA_NOPE_DIM].reshape(MLA_KV_RANK, -1), MLA_HEADS, MLA_NOPE_DIM).astype(BF16)
        wuv = wukv[:, :, MLA_NOPE_DIM:].reshape(MLA_KV_RANK, -1).astype(BF16)
        qn, kvc, kvs, kvw, gates, qm, km, vm = _proj_call(
            x, sh2, sc2, _proj_weights(w_in[l]), mla_q_norm[l].reshape(1, -1), mla_kv_norm[l].reshape(1, -1),
            wuq, wuk, wuv, tabs)
        kvcmp = _cmp_call(kvc, *_cmp_weights(cmp_k_pos[l], cmp_k_w1[l], cmp_k_w2[l],
                                             cmp_v_pos[l], cmp_v_w1[l], cmp_v_w2[l]))
        o_nsa = _nsa_call(qn, gates, kvcmp, kvs, kvw, ov, et)
        o_mla = _mla_call(qm, km, vm)
        n_nsa = NSA_HEADS * NSA_HEAD_DIM
        w_o = w_out[l].astype(BF16)
        x = _out_call(x, o_nsa, o_mla, g2, w_o[:n_nsa], w_o[n_nsa:], ln_g[l, 1], ln_b[l, 1], alpha)

        x = _ffn_call(x, sh3, sc3, g3, ffn2_wg[l].astype(BF16), ffn2_wu[l].astype(BF16), ffn2_wd[l].astype(BF16),
                      ln_g[l, 2], ln_b[l, 2], alpha, 0.5)
    return x
```

```python
import functools

import numpy as np
import jax
import jax.numpy as jnp
from jax import lax
from jax.experimental import pallas as pl
from jax.experimental.pallas import tpu as pltpu

F32 = jnp.float32
BF16 = jnp.bfloat16

ROPE_THETA = 500000.0
NSA_HEADS = 8
NSA_KV_GROUPS = 2
NSA_HPG = NSA_HEADS // NSA_KV_GROUPS
NSA_HEAD_DIM = 64
NSA_ROPE_DIM = 16
CMP_BLOCK = 32
CMP_STRIDE = 16
CMP_HIDDEN = 256
SLC_BLOCK = 64
SLC_TOPK = 16
WINDOW = 512
FORCE_SCORE = 1e9
MLA_HEADS = 8
MLA_NOPE_DIM = 64
MLA_ROPE_DIM = 32
MLA_V_DIM = 64
MLA_Q_RANK = 384
MLA_KV_RANK = 256
EPS = 1e-5
N_MOD = 9

LANES = 128
NEG = -1e30
M_INIT = -1e29
PICKED = -3e38
ROW_CHUNK = 128
LOG2E = 1.4426950408889634
VMEM_LIMIT = 48 * 1024 * 1024
MLA_VMEM_LIMIT = 56 * 1024 * 1024

NSA_TQ = 256
NSA_TK = 1024
MLA_T = 512
MLA_HPS = 4

OFF_Q, OFF_KV, OFF_GATE, OFF_CQ, OFF_CKV, OFF_KR, PROJ_W = 0, 512, 1280, 1536, 1920, 2176, 2304


def _cparams(sem, vmem=VMEM_LIMIT):
    return pltpu.CompilerParams(dimension_semantics=sem, vmem_limit_bytes=vmem)


def _layer_norm(z, g, b):
    mu = jnp.mean(z, -1, keepdims=True)
    zc = z - mu
    var = jnp.mean(zc * zc, -1, keepdims=True)
    return zc * lax.rsqrt(var + EPS) * g + b


def _dot(a, b):
    return jnp.dot(a, b, preferred_element_type=F32)


def _dot_t(a, b):
    return lax.dot_general(a, b, (((1,), (1,)), ((), ())), preferred_element_type=F32)


def _rep_rows(a, n):
    return jnp.concatenate([a] * n, axis=0)


def _rep_lanes(a, n):
    return jnp.concatenate([a] * n, axis=1)


def _iota(shape, d):
    return lax.broadcasted_iota(jnp.int32, shape, d)


def _mod_body(c_ref, w_ref, b_ref, o_ref):
    c = c_ref[...]
    s = (c * jax.nn.sigmoid(c)).astype(BF16)
    o_ref[...] = _dot(s, w_ref[...].astype(BF16)) + b_ref[...]


def _mod_call(c, w_ada, b_ada):
    B, D = c.shape
    rows = 8
    c8 = jnp.zeros((rows, D), F32).at[:B].set(c)
    out = pl.pallas_call(
        _mod_body,
        grid=(N_MOD,),
        in_specs=[pl.BlockSpec((rows, D), lambda j: (0, 0)),
                  pl.BlockSpec((D, D), lambda j: (0, j)),
                  pl.BlockSpec((1, D), lambda j: (0, j))],
        out_specs=pl.BlockSpec((rows, D), lambda j: (0, j)),
        out_shape=jax.ShapeDtypeStruct((rows, N_MOD * D), F32),
        compiler_params=_cparams(("parallel",)),
        name="mod",
    )(c8, w_ada, b_ada.reshape(1, N_MOD * D))
    return out[:B].reshape(B, N_MOD, 1, D)


def _ffn_body(x_ref, sh_ref, sc_ref, g_ref, wg_ref, wu_ref, wd_ref, lng_ref, lnb_ref, o_ref,
              u_scr, acc_scr, *, alpha, coef, n_ff):
    j = pl.program_id(2)

    @pl.when(j == 0)
    def _():
        u_scr[...] = (x_ref[...] * (1.0 + sc_ref[...]) + sh_ref[...]).astype(BF16)
        acc_scr[...] = jnp.zeros_like(acc_scr)

    u = u_scr[...]
    a = _dot(u, wg_ref[...])
    b = _dot(u, wu_ref[...])
    h = (a * jax.nn.sigmoid(a)) * b
    acc_scr[...] += _dot(h.astype(BF16), wd_ref[...])

    @pl.when(j == n_ff - 1)
    def _():
        z = alpha * x_ref[...] + coef * (1.0 + g_ref[...]) * acc_scr[...]
        o_ref[...] = _layer_norm(z, lng_ref[...], lnb_ref[...])


def _ffn_call(x, sh, sc, g, wg, wu, wd, lng, lnb, alpha, coef):
    B, T, D = x.shape
    FF = wg.shape[1]
    tm = min(T, 1024)
    tf = 256
    assert T % tm == 0 and FF % tf == 0
    n_ff = FF // tf
    row = pl.BlockSpec((None, 1, D), lambda b, i, j: (b, 0, 0))
    vec = pl.BlockSpec((1, D), lambda b, i, j: (0, 0))
    return pl.pallas_call(
        functools.partial(_ffn_body, alpha=alpha, coef=coef, n_ff=n_ff),
        grid=(B, T // tm, n_ff),
        in_specs=[pl.BlockSpec((None, tm, D), lambda b, i, j: (b, i, 0)), row, row, row,
                  pl.BlockSpec((D, tf), lambda b, i, j: (0, j)),
                  pl.BlockSpec((D, tf), lambda b, i, j: (0, j)),
                  pl.BlockSpec((tf, D), lambda b, i, j: (j, 0)), vec, vec],
        out_specs=pl.BlockSpec((None, tm, D), lambda b, i, j: (b, i, 0)),
        out_shape=jax.ShapeDtypeStruct((B, T, D), F32),
        scratch_shapes=[pltpu.VMEM((tm, D), BF16), pltpu.VMEM((tm, D), F32)],
        compiler_params=_cparams(("parallel", "parallel", "arbitrary")),
        name="ffn",
    )(x, sh, sc, g, wg, wu, wd, lng.reshape(1, D), lnb.reshape(1, D))


def _rope_slot(x, c, sa, sb, half):
    return x * c + pltpu.roll(x, LANES - half, 1) * sa + pltpu.roll(x, half, 1) * sb


def _rms_norm(x, g):
    return x * lax.rsqrt(jnp.mean(x * x, -1, keepdims=True) + EPS) * g


def _proj_body(x_ref, sh_ref, sc_ref, w_ref, gq_ref, gkv_ref, wuq_ref, wuk_ref, wuv_ref,
               tab_ref,
               qn_ref, kvc_ref, kvs_ref, kvw_ref, gate_ref, qm_ref, km_ref, vm_ref,
               *, nsa_scale, mla_scale):
    u = (x_ref[...] * (1.0 + sc_ref[...]) + sh_ref[...]).astype(BF16)
    h = _dot(u, w_ref[...])
    cq, sqa, sqb, cn, sna, snb, cm, sma, smb = [tab_ref[:, t * LANES:(t + 1) * LANES] for t in range(N_TABS)]
    nh = NSA_ROPE_DIM // 2
    mh = MLA_ROPE_DIM // 2

    for s in range(NSA_HEADS * NSA_HEAD_DIM // LANES):
        lo = OFF_Q + s * LANES
        qn_ref[:, s * LANES:(s + 1) * LANES] = (
            _rope_slot(h[:, lo:lo + LANES], cq, sqa, sqb, nh) * nsa_scale).astype(BF16)
    for idx, ref in enumerate((kvc_ref, kvs_ref, kvw_ref)):
        for g in range(NSA_KV_GROUPS):
            lo = OFF_KV + (idx * NSA_KV_GROUPS + g) * LANES
            ref[g] = _rope_slot(h[:, lo:lo + LANES], cn, sna, snb, nh).astype(BF16)
    gate_ref[...] = jax.nn.sigmoid(h[:, OFF_GATE:OFF_GATE + NSA_KV_GROUPS * LANES])

    cqn = _rms_norm(h[:, OFF_CQ:OFF_CQ + MLA_Q_RANK], gq_ref[...]).astype(BF16)
    qm = _dot(cqn, wuq_ref[...])
    ckvn = _rms_norm(h[:, OFF_CKV:OFF_CKV + MLA_KV_RANK], gkv_ref[...]).astype(BF16)
    kn = _dot(ckvn, wuk_ref[...])
    vm_ref[...] = _dot(ckvn, wuv_ref[...]).astype(BF16)
    kr = _rope_slot(h[:, OFF_KR:OFF_KR + LANES], cm, sma, smb, mh)
    for s in range(MLA_HEADS):
        sl = slice(s * LANES, (s + 1) * LANES)
        qm_ref[:, sl] = (_rope_slot(qm[:, sl], cm, sma, smb, mh) * mla_scale).astype(BF16)
        km_ref[:, sl] = (kn[:, sl] + kr).astype(BF16)


def _proj_call(x, sh, sc, w_all, gq, gkv, wuq, wuk, wuv, tabs):
    B, T, D = x.shape
    tm = min(T, 256)
    G = NSA_KV_GROUPS
    row = pl.BlockSpec((None, 1, D), lambda b, i: (b, 0, 0))

    def full(a):
        return pl.BlockSpec(a.shape, lambda b, i: (0,) * a.ndim)

    tab = pl.BlockSpec((tm, N_TABS * LANES), lambda b, i: (i, 0))
    tok = lambda w: pl.BlockSpec((None, tm, w), lambda b, i: (b, i, 0))
    grp = pl.BlockSpec((G, None, tm, LANES), lambda b, i: (0, b, i, 0))
    sd = jax.ShapeDtypeStruct
    return pl.pallas_call(
        functools.partial(_proj_body, nsa_scale=NSA_HEAD_DIM ** -0.5 * LOG2E,
                          mla_scale=(MLA_NOPE_DIM + MLA_ROPE_DIM) ** -0.5 * LOG2E),
        grid=(B, T // tm),
        in_specs=[tok(D), row, row, full(w_all), full(gq), full(gkv), full(wuq), full(wuk), full(wuv)]
                 + [tab],
        out_specs=[tok(NSA_HEADS * NSA_HEAD_DIM), grp, grp, grp, tok(G * LANES),
                   tok(MLA_HEADS * LANES), tok(MLA_HEADS * LANES), tok(MLA_HEADS * MLA_V_DIM)],
        out_shape=[sd((B, T, NSA_HEADS * NSA_HEAD_DIM), BF16), sd((G, B, T, LANES), BF16),
                   sd((G, B, T, LANES), BF16), sd((G, B, T, LANES), BF16), sd((B, T, G * LANES), F32),
                   sd((B, T, MLA_HEADS * LANES), BF16), sd((B, T, MLA_HEADS * LANES), BF16),
                   sd((B, T, MLA_HEADS * MLA_V_DIM), BF16)],
        compiler_params=_cparams(("parallel", "parallel")),
        name="proj",
    )(x, sh, sc, w_all, gq, gkv, wuq, wuk, wuv, tabs)


def _gelu_tanh(x):
    return x * (0.5 * (1.0 + jnp.tanh(0.7978845608028654 * (x + 0.044715 * (x * x * x)))))


def _cmp_body(c_ref, w1a_ref, w1b_ref, pa_ref, pb_ref, w2_ref, o_ref):
    c = c_ref[...]
    nc = c.shape[0]
    w1a, w1b = w1a_ref[...], w1b_ref[...]
    a = _dot(c, w1a)
    b = _dot(c, w1b)
    bias = _dot(pa_ref[...], w1a) + _dot(pb_ref[...], w1b)
    pre = a + pltpu.roll(b, nc - 1, 0) + bias[0:1]
    o_ref[...] = _dot(_gelu_tanh(pre).astype(BF16), w2_ref[...]).astype(BF16)


def _cmp_call(kvc, w1a, w1b, pa, pb, w2):
    G, B, T, _ = kvc.shape
    nc = T // CMP_STRIDE
    c = kvc.reshape(G, B, nc, CMP_STRIDE * LANES)

    def full(a):
        return pl.BlockSpec(a.shape, lambda g, b: (0,) * a.ndim)

    return pl.pallas_call(
        _cmp_body,
        grid=(G, B),
        in_specs=[pl.BlockSpec((None, None, nc, CMP_STRIDE * LANES), lambda g, b: (g, b, 0, 0)),
                  full(w1a), full(w1b), full(pa), full(pb), full(w2)],
        out_specs=pl.BlockSpec((None, None, nc, LANES), lambda g, b: (g, b, 0, 0)),
        out_shape=jax.ShapeDtypeStruct((G, B, nc, LANES), BF16),
        compiler_params=_cparams(("parallel", "parallel")),
        name="cmp",
    )(c, w1a, w1b, pa, pb, w2)


def _nsa_body(q_ref, gate_ref, kvc_ref, kvs_ref, kvw_ref, ov_ref, et_ref, o_ref,
              m_scr, l_scr, acc_scr, part_scr, *, n_slc):
    i = pl.program_id(2)
    nc = kvc_ref.shape[0]
    hpg, tq_n, rows, tk = NSA_HPG, NSA_TQ, NSA_HPG * NSA_TQ, NSA_TK
    t0 = pl.multiple_of(i * tq_n, tq_n)
    q = q_ref[...].astype(F32)
    low = _iota((tq_n, LANES), 1) < NSA_HEAD_DIM
    heads = []
    for h in range(hpg):
        tile_ = q[:, (h // 2) * LANES:(h // 2 + 1) * LANES]
        heads.append(jnp.where(low, tile_ if h % 2 == 0 else pltpu.roll(tile_, NSA_HEAD_DIM, 1), 0.0))
    qs = jnp.concatenate(heads, axis=0).astype(BF16)

    wk = WINDOW + tq_n
    start = pl.multiple_of(jnp.maximum(t0 - WINDOW, 0), tq_n)
    kvw = kvw_ref[pl.ds(start, wk), :]
    s_w = _dot_t(qs, kvw)
    diff = (t0 + _iota((tq_n, wk), 0)) - (start + _iota((tq_n, wk), 1))
    s_w = s_w + _rep_rows(jnp.where((diff >= 0) & (diff < WINDOW), 0.0, NEG), hpg)
    e_w = jnp.exp2(s_w - jnp.max(s_w, -1, keepdims=True))
    o_win = _dot(e_w.astype(BF16), kvw) / jnp.sum(e_w, -1, keepdims=True)

    kvc = kvc_ref[...]
    s = _dot_t(qs, kvc)
    tq = t0 + _iota((tq_n, nc), 0)
    n = _iota((tq_n, nc), 1)
    valid = (n * CMP_STRIDE + (CMP_BLOCK - 1) <= tq) & (n < nc - 1)
    s = s + _rep_rows(jnp.where(valid, 0.0, NEG), hpg)
    e = jnp.exp2(s - jnp.max(s, -1, keepdims=True)) * _rep_rows(jnp.where(valid, 1.0, 0.0), hpg)
    p = e / jnp.maximum(jnp.sum(e, -1, keepdims=True), 1e-30)
    o_cmp = _dot(p.astype(BF16), kvc)

    ps = p[0:tq_n] + p[tq_n:2 * tq_n] + p[2 * tq_n:3 * tq_n] + p[3 * tq_n:4 * tq_n]
    ph = ps.astype(BF16)
    plo = (ps - ph.astype(F32)).astype(BF16)
    ov = ov_ref[...]
    imp_t = (_dot(ph, ov) + _dot(plo, ov)).T

    jr = _iota((LANES, tq_n), 0)
    tl = t0 + _iota((LANES, tq_n), 1)
    cur = lax.shift_right_logical(tl, 6)
    causal = jr * SLC_BLOCK <= tl
    forced = ((jr == 0) | (jr == cur) | (jr == cur - 1)) & causal
    v = jnp.where(forced, PICKED, jnp.where(causal, imp_t, -1.0))
    if n_slc < LANES:
        v = jnp.where(jr < n_slc, v, PICKED)
    jf = jr.astype(F32)
    sel_t = jnp.where(forced, 1.0, 0.0)
    n_forced = 3
    for _ in range(SLC_TOPK - n_forced):
        mx = jnp.max(v, axis=0, keepdims=True)
        idx = jnp.min(jnp.where(v == mx, jf, float(LANES)), axis=0, keepdims=True)
        hit = jf == idx
        sel_t = jnp.where(hit, 1.0, sel_t)
        v = jnp.where(hit, PICKED, v)
    sel = sel_t.T

    gt = gate_ref[...]
    gcol = lambda br: jnp.concatenate([gt[:, h * 3 + br:h * 3 + br + 1] for h in range(hpg)], axis=0)
    part_scr[...] = gcol(0) * o_cmp + gcol(2) * o_win

    upto = _iota((tq_n, LANES), 1) < lax.shift_right_logical(t0 + tq_n, 6)
    lhs = jnp.concatenate([qs, _rep_rows(jnp.where((sel > 0.5) & upto, 0.0, NEG).astype(BF16), hpg)], axis=1)
    tk_shift = tk.bit_length() - 1
    n_full = lax.shift_right_logical(t0 + tq_n - 1, tk_shift)
    tail_idx = lax.shift_right_logical(t0 + tq_n - 1 - n_full * tk, tq_n.bit_length() - 1)

    m_scr[...] = jnp.full(m_scr.shape, M_INIT, F32)
    l_scr[...] = jnp.zeros(l_scr.shape, F32)
    acc_scr[...] = jnp.zeros(acc_scr.shape, F32)
    rc = ROW_CHUNK

    def tile(k, width, last):
        k0 = pl.multiple_of(k * tk, tk)
        kv = kvs_ref[pl.ds(k0, width), :]
        rhs = jnp.concatenate([kv, et_ref[pl.ds(k0, width), :]], axis=1)
        s_all = _dot_t(lhs, rhs)
        ps, als = [], []
        for r in range(0, rows, rc):
            sl = slice(r, r + rc)
            sc = s_all[sl]
            if last:
                own = jnp.where(_iota((rc, tq_n), 1) <= (r % tq_n) + _iota((rc, tq_n), 0), 0.0, NEG)
                sc = sc + own if width == tq_n else jnp.concatenate(
                    [sc[:, :width - tq_n], sc[:, width - tq_n:] + own], axis=1)
            m_prev = m_scr[sl]
            m_new = jnp.maximum(m_prev, jnp.max(sc, -1, keepdims=True))
            alpha = jnp.exp2(m_prev - m_new)
            pp = jnp.exp2(sc - _rep_lanes(m_new, width // LANES))
            l_scr[sl] = alpha * l_scr[sl] + jnp.sum(pp, -1, keepdims=True)
            m_scr[sl] = m_new
            ps.append(pp.astype(BF16))
            als.append(alpha)
        acc_scr[...] = jnp.concatenate(als, axis=0) * acc_scr[...] + _dot(jnp.concatenate(ps, axis=0), kv)

    def body(k, carry):
        tile(k, tk, False)
        return carry

    lax.fori_loop(0, n_full, body, 0)
    for j in range(tk // tq_n):
        pl.when(tail_idx == j)(functools.partial(tile, n_full, (j + 1) * tq_n, True))
    acc = acc_scr[...]
    out = part_scr[...] + gcol(1) * (acc / l_scr[...])
    out = jnp.where(_iota((rows, LANES), 1) >= NSA_HEAD_DIM, out, 0.0)
    for hp in range(hpg // 2):
        even = out[2 * hp * tq_n:(2 * hp + 1) * tq_n]
        odd = out[(2 * hp + 1) * tq_n:(2 * hp + 2) * tq_n]
        o_ref[:, hp * LANES:(hp + 1) * LANES] = (pltpu.roll(even, NSA_HEAD_DIM, 1) + odd).astype(BF16)


def _nsa_call(qn, gates, kvcmp, kvs, kvw, ov, et):
    B, T, _ = qn.shape
    G = NSA_KV_GROUPS
    nc = kvcmp.shape[2]
    n_slc = T // SLC_BLOCK
    assert SLC_TOPK <= n_slc <= LANES and nc % LANES == 0 and T % NSA_TK == 0 and T >= WINDOW + NSA_TQ
    assert NSA_TK & (NSA_TK - 1) == 0 and NSA_TQ % SLC_BLOCK == 0 and NSA_TK % NSA_TQ == 0
    gw = NSA_HPG * NSA_HEAD_DIM
    rows = NSA_HPG * NSA_TQ
    whole = lambda r: pl.BlockSpec((None, None, r, LANES), lambda b, g, i: (g, b, 0, 0))
    const = lambda a: pl.BlockSpec(a.shape, lambda b, g, i: (0, 0))
    return pl.pallas_call(
        functools.partial(_nsa_body, n_slc=n_slc),
        grid=(B, G, T // NSA_TQ),
        in_specs=[pl.BlockSpec((None, NSA_TQ, gw), lambda b, g, i: (b, i, g)),
                  pl.BlockSpec((None, NSA_TQ, LANES), lambda b, g, i: (b, i, g)),
                  whole(nc), whole(T), whole(T), const(ov), const(et)],
        out_specs=pl.BlockSpec((None, NSA_TQ, NSA_HPG * NSA_HEAD_DIM), lambda b, g, i: (b, i, g)),
        out_shape=jax.ShapeDtypeStruct((B, T, NSA_HEADS * NSA_HEAD_DIM), BF16),
        scratch_shapes=[pltpu.VMEM((rows, LANES), F32)] * 4,
        compiler_params=_cparams(("parallel", "parallel", "arbitrary")),
        name="nsa",
    )(qn, gates, kvcmp, kvs, kvw, ov, et)


def _mla_body(q_ref, k_ref, v_ref, o_ref, p_buf, a_buf, m_scr, l_scr, acc_scr):
    i = pl.program_id(2)
    tt, nh = MLA_T, MLA_HPS
    head = lambda hh: slice(hh * LANES, (hh + 1) * LANES)

    pair = lambda hh: slice((hh // 2) * LANES, (hh // 2 + 1) * LANES)

    def scores(kt, hh):
        k0 = pl.multiple_of(kt * tt, tt)
        return _dot_t(q_ref[:, head(hh)], k_ref[pl.ds(k0, tt), head(hh)])

    def values(kt, hh):
        return v_ref[pl.ds(pl.multiple_of(kt * tt, tt), tt), pair(hh)]

    m_scr[...] = jnp.full(m_scr.shape, M_INIT, F32)
    l_scr[...] = jnp.zeros(l_scr.shape, F32)
    acc_scr[...] = jnp.zeros(acc_scr.shape, F32)
    p_buf[...] = jnp.zeros(p_buf.shape, BF16)
    a_buf[...] = jnp.ones(a_buf.shape, F32)

    def stage(k, diagonal):
        kp = jnp.maximum(k - 1, 0)
        if diagonal:
            tri = jnp.where(_iota((tt, tt), 1) <= _iota((tt, tt), 0), 0.0, NEG)
        for hh in range(nh):
            s = scores(k, hh)
            if diagonal:
                s = s + tri
            acc_scr[hh] = a_buf[hh] * acc_scr[hh] + _dot(p_buf[hh], values(kp, hh))
            m_prev = m_scr[hh]
            m_new = jnp.maximum(m_prev, jnp.max(s, -1, keepdims=True))
            alpha = jnp.exp2(m_prev - m_new)
            p = jnp.exp2(s - _rep_lanes(m_new, tt // LANES))
            l_scr[hh] = alpha * l_scr[hh] + jnp.sum(p, -1, keepdims=True)
            m_scr[hh] = m_new
            a_buf[hh] = alpha
            p_buf[hh] = p.astype(BF16)

    def body(k, carry):
        stage(k, False)
        return carry

    lax.fori_loop(0, i, body, 0)
    stage(i, True)
    outs = [(a_buf[hh] * acc_scr[hh] + _dot(p_buf[hh], values(i, hh))) / l_scr[hh] for hh in range(nh)]
    first = _iota((tt, LANES), 1) < MLA_V_DIM
    for pp in range(nh // 2):
        o_ref[:, pp * LANES:(pp + 1) * LANES] = jnp.where(first, outs[2 * pp], outs[2 * pp + 1]).astype(BF16)


def _mla_call(qm, km, vm):
    B, T, _ = qm.shape
    nh = MLA_HPS
    assert T % MLA_T == 0 and MLA_HEADS % nh == 0 and nh % 2 == 0
    vw = nh * MLA_V_DIM
    return pl.pallas_call(
        _mla_body,
        grid=(B, MLA_HEADS // nh, T // MLA_T),
        in_specs=[pl.BlockSpec((None, MLA_T, nh * LANES), lambda b, p, i: (b, i, p)),
                  pl.BlockSpec((None, T, nh * LANES), lambda b, p, i: (b, 0, p)),
                  pl.BlockSpec((None, T, vw), lambda b, p, i: (b, 0, p))],
        out_specs=pl.BlockSpec((None, MLA_T, vw), lambda b, p, i: (b, i, p)),
        out_shape=jax.ShapeDtypeStruct((B, T, MLA_HEADS * MLA_V_DIM), BF16),
        scratch_shapes=[pltpu.VMEM((nh, MLA_T, MLA_T), BF16)] + [pltpu.VMEM((nh, MLA_T, LANES), F32)] * 4,
        compiler_params=_cparams(("parallel", "parallel", "arbitrary"), MLA_VMEM_LIMIT),
        name="mla",
    )(qm, km, vm)


def _out_body(x_ref, on_ref, om_ref, g_ref, wn_ref, wm_ref, lng_ref, lnb_ref, o_ref, *, alpha):
    y = _dot(on_ref[...], wn_ref[...]) + _dot(om_ref[...], wm_ref[...])
    z = alpha * x_ref[...] + (1.0 + g_ref[...]) * y
    o_ref[...] = _layer_norm(z, lng_ref[...], lnb_ref[...])


def _out_call(x, o_nsa, o_mla, g, w_on, w_om, lng, lnb, alpha):
    B, T, D = x.shape
    tm = min(T, 512)
    tok = lambda w: pl.BlockSpec((None, tm, w), lambda b, i: (b, i, 0))
    full = lambda a: pl.BlockSpec(a.shape, lambda b, i: (0,) * a.ndim)
    lng, lnb = lng.reshape(1, D), lnb.reshape(1, D)
    return pl.pallas_call(
        functools.partial(_out_body, alpha=alpha),
        grid=(B, T // tm),
        in_specs=[tok(D), tok(o_nsa.shape[-1]), tok(o_mla.shape[-1]),
                  pl.BlockSpec((None, 1, D), lambda b, i: (b, 0, 0)),
                  full(w_on), full(w_om), full(lng), full(lnb)],
        out_specs=tok(D),
        out_shape=jax.ShapeDtypeStruct((B, T, D), F32),
        compiler_params=_cparams(("parallel", "parallel")),
        name="out",
    )(x, o_nsa, o_mla, g, w_on, w_om, lng, lnb)


ROPE_SETS = (((0, NSA_HEAD_DIM), NSA_ROPE_DIM),
             ((0,), NSA_ROPE_DIM),
             ((MLA_NOPE_DIM,), MLA_ROPE_DIM))
N_TABS = 3 * len(ROPE_SETS)


def _rope_tables(T):
    pos = jnp.arange(T).astype(F32)[:, None]
    cols, place, outside = [], [], np.zeros((1, N_TABS * LANES), np.float32)
    n_rows = sum(dim for _, dim in ROPE_SETS)
    row = 0
    for si, (lane0s, dim) in enumerate(ROPE_SETS):
        half = dim // 2
        ang = pos * (ROPE_THETA ** (-jnp.arange(half, dtype=F32) / half))[None, :]
        cols += [jnp.cos(ang), jnp.sin(ang)]
        p = np.zeros((n_rows, N_TABS * LANES), np.float32)
        c0 = 3 * si * LANES
        r = np.arange(half)
        for lane0 in lane0s:
            p[row + r, c0 + lane0 + r] = 1.0
            p[row + r, c0 + lane0 + half + r] = 1.0
            p[row + half + r, c0 + LANES + lane0 + r] = -1.0
            p[row + half + r, c0 + 2 * LANES + lane0 + half + r] = 1.0
        place.append(p)
        outside[0, c0:c0 + LANES] = 1.0 - p[:, c0:c0 + LANES].sum(axis=0)
        row += dim
    return jnp.dot(jnp.concatenate(cols, axis=1), jnp.asarray(sum(place)),
                   precision=lax.Precision.HIGHEST) + jnp.asarray(outside)


def _pad_slots(w, n, width):
    K = w.shape[0]
    return jnp.pad(w.reshape(K, n, width), ((0, 0), (0, 0), (0, LANES - width))).reshape(K, n * LANES)


def _proj_weights(w_in):
    D = w_in.shape[0]
    hd, G = NSA_HEAD_DIM, NSA_KV_GROUPS
    nq, nkv = NSA_HEADS * hd, G * hd
    off = np.cumsum([0, nq] + [nkv] * 6 + [3 * NSA_HEADS, MLA_Q_RANK, MLA_KV_RANK, MLA_ROPE_DIM])
    cols = [w_in[:, :nq]]
    for pair in range(3):
        ko, vo = off[1 + 2 * pair], off[2 + 2 * pair]
        for g in range(G):
            cols += [w_in[:, ko + g * hd:ko + (g + 1) * hd], w_in[:, vo + g * hd:vo + (g + 1) * hd]]
    per_g = 3 * NSA_HPG
    for g in range(G):
        cols.append(jnp.pad(w_in[:, off[7] + g * per_g:off[7] + (g + 1) * per_g], ((0, 0), (0, LANES - per_g))))
    cols += [w_in[:, off[8]:off[9]], w_in[:, off[9]:off[10]]]
    cols.append(jnp.pad(w_in[:, off[10]:off[11]],
                        ((0, 0), (MLA_NOPE_DIM, LANES - MLA_NOPE_DIM - MLA_ROPE_DIM))))
    w_all = jnp.concatenate(cols, axis=1).astype(BF16)
    assert w_all.shape == (D, PROJ_W)
    return w_all


def _cmp_weights(k_pos, k_w1, k_w2, v_pos, v_w1, v_w2):
    hd, st, hid = NSA_HEAD_DIM, CMP_STRIDE, CMP_HIDDEN

    def half(lo):
        w = jnp.zeros((st, LANES, 2 * hid), F32)
        w = w.at[:, :hd, :hid].set(k_w1[lo * hd:(lo + st) * hd].reshape(st, hd, hid))
        w = w.at[:, hd:, hid:].set(v_w1[lo * hd:(lo + st) * hd].reshape(st, hd, hid))
        pos = jnp.concatenate([k_pos[lo:lo + st], v_pos[lo:lo + st]], axis=1).reshape(1, st * LANES)
        return w.reshape(st * LANES, 2 * hid).astype(BF16), jnp.broadcast_to(pos, (8, st * LANES)).astype(BF16)

    w1a, pa = half(0)
    w1b, pb = half(st)
    w2 = jnp.zeros((2 * hid, LANES), F32).at[:hid, :hd].set(k_w2).at[hid:, hd:].set(v_w2).astype(BF16)
    return w1a, w1b, pa, pb, w2


def _block_tables(T):
    nc, n_slc = T // CMP_STRIDE, T // SLC_BLOCK
    cs = np.arange(nc)[:, None] * CMP_STRIDE
    blk = np.arange(LANES)[None, :]
    ov = (cs < blk * SLC_BLOCK + SLC_BLOCK) & (cs + CMP_BLOCK > blk * SLC_BLOCK) & (np.arange(nc)[:, None] < nc - 1) & (blk < n_slc)
    et = (np.arange(T)[:, None] // SLC_BLOCK) == blk
    return jnp.asarray(ov.astype(np.float32), BF16), jnp.asarray(et.astype(np.float32), BF16)


def kernel(x, c, w_ada, b_ada, ln_g, ln_b, ffn1_wg, ffn1_wu, ffn1_wd, w_in, cmp_k_pos, cmp_k_w1, cmp_k_w2,
           cmp_v_pos, cmp_v_w1, cmp_v_w2, mla_q_norm, mla_kv_norm, mla_w_uq, mla_w_ukv, w_out,
           ffn2_wg, ffn2_wu, ffn2_wd):
    B, T, D = x.shape
    depth = w_ada.shape[0]
    alpha = (2.0 * depth) ** 0.25
    tabs = _rope_tables(T)
    ov, et = _block_tables(T)
    for l in range(depth):
        mod = _mod_call(c, w_ada[l], b_ada[l])
        sh1, sc1, g1, sh2, sc2, g2, sh3, sc3, g3 = [mod[:, k] for k in range(N_MOD)]
        x = _ffn_call(x, sh1, sc1, g1, ffn1_wg[l].astype(BF16), ffn1_wu[l].astype(BF16), ffn1_wd[l].astype(BF16),
                      ln_g[l, 0], ln_b[l, 0], alpha, 0.5)

        wuq = _pad_slots(mla_w_uq[l], MLA_HEADS, MLA_NOPE_DIM + MLA_ROPE_DIM).astype(BF16)
        wukv = mla_w_ukv[l].reshape(MLA_KV_RANK, MLA_HEADS, MLA_NOPE_DIM + MLA_V_DIM)
        wuk = _pad_slots(wukv[:, :, :MLA_NOPE_DIM].reshape(MLA_KV_RANK, -1), MLA_HEADS, MLA_NOPE_DIM).astype(BF16)
        wuv = wukv[:, :, MLA_NOPE_DIM:].reshape(MLA_KV_RANK, -1).astype(BF16)
        qn, kvc, kvs, kvw, gates, qm, km, vm = _proj_call(
            x, sh2, sc2, _proj_weights(w_in[l]), mla_q_norm[l].reshape(1, -1), mla_kv_norm[l].reshape(1, -1),
            wuq, wuk, wuv, tabs)
        kvcmp = _cmp_call(kvc, *_cmp_weights(cmp_k_pos[l], cmp_k_w1[l], cmp_k_w2[l],
                                             cmp_v_pos[l], cmp_v_w1[l], cmp_v_w2[l]))
        o_nsa = _nsa_call(qn, gates, kvcmp, kvs, kvw, ov, et)
        o_mla = _mla_call(qm, km, vm)
        n_nsa = NSA_HEADS * NSA_HEAD_DIM
        w_o = w_out[l].astype(BF16)
        x = _out_call(x, o_nsa, o_mla, g2, w_o[:n_nsa], w_o[n_nsa:], ln_g[l, 1], ln_b[l, 1], alpha)

        x = _ffn_call(x, sh3, sc3, g3, ffn2_wg[l].astype(BF16), ffn2_wu[l].astype(BF16), ffn2_wd[l].astype(BF16),
                      ln_g[l, 2], ln_b[l, 2], alpha, 0.5)
    return x
```

```python
import functools

import numpy as np
import jax
import jax.numpy as jnp
from jax import lax
from jax.experimental import pallas as pl
from jax.experimental.pallas import tpu as pltpu

F32 = jnp.float32
BF16 = jnp.bfloat16

ROPE_THETA = 500000.0
NSA_HEADS = 8
NSA_KV_GROUPS = 2
NSA_HPG = NSA_HEADS // NSA_KV_GROUPS
NSA_HEAD_DIM = 64
NSA_ROPE_DIM = 16
CMP_BLOCK = 32
CMP_STRIDE = 16
CMP_HIDDEN = 256
SLC_BLOCK = 64
SLC_TOPK = 16
WINDOW = 512
FORCE_SCORE = 1e9
MLA_HEADS = 8
MLA_NOPE_DIM = 64
MLA_ROPE_DIM = 32
MLA_V_DIM = 64
MLA_Q_RANK = 384
MLA_KV_RANK = 256
EPS = 1e-5
N_MOD = 9

LANES = 128
NEG = -1e30
M_INIT = -1e29
PICKED = -3e38
ROW_CHUNK = 128
LOG2E = 1.4426950408889634
VMEM_LIMIT = 48 * 1024 * 1024
MLA_VMEM_LIMIT = 56 * 1024 * 1024

NSA_TQ = 256
NSA_TK = 1024
MLA_T = 512
MLA_HPS = 4

OFF_Q, OFF_KV, OFF_GATE, OFF_CQ, OFF_CKV, OFF_KR, PROJ_W = 0, 512, 1280, 1536, 1920, 2176, 2304


def _cparams(sem, vmem=VMEM_LIMIT):
    return pltpu.CompilerParams(dimension_semantics=sem, vmem_limit_bytes=vmem)


def _layer_norm(z, g, b):
    mu = jnp.mean(z, -1, keepdims=True)
    zc = z - mu
    var = jnp.mean(zc * zc, -1, keepdims=True)
    return zc * lax.rsqrt(var + EPS) * g + b


def _dot(a, b):
    return jnp.dot(a, b, preferred_element_type=F32)


def _dot_t(a, b):
    return lax.dot_general(a, b, (((1,), (1,)), ((), ())), preferred_element_type=F32)


def _rep_rows(a, n):
    return jnp.concatenate([a] * n, axis=0)


def _rep_lanes(a, n):
    return jnp.concatenate([a] * n, axis=1)


def _iota(shape, d):
    return lax.broadcasted_iota(jnp.int32, shape, d)


def _mod_body(c_ref, w_ref, b_ref, o_ref):
    c = c_ref[...]
    s = (c * jax.nn.sigmoid(c)).astype(BF16)
    o_ref[...] = _dot(s, w_ref[...].astype(BF16)) + b_ref[...]


def _mod_call(c, w_ada, b_ada):
    B, D = c.shape
    rows = 8
    c8 = jnp.zeros((rows, D), F32).at[:B].set(c)
    out = pl.pallas_call(
        _mod_body,
        grid=(N_MOD,),
        in_specs=[pl.BlockSpec((rows, D), lambda j: (0, 0)),
                  pl.BlockSpec((D, D), lambda j: (0, j)),
                  pl.BlockSpec((1, D), lambda j: (0, j))],
        out_specs=pl.BlockSpec((rows, D), lambda j: (0, j)),
        out_shape=jax.ShapeDtypeStruct((rows, N_MOD * D), F32),
        compiler_params=_cparams(("parallel",)),
        name="mod",
    )(c8, w_ada, b_ada.reshape(1, N_MOD * D))
    return out[:B].reshape(B, N_MOD, 1, D)


def _ffn_body(x_ref, sh_ref, sc_ref, g_ref, wg_ref, wu_ref, wd_ref, lng_ref, lnb_ref, o_ref,
              u_scr, acc_scr, *, alpha, coef, n_ff):
    j = pl.program_id(2)

    @pl.when(j == 0)
    def _():
        u_scr[...] = (x_ref[...] * (1.0 + sc_ref[...]) + sh_ref[...]).astype(BF16)
        acc_scr[...] = jnp.zeros_like(acc_scr)

    u = u_scr[...]
    a = _dot(u, wg_ref[...])
    b = _dot(u, wu_ref[...])
    h = (a * jax.nn.sigmoid(a)) * b
    acc_scr[...] += _dot(h.astype(BF16), wd_ref[...])

    @pl.when(j == n_ff - 1)
    def _():
        z = alpha * x_ref[...] + coef * (1.0 + g_ref[...]) * acc_scr[...]
        o_ref[...] = _layer_norm(z, lng_ref[...], lnb_ref[...])


def _ffn_call(x, sh, sc, g, wg, wu, wd, lng, lnb, alpha, coef):
    B, T, D = x.shape
    FF = wg.shape[1]
    tm = min(T, 1024)
    tf = 256
    assert T % tm == 0 and FF % tf == 0
    n_ff = FF // tf
    row = pl.BlockSpec((None, 1, D), lambda b, i, j: (b, 0, 0))
    vec = pl.BlockSpec((1, D), lambda b, i, j: (0, 0))
    return pl.pallas_call(
        functools.partial(_ffn_body, alpha=alpha, coef=coef, n_ff=n_ff),
        grid=(B, T // tm, n_ff),
        in_specs=[pl.BlockSpec((None, tm, D), lambda b, i, j: (b, i, 0)), row, row, row,
                  pl.BlockSpec((D, tf), lambda b, i, j: (0, j)),
                  pl.BlockSpec((D, tf), lambda b, i, j: (0, j)),
                  pl.BlockSpec((tf, D), lambda b, i, j: (j, 0)), vec, vec],
        out_specs=pl.BlockSpec((None, tm, D), lambda b, i, j: (b, i, 0)),
        out_shape=jax.ShapeDtypeStruct((B, T, D), F32),
        scratch_shapes=[pltpu.VMEM((tm, D), BF16), pltpu.VMEM((tm, D), F32)],
        compiler_params=_cparams(("parallel", "parallel", "arbitrary")),
        name="ffn",
    )(x, sh, sc, g, wg, wu, wd, lng.reshape(1, D), lnb.reshape(1, D))


def _rope_slot(x, c, sa, sb, half):
    return x * c + pltpu.roll(x, LANES - half, 1) * sa + pltpu.roll(x, half, 1) * sb


def _rms_norm(x, g):
    return x * lax.rsqrt(jnp.mean(x * x, -1, keepdims=True) + EPS) * g


def _proj_body(x_ref, sh_ref, sc_ref, w_ref, gq_ref, gkv_ref, wuq_ref, wuk_ref, wuv_ref,
               tab_ref,
               qn_ref, kvc_ref, kvs_ref, kvw_ref, gate_ref, qm_ref, km_ref, vm_ref,
               *, nsa_scale, mla_scale):
    u = (x_ref[...] * (1.0 + sc_ref[...]) + sh_ref[...]).astype(BF16)
    h = _dot(u, w_ref[...])
    cq, sqa, sqb, cn, sna, snb, cm, sma, smb = [tab_ref[:, t * LANES:(t + 1) * LANES] for t in range(N_TABS)]
    nh = NSA_ROPE_DIM // 2
    mh = MLA_ROPE_DIM // 2

    for s in range(NSA_HEADS * NSA_HEAD_DIM // LANES):
        lo = OFF_Q + s * LANES
        qn_ref[:, s * LANES:(s + 1) * LANES] = (
            _rope_slot(h[:, lo:lo + LANES], cq, sqa, sqb, nh) * nsa_scale).astype(BF16)
    for idx, ref in enumerate((kvc_ref, kvs_ref, kvw_ref)):
        for g in range(NSA_KV_GROUPS):
            lo = OFF_KV + (idx * NSA_KV_GROUPS + g) * LANES
            ref[g] = _rope_slot(h[:, lo:lo + LANES], cn, sna, snb, nh).astype(BF16)
    gate_ref[...] = jax.nn.sigmoid(h[:, OFF_GATE:OFF_GATE + NSA_KV_GROUPS * LANES])

    cqn = _rms_norm(h[:, OFF_CQ:OFF_CQ + MLA_Q_RANK], gq_ref[...]).astype(BF16)
    qm = _dot(cqn, wuq_ref[...])
    ckvn = _rms_norm(h[:, OFF_CKV:OFF_CKV + MLA_KV_RANK], gkv_ref[...]).astype(BF16)
    kn = _dot(ckvn, wuk_ref[...])
    vm_ref[...] = _dot(ckvn, wuv_ref[...]).astype(BF16)
    kr = _rope_slot(h[:, OFF_KR:OFF_KR + LANES], cm, sma, smb, mh)
    for s in range(MLA_HEADS):
        sl = slice(s * LANES, (s + 1) * LANES)
        qm_ref[:, sl] = (_rope_slot(qm[:, sl], cm, sma, smb, mh) * mla_scale).astype(BF16)
        km_ref[:, sl] = (kn[:, sl] + kr).astype(BF16)


def _proj_call(x, sh, sc, w_all, gq, gkv, wuq, wuk, wuv, tabs):
    B, T, D = x.shape
    tm = min(T, 512)
    G = NSA_KV_GROUPS
    row = pl.BlockSpec((None, 1, D), lambda b, i: (b, 0, 0))

    def full(a):
        return pl.BlockSpec(a.shape, lambda b, i: (0,) * a.ndim)

    tab = pl.BlockSpec((tm, N_TABS * LANES), lambda b, i: (i, 0))
    tok = lambda w: pl.BlockSpec((None, tm, w), lambda b, i: (b, i, 0))
    grp = pl.BlockSpec((G, None, tm, LANES), lambda b, i: (0, b, i, 0))
    sd = jax.ShapeDtypeStruct
    return pl.pallas_call(
        functools.partial(_proj_body, nsa_scale=NSA_HEAD_DIM ** -0.5 * LOG2E,
                          mla_scale=(MLA_NOPE_DIM + MLA_ROPE_DIM) ** -0.5 * LOG2E),
        grid=(B, T // tm),
        in_specs=[tok(D), row, row, full(w_all), full(gq), full(gkv), full(wuq), full(wuk), full(wuv)]
                 + [tab],
        out_specs=[tok(NSA_HEADS * NSA_HEAD_DIM), grp, grp, grp, tok(G * LANES),
                   tok(MLA_HEADS * LANES), tok(MLA_HEADS * LANES), tok(MLA_HEADS * MLA_V_DIM)],
        out_shape=[sd((B, T, NSA_HEADS * NSA_HEAD_DIM), BF16), sd((G, B, T, LANES), BF16),
                   sd((G, B, T, LANES), BF16), sd((G, B, T, LANES), BF16), sd((B, T, G * LANES), F32),
                   sd((B, T, MLA_HEADS * LANES), BF16), sd((B, T, MLA_HEADS * LANES), BF16),
                   sd((B, T, MLA_HEADS * MLA_V_DIM), BF16)],
        compiler_params=_cparams(("parallel", "parallel")),
        name="proj",
    )(x, sh, sc, w_all, gq, gkv, wuq, wuk, wuv, tabs)


def _gelu_tanh(x):
    return x * (0.5 * (1.0 + jnp.tanh(0.7978845608028654 * (x + 0.044715 * (x * x * x)))))


def _cmp_body(c_ref, w1a_ref, w1b_ref, pa_ref, pb_ref, w2_ref, o_ref):
    c = c_ref[...]
    nc = c.shape[0]
    w1a, w1b = w1a_ref[...], w1b_ref[...]
    a = _dot(c, w1a)
    b = _dot(c, w1b)
    bias = _dot(pa_ref[...], w1a) + _dot(pb_ref[...], w1b)
    pre = a + pltpu.roll(b, nc - 1, 0) + bias[0:1]
    o_ref[...] = _dot(_gelu_tanh(pre).astype(BF16), w2_ref[...]).astype(BF16)


def _cmp_call(kvc, w1a, w1b, pa, pb, w2):
    G, B, T, _ = kvc.shape
    nc = T // CMP_STRIDE
    c = kvc.reshape(G, B, nc, CMP_STRIDE * LANES)

    def full(a):
        return pl.BlockSpec(a.shape, lambda g, b: (0,) * a.ndim)

    return pl.pallas_call(
        _cmp_body,
        grid=(G, B),
        in_specs=[pl.BlockSpec((None, None, nc, CMP_STRIDE * LANES), lambda g, b: (g, b, 0, 0)),
                  full(w1a), full(w1b), full(pa), full(pb), full(w2)],
        out_specs=pl.BlockSpec((None, None, nc, LANES), lambda g, b: (g, b, 0, 0)),
        out_shape=jax.ShapeDtypeStruct((G, B, nc, LANES), BF16),
        compiler_params=_cparams(("parallel", "parallel")),
        name="cmp",
    )(c, w1a, w1b, pa, pb, w2)


def _nsa_body(q_ref, gate_ref, kvc_ref, kvs_ref, kvw_ref, ov_ref, et_ref, band_ref, o_ref,
              m_scr, l_scr, acc_scr, part_scr, *, n_slc):
    i = pl.program_id(2)
    nc = kvc_ref.shape[0]
    hpg, tq_n, rows, tk = NSA_HPG, NSA_TQ, NSA_HPG * NSA_TQ, NSA_TK
    t0 = pl.multiple_of(i * tq_n, tq_n)
    q = q_ref[...].astype(F32)
    low = _iota((tq_n, LANES), 1) < NSA_HEAD_DIM
    heads = []
    for h in range(hpg):
        tile_ = q[:, (h // 2) * LANES:(h // 2 + 1) * LANES]
        heads.append(jnp.where(low, tile_ if h % 2 == 0 else pltpu.roll(tile_, NSA_HEAD_DIM, 1), 0.0))
    qs = jnp.concatenate(heads, axis=0).astype(BF16)

    wk = WINDOW + tq_n
    start = pl.multiple_of(jnp.maximum(t0 - WINDOW, 0), tq_n)
    kvw = kvw_ref[pl.ds(start, wk), :]
    s_w = _dot_t(qs, kvw)
    s_w = s_w + _rep_rows(band_ref[...], hpg)
    e_w = jnp.exp2(s_w - jnp.max(s_w, -1, keepdims=True))
    o_win = _dot(e_w.astype(BF16), kvw) / jnp.sum(e_w, -1, keepdims=True)

    kvc = kvc_ref[...]
    s = _dot_t(qs, kvc)
    tq = t0 + _iota((tq_n, nc), 0)
    n = _iota((tq_n, nc), 1)
    valid = (n * CMP_STRIDE + (CMP_BLOCK - 1) <= tq) & (n < nc - 1)
    s = s + _rep_rows(jnp.where(valid, 0.0, NEG), hpg)
    e = jnp.exp2(s - jnp.maximum(jnp.max(s, -1, keepdims=True), M_INIT))
    p = e / jnp.maximum(jnp.sum(e, -1, keepdims=True), 1e-30)
    o_cmp = _dot(p.astype(BF16), kvc)

    ps = p[0:tq_n] + p[tq_n:2 * tq_n] + p[2 * tq_n:3 * tq_n] + p[3 * tq_n:4 * tq_n]
    ph = ps.astype(BF16)
    plo = (ps - ph.astype(F32)).astype(BF16)
    ov = ov_ref[...]
    imp_t = (_dot(ph, ov) + _dot(plo, ov)).T

    jr = _iota((LANES, tq_n), 0)
    tl = t0 + _iota((LANES, tq_n), 1)
    cur = lax.shift_right_logical(tl, 6)
    causal = jr * SLC_BLOCK <= tl
    forced = ((jr == 0) | (jr == cur) | (jr == cur - 1)) & causal
    v = jnp.where(forced, PICKED, jnp.where(causal, imp_t, -1.0))
    if n_slc < LANES:
        v = jnp.where(jr < n_slc, v, PICKED)
    jf = jr.astype(F32)
    sel_t = jnp.where(forced, 1.0, 0.0)
    n_forced = 3
    for _ in range(SLC_TOPK - n_forced):
        mx = jnp.max(v, axis=0, keepdims=True)
        idx = jnp.min(jnp.where(v == mx, jf, float(LANES)), axis=0, keepdims=True)
        hit = jf == idx
        sel_t = jnp.where(hit, 1.0, sel_t)
        v = jnp.where(hit, PICKED, v)
    sel = sel_t.T

    gt = gate_ref[...]
    gcol = lambda br: jnp.concatenate([gt[:, h * 3 + br:h * 3 + br + 1] for h in range(hpg)], axis=0)
    part_scr[...] = gcol(0) * o_cmp + gcol(2) * o_win

    upto = _iota((tq_n, LANES), 1) < lax.shift_right_logical(t0 + tq_n, 6)
    lhs = jnp.concatenate([qs, _rep_rows(jnp.where((sel > 0.5) & upto, 0.0, NEG).astype(BF16), hpg)], axis=1)
    tk_shift = tk.bit_length() - 1
    n_full = lax.shift_right_logical(t0 + tq_n - 1, tk_shift)
    tail_idx = lax.shift_right_logical(t0 + tq_n - 1 - n_full * tk, tq_n.bit_length() - 1)

    m_scr[...] = jnp.full(m_scr.shape, M_INIT, F32)
    l_scr[...] = jnp.zeros(l_scr.shape, F32)
    acc_scr[...] = jnp.zeros(acc_scr.shape, F32)
    rc = ROW_CHUNK

    def tile(k, width, last):
        k0 = pl.multiple_of(k * tk, tk)
        kv = kvs_ref[pl.ds(k0, width), :]
        rhs = jnp.concatenate([kv, et_ref[pl.ds(k0, width), :]], axis=1)
        s_all = _dot_t(lhs, rhs)
        ps, als = [], []
        for r in range(0, rows, rc):
            sl = slice(r, r + rc)
            sc = s_all[sl]
            if last:
                own = jnp.where(_iota((rc, tq_n), 1) <= (r % tq_n) + _iota((rc, tq_n), 0), 0.0, NEG)
                sc = sc + own if width == tq_n else jnp.concatenate(
                    [sc[:, :width - tq_n], sc[:, width - tq_n:] + own], axis=1)
            m_prev = m_scr[sl]
            m_new = jnp.maximum(m_prev, jnp.max(sc, -1, keepdims=True))
            alpha = jnp.exp2(m_prev - m_new)
            pp = jnp.exp2(sc - _rep_lanes(m_new, width // LANES))
            l_scr[sl] = alpha * l_scr[sl] + jnp.sum(pp, -1, keepdims=True)
            m_scr[sl] = m_new
            ps.append(pp.astype(BF16))
            als.append(alpha)
        acc_scr[...] = jnp.concatenate(als, axis=0) * acc_scr[...] + _dot(jnp.concatenate(ps, axis=0), kv)

    def body(k, carry):
        tile(k, tk, False)
        return carry

    lax.fori_loop(0, n_full, body, 0)
    for j in range(tk // tq_n):
        pl.when(tail_idx == j)(functools.partial(tile, n_full, (j + 1) * tq_n, True))
    acc = acc_scr[...]
    out = part_scr[...] + gcol(1) * (acc / l_scr[...])
    out = jnp.where(_iota((rows, LANES), 1) >= NSA_HEAD_DIM, out, 0.0)
    for hp in range(hpg // 2):
        even = out[2 * hp * tq_n:(2 * hp + 1) * tq_n]
        odd = out[(2 * hp + 1) * tq_n:(2 * hp + 2) * tq_n]
        o_ref[:, hp * LANES:(hp + 1) * LANES] = (pltpu.roll(even, NSA_HEAD_DIM, 1) + odd).astype(BF16)


def _nsa_call(qn, gates, kvcmp, kvs, kvw, ov, et, band):
    B, T, _ = qn.shape
    G = NSA_KV_GROUPS
    nc = kvcmp.shape[2]
    n_slc = T // SLC_BLOCK
    assert SLC_TOPK <= n_slc <= LANES and nc % LANES == 0 and T % NSA_TK == 0 and T >= WINDOW + NSA_TQ
    assert NSA_TK & (NSA_TK - 1) == 0 and NSA_TQ % SLC_BLOCK == 0 and NSA_TK % NSA_TQ == 0
    assert WINDOW % NSA_TQ == 0 and band.shape == (WINDOW // NSA_TQ + 1, NSA_TQ, WINDOW + NSA_TQ)
    gw = NSA_HPG * NSA_HEAD_DIM
    rows = NSA_HPG * NSA_TQ
    whole = lambda r: pl.BlockSpec((None, None, r, LANES), lambda b, g, i: (g, b, 0, 0))
    const = lambda a: pl.BlockSpec(a.shape, lambda b, g, i: (0, 0))
    return pl.pallas_call(
        functools.partial(_nsa_body, n_slc=n_slc),
        grid=(B, G, T // NSA_TQ),
        in_specs=[pl.BlockSpec((None, NSA_TQ, gw), lambda b, g, i: (b, i, g)),
                  pl.BlockSpec((None, NSA_TQ, LANES), lambda b, g, i: (b, i, g)),
                  whole(nc), whole(T), whole(T), const(ov), const(et),
                  pl.BlockSpec((None,) + band.shape[1:], lambda b, g, i: (jnp.minimum(i, band.shape[0] - 1), 0, 0))],
        out_specs=pl.BlockSpec((None, NSA_TQ, NSA_HPG * NSA_HEAD_DIM), lambda b, g, i: (b, i, g)),
        out_shape=jax.ShapeDtypeStruct((B, T, NSA_HEADS * NSA_HEAD_DIM), BF16),
        scratch_shapes=[pltpu.VMEM((rows, LANES), F32)] * 4,
        compiler_params=_cparams(("parallel", "parallel", "arbitrary")),
        name="nsa",
    )(qn, gates, kvcmp, kvs, kvw, ov, et, band)


def _mla_body(q_ref, k_ref, v_ref, o_ref, p_buf, a_buf, m_scr, l_scr, acc_scr):
    i = pl.program_id(2)
    tt, nh = MLA_T, MLA_HPS
    head = lambda hh: slice(hh * LANES, (hh + 1) * LANES)

    pair = lambda hh: slice((hh // 2) * LANES, (hh // 2 + 1) * LANES)

    def scores(kt, hh):
        k0 = pl.multiple_of(kt * tt, tt)
        return _dot_t(q_ref[:, head(hh)], k_ref[pl.ds(k0, tt), head(hh)])

    def values(kt, hh):
        return v_ref[pl.ds(pl.multiple_of(kt * tt, tt), tt), pair(hh)]

    m_scr[...] = jnp.full(m_scr.shape, M_INIT, F32)
    l_scr[...] = jnp.zeros(l_scr.shape, F32)
    acc_scr[...] = jnp.zeros(acc_scr.shape, F32)
    p_buf[...] = jnp.zeros(p_buf.shape, BF16)
    a_buf[...] = jnp.ones(a_buf.shape, F32)

    def stage(k, diagonal):
        kp = jnp.maximum(k - 1, 0)
        if diagonal:
            tri = jnp.where(_iota((tt, tt), 1) <= _iota((tt, tt), 0), 0.0, NEG)
        for hh in range(nh):
            s = scores(k, hh)
            if diagonal:
                s = s + tri
            acc_scr[hh] = a_buf[hh] * acc_scr[hh] + _dot(p_buf[hh], values(kp, hh))
            m_prev = m_scr[hh]
            m_new = jnp.maximum(m_prev, jnp.max(s, -1, keepdims=True))
            alpha = jnp.exp2(m_prev - m_new)
            p = jnp.exp2(s - _rep_lanes(m_new, tt // LANES))
            l_scr[hh] = alpha * l_scr[hh] + jnp.sum(p, -1, keepdims=True)
            m_scr[hh] = m_new
            a_buf[hh] = alpha
            p_buf[hh] = p.astype(BF16)

    def body(k, carry):
        stage(k, False)
        return carry

    lax.fori_loop(0, i, body, 0)
    stage(i, True)
    outs = [(a_buf[hh] * acc_scr[hh] + _dot(p_buf[hh], values(i, hh))) / l_scr[hh] for hh in range(nh)]
    first = _iota((tt, LANES), 1) < MLA_V_DIM
    for pp in range(nh // 2):
        o_ref[:, pp * LANES:(pp + 1) * LANES] = jnp.where(first, outs[2 * pp], outs[2 * pp + 1]).astype(BF16)


def _mla_call(qm, km, vm):
    B, T, _ = qm.shape
    nh = MLA_HPS
    assert T % MLA_T == 0 and MLA_HEADS % nh == 0 and nh % 2 == 0
    vw = nh * MLA_V_DIM
    return pl.pallas_call(
        _mla_body,
        grid=(B, MLA_HEADS // nh, T // MLA_T),
        in_specs=[pl.BlockSpec((None, MLA_T, nh * LANES), lambda b, p, i: (b, i, p)),
                  pl.BlockSpec((None, T, nh * LANES), lambda b, p, i: (b, 0, p)),
                  pl.BlockSpec((None, T, vw), lambda b, p, i: (b, 0, p))],
        out_specs=pl.BlockSpec((None, MLA_T, vw), lambda b, p, i: (b, i, p)),
        out_shape=jax.ShapeDtypeStruct((B, T, MLA_HEADS * MLA_V_DIM), BF16),
        scratch_shapes=[pltpu.VMEM((nh, MLA_T, MLA_T), BF16)] + [pltpu.VMEM((nh, MLA_T, LANES), F32)] * 4,
        compiler_params=_cparams(("parallel", "parallel", "arbitrary"), MLA_VMEM_LIMIT),
        name="mla",
    )(qm, km, vm)


def _out_body(x_ref, on_ref, om_ref, g_ref, wn_ref, wm_ref, lng_ref, lnb_ref, o_ref, *, alpha):
    y = _dot(on_ref[...], wn_ref[...]) + _dot(om_ref[...], wm_ref[...])
    z = alpha * x_ref[...] + (1.0 + g_ref[...]) * y
    o_ref[...] = _layer_norm(z, lng_ref[...], lnb_ref[...])


def _out_call(x, o_nsa, o_mla, g, w_on, w_om, lng, lnb, alpha):
    B, T, D = x.shape
    tm = min(T, 512)
    tok = lambda w: pl.BlockSpec((None, tm, w), lambda b, i: (b, i, 0))
    full = lambda a: pl.BlockSpec(a.shape, lambda b, i: (0,) * a.ndim)
    lng, lnb = lng.reshape(1, D), lnb.reshape(1, D)
    return pl.pallas_call(
        functools.partial(_out_body, alpha=alpha),
        grid=(B, T // tm),
        in_specs=[tok(D), tok(o_nsa.shape[-1]), tok(o_mla.shape[-1]),
                  pl.BlockSpec((None, 1, D), lambda b, i: (b, 0, 0)),
                  full(w_on), full(w_om), full(lng), full(lnb)],
        out_specs=tok(D),
        out_shape=jax.ShapeDtypeStruct((B, T, D), F32),
        compiler_params=_cparams(("parallel", "parallel")),
        name="out",
    )(x, o_nsa, o_mla, g, w_on, w_om, lng, lnb)


ROPE_SETS = (((0, NSA_HEAD_DIM), NSA_ROPE_DIM),
             ((0,), NSA_ROPE_DIM),
             ((MLA_NOPE_DIM,), MLA_ROPE_DIM))
N_TABS = 3 * len(ROPE_SETS)


def _rope_tables(T):
    pos = jnp.arange(T).astype(F32)[:, None]
    pieces = []
    for lane0s, dim in ROPE_SETS:
        half = dim // 2
        ang = pos * (ROPE_THETA ** (-jnp.arange(half, dtype=F32) / half))[None, :]
        cos, sin = jnp.cos(ang), jnp.sin(ang)
        zero = jnp.zeros((T, half), F32)
        for first, second, fill in ((cos, cos, 1.0), (-sin, zero, 0.0), (zero, sin, 0.0)):
            lane = 0
            for lane0 in lane0s:
                pieces += [jnp.full((T, lane0 - lane), fill, F32), first, second]
                lane = lane0 + dim
            pieces.append(jnp.full((T, LANES - lane), fill, F32))
    return jnp.concatenate([p for p in pieces if p.shape[1]], axis=1)


def _pad_slots(w, n, width):
    K = w.shape[0]
    return jnp.pad(w.reshape(K, n, width), ((0, 0), (0, 0), (0, LANES - width))).reshape(K, n * LANES)


def _proj_weights(w_in):
    D = w_in.shape[0]
    hd, G = NSA_HEAD_DIM, NSA_KV_GROUPS
    nq, nkv = NSA_HEADS * hd, G * hd
    off = np.cumsum([0, nq] + [nkv] * 6 + [3 * NSA_HEADS, MLA_Q_RANK, MLA_KV_RANK, MLA_ROPE_DIM])
    cols = [w_in[:, :nq]]
    for pair in range(3):
        ko, vo = off[1 + 2 * pair], off[2 + 2 * pair]
        for g in range(G):
            cols += [w_in[:, ko + g * hd:ko + (g + 1) * hd], w_in[:, vo + g * hd:vo + (g + 1) * hd]]
    per_g = 3 * NSA_HPG
    for g in range(G):
        cols.append(jnp.pad(w_in[:, off[7] + g * per_g:off[7] + (g + 1) * per_g], ((0, 0), (0, LANES - per_g))))
    cols += [w_in[:, off[8]:off[9]], w_in[:, off[9]:off[10]]]
    cols.append(jnp.pad(w_in[:, off[10]:off[11]],
                        ((0, 0), (MLA_NOPE_DIM, LANES - MLA_NOPE_DIM - MLA_ROPE_DIM))))
    w_all = jnp.concatenate(cols, axis=1).astype(BF16)
    assert w_all.shape == (D, PROJ_W)
    return w_all


def _cmp_weights(k_pos, k_w1, k_w2, v_pos, v_w1, v_w2):
    hd, st, hid = NSA_HEAD_DIM, CMP_STRIDE, CMP_HIDDEN

    def half(lo):
        w = jnp.zeros((st, LANES, 2 * hid), F32)
        w = w.at[:, :hd, :hid].set(k_w1[lo * hd:(lo + st) * hd].reshape(st, hd, hid))
        w = w.at[:, hd:, hid:].set(v_w1[lo * hd:(lo + st) * hd].reshape(st, hd, hid))
        pos = jnp.concatenate([k_pos[lo:lo + st], v_pos[lo:lo + st]], axis=1).reshape(1, st * LANES)
        return w.reshape(st * LANES, 2 * hid).astype(BF16), jnp.broadcast_to(pos, (8, st * LANES)).astype(BF16)

    w1a, pa = half(0)
    w1b, pb = half(st)
    w2 = jnp.zeros((2 * hid, LANES), F32).at[:hid, :hd].set(k_w2).at[hid:, hd:].set(v_w2).astype(BF16)
    return w1a, w1b, pa, pb, w2


def _block_tables(T):
    nc, n_slc = T // CMP_STRIDE, T // SLC_BLOCK
    cs = np.arange(nc)[:, None] * CMP_STRIDE
    blk = np.arange(LANES)[None, :]
    ov = (cs < blk * SLC_BLOCK + SLC_BLOCK) & (cs + CMP_BLOCK > blk * SLC_BLOCK) & (np.arange(nc)[:, None] < nc - 1) & (blk < n_slc)
    et = (np.arange(T)[:, None] // SLC_BLOCK) == blk
    off = np.arange(WINDOW // NSA_TQ + 1)[:, None, None] * NSA_TQ
    diff = off + np.arange(NSA_TQ)[None, :, None] - np.arange(WINDOW + NSA_TQ)[None, None, :]
    band = np.where((diff >= 0) & (diff < WINDOW), 0.0, NEG).astype(np.float32)
    return jnp.asarray(ov.astype(np.float32), BF16), jnp.asarray(et.astype(np.float32), BF16), jnp.asarray(band)


def kernel(x, c, w_ada, b_ada, ln_g, ln_b, ffn1_wg, ffn1_wu, ffn1_wd, w_in, cmp_k_pos, cmp_k_w1, cmp_k_w2,
           cmp_v_pos, cmp_v_w1, cmp_v_w2, mla_q_norm, mla_kv_norm, mla_w_uq, mla_w_ukv, w_out,
           ffn2_wg, ffn2_wu, ffn2_wd):
    B, T, D = x.shape
    depth = w_ada.shape[0]
    alpha = (2.0 * depth) ** 0.25
    tabs = _rope_tables(T)
    ov, et, band = _block_tables(T)
    for l in range(depth):
        mod = _mod_call(c, w_ada[l], b_ada[l])
        sh1, sc1, g1, sh2, sc2, g2, sh3, sc3, g3 = [mod[:, k] for k in range(N_MOD)]
        x = _ffn_call(x, sh1, sc1, g1, ffn1_wg[l].astype(BF16), ffn1_wu[l].astype(BF16), ffn1_wd[l].astype(BF16),
                      ln_g[l, 0], ln_b[l, 0], alpha, 0.5)

        wuq = _pad_slots(mla_w_uq[l], MLA_HEADS, MLA_NOPE_DIM + MLA_ROPE_DIM).astype(BF16)
        wukv = mla_w_ukv[l].reshape(MLA_KV_RANK, MLA_HEADS, MLA_NOPE_DIM + MLA_V_DIM)
        wuk = _pad_slots(wukv[:, :, :MLA_NOPE_DIM].reshape(MLA_KV_RANK, -1), MLA_HEADS, MLA_NOPE_DIM).astype(BF16)
        wuv = wukv[:, :, MLA_NOPE_DIM:].reshape(MLA_KV_RANK, -1).astype(BF16)
        qn, kvc, kvs, kvw, gates, qm, km, vm = _proj_call(
            x, sh2, sc2, _proj_weights(w_in[l]), mla_q_norm[l].reshape(1, -1), mla_kv_norm[l].reshape(1, -1),
            wuq, wuk, wuv, tabs)
        kvcmp = _cmp_call(kvc, *_cmp_weights(cmp_k_pos[l], cmp_k_w1[l], cmp_k_w2[l],
                                             cmp_v_pos[l], cmp_v_w1[l], cmp_v_w2[l]))
        o_nsa = _nsa_call(qn, gates, kvcmp, kvs, kvw, ov, et, band)
        o_mla = _mla_call(qm, km, vm)
        n_nsa = NSA_HEADS * NSA_HEAD_DIM
        w_o = w_out[l].astype(BF16)
        x = _out_call(x, o_nsa, o_mla, g2, w_o[:n_nsa], w_o[n_nsa:], ln_g[l, 1], ln_b[l, 1], alpha)

        x = _ffn_call(x, sh3, sc3, g3, ffn2_wg[l].astype(BF16), ffn2_wu[l].astype(BF16), ffn2_wd[l].astype(BF16),
                      ln_g[l, 2], ln_b[l, 2], alpha, 0.5)
    return x
```

```python
import functools

import numpy as np
import jax
import jax.numpy as jnp
from jax import lax
from jax.experimental import pallas as pl
from jax.experimental.pallas import tpu as pltpu

F32 = jnp.float32
BF16 = jnp.bfloat16

ROPE_THETA = 500000.0
NSA_HEADS = 8
NSA_KV_GROUPS = 2
NSA_HPG = NSA_HEADS // NSA_KV_GROUPS
NSA_HEAD_DIM = 64
NSA_ROPE_DIM = 16
CMP_BLOCK = 32
CMP_STRIDE = 16
CMP_HIDDEN = 256
SLC_BLOCK = 64
SLC_TOPK = 16
WINDOW = 512
FORCE_SCORE = 1e9
MLA_HEADS = 8
MLA_NOPE_DIM = 64
MLA_ROPE_DIM = 32
MLA_V_DIM = 64
MLA_Q_RANK = 384
MLA_KV_RANK = 256
EPS = 1e-5
N_MOD = 9

LANES = 128
NEG = -1e30
M_INIT = -1e29
PICKED = -3e38
ROW_CHUNK = 128
LOG2E = 1.4426950408889634
VMEM_LIMIT = 48 * 1024 * 1024
MLA_VMEM_LIMIT = 56 * 1024 * 1024

NSA_TQ = 256
NSA_TK = 1024
MLA_T = 512
MLA_HPS = 4

OFF_Q, OFF_KV, OFF_GATE, OFF_CQ, OFF_CKV, OFF_KR, PROJ_W = 0, 512, 1280, 1536, 1920, 2176, 2304


def _cparams(sem, vmem=VMEM_LIMIT):
    return pltpu.CompilerParams(dimension_semantics=sem, vmem_limit_bytes=vmem)


def _layer_norm(z, g, b):
    mu = jnp.mean(z, -1, keepdims=True)
    zc = z - mu
    var = jnp.mean(zc * zc, -1, keepdims=True)
    return zc * lax.rsqrt(var + EPS) * g + b


def _dot(a, b):
    return jnp.dot(a, b, preferred_element_type=F32)


def _dot_t(a, b):
    return lax.dot_general(a, b, (((1,), (1,)), ((), ())), preferred_element_type=F32)


def _rep_rows(a, n):
    return jnp.concatenate([a] * n, axis=0)


def _rep_lanes(a, n):
    return jnp.concatenate([a] * n, axis=1)


def _iota(shape, d):
    return lax.broadcasted_iota(jnp.int32, shape, d)


def _mod_body(c_ref, w_ref, b_ref, o_ref):
    c = c_ref[...]
    s = (c * jax.nn.sigmoid(c)).astype(BF16)
    o_ref[...] = _dot(s, w_ref[...].astype(BF16)) + b_ref[...]


def _mod_call(c, w_ada, b_ada):
    B, D = c.shape
    rows = 8
    c8 = jnp.zeros((rows, D), F32).at[:B].set(c)
    out = pl.pallas_call(
        _mod_body,
        grid=(N_MOD,),
        in_specs=[pl.BlockSpec((rows, D), lambda j: (0, 0)),
                  pl.BlockSpec((D, D), lambda j: (0, j)),
                  pl.BlockSpec((1, D), lambda j: (0, j))],
        out_specs=pl.BlockSpec((rows, D), lambda j: (0, j)),
        out_shape=jax.ShapeDtypeStruct((rows, N_MOD * D), F32),
        compiler_params=_cparams(("parallel",)),
        name="mod",
    )(c8, w_ada, b_ada.reshape(1, N_MOD * D))
    return out[:B].reshape(B, N_MOD, 1, D)


def _ffn_body(x_ref, sh_ref, sc_ref, g_ref, wg_ref, wu_ref, wd_ref, lng_ref, lnb_ref, o_ref,
              u_scr, acc_scr, *, alpha, coef, n_ff):
    j = pl.program_id(2)

    @pl.when(j == 0)
    def _():
        u_scr[...] = (x_ref[...] * (1.0 + sc_ref[...]) + sh_ref[...]).astype(BF16)
        acc_scr[...] = jnp.zeros_like(acc_scr)

    u = u_scr[...]
    a = _dot(u, wg_ref[...].astype(BF16))
    b = _dot(u, wu_ref[...].astype(BF16))
    h = (a * jax.nn.sigmoid(a)) * b
    acc_scr[...] += _dot(h.astype(BF16), wd_ref[...].astype(BF16))

    @pl.when(j == n_ff - 1)
    def _():
        z = alpha * x_ref[...] + coef * (1.0 + g_ref[...]) * acc_scr[...]
        o_ref[...] = _layer_norm(z, lng_ref[...], lnb_ref[...])


def _ffn_call(x, sh, sc, g, wg, wu, wd, lng, lnb, alpha, coef):
    B, T, D = x.shape
    FF = wg.shape[1]
    tm = min(T, 1024)
    tf = 256
    assert T % tm == 0 and FF % tf == 0
    n_ff = FF // tf
    row = pl.BlockSpec((None, 1, D), lambda b, i, j: (b, 0, 0))
    vec = pl.BlockSpec((1, D), lambda b, i, j: (0, 0))
    return pl.pallas_call(
        functools.partial(_ffn_body, alpha=alpha, coef=coef, n_ff=n_ff),
        grid=(B, T // tm, n_ff),
        in_specs=[pl.BlockSpec((None, tm, D), lambda b, i, j: (b, i, 0)), row, row, row,
                  pl.BlockSpec((D, tf), lambda b, i, j: (0, j)),
                  pl.BlockSpec((D, tf), lambda b, i, j: (0, j)),
                  pl.BlockSpec((tf, D), lambda b, i, j: (j, 0)), vec, vec],
        out_specs=pl.BlockSpec((None, tm, D), lambda b, i, j: (b, i, 0)),
        out_shape=jax.ShapeDtypeStruct((B, T, D), F32),
        scratch_shapes=[pltpu.VMEM((tm, D), BF16), pltpu.VMEM((tm, D), F32)],
        compiler_params=_cparams(("parallel", "parallel", "arbitrary")),
        name="ffn",
    )(x, sh, sc, g, wg, wu, wd, lng.reshape(1, D), lnb.reshape(1, D))


def _rope_slot(x, c, sa, sb, half):
    return x * c + pltpu.roll(x, LANES - half, 1) * sa + pltpu.roll(x, half, 1) * sb


def _rms_norm(x, g):
    return x * lax.rsqrt(jnp.mean(x * x, -1, keepdims=True) + EPS) * g


def _proj_body(x_ref, sh_ref, sc_ref, w_ref, gq_ref, gkv_ref, wuq_ref, wuk_ref, wuv_ref,
               base_ref, tile_ref, sgn_ref,
               qn_ref, kvc_ref, kvs_ref, kvw_ref, gate_ref, qm_ref, km_ref, vm_ref,
               *, nsa_scale, mla_scale):
    u = (x_ref[...] * (1.0 + sc_ref[...]) + sh_ref[...]).astype(BF16)
    h = _dot(u, w_ref[...])
    tabs = []
    for st in range(len(ROPE_SETS)):
        lc, ls = slice(2 * st * LANES, (2 * st + 1) * LANES), slice((2 * st + 1) * LANES, (2 * st + 2) * LANES)
        cb, sb_, ca, sa_ = base_ref[:, lc], base_ref[:, ls], tile_ref[:, lc], tile_ref[:, ls]
        sin = sa_ * cb + ca * sb_
        tabs += [ca * cb - sa_ * sb_, sin * sgn_ref[0:1, lc], sin * sgn_ref[0:1, ls]]
    cq, sqa, sqb, cn, sna, snb, cm, sma, smb = tabs
    nh = NSA_ROPE_DIM // 2
    mh = MLA_ROPE_DIM // 2

    for s in range(NSA_HEADS * NSA_HEAD_DIM // LANES):
        lo = OFF_Q + s * LANES
        qn_ref[:, s * LANES:(s + 1) * LANES] = (
            _rope_slot(h[:, lo:lo + LANES], cq, sqa, sqb, nh) * nsa_scale).astype(BF16)
    for idx, ref in enumerate((kvc_ref, kvs_ref, kvw_ref)):
        for g in range(NSA_KV_GROUPS):
            lo = OFF_KV + (idx * NSA_KV_GROUPS + g) * LANES
            ref[g] = _rope_slot(h[:, lo:lo + LANES], cn, sna, snb, nh).astype(BF16)
    gate_ref[...] = jax.nn.sigmoid(h[:, OFF_GATE:OFF_GATE + NSA_KV_GROUPS * LANES])

    cqn = _rms_norm(h[:, OFF_CQ:OFF_CQ + MLA_Q_RANK], gq_ref[...]).astype(BF16)
    qm = _dot(cqn, wuq_ref[...])
    ckvn = _rms_norm(h[:, OFF_CKV:OFF_CKV + MLA_KV_RANK], gkv_ref[...]).astype(BF16)
    kn = _dot(ckvn, wuk_ref[...])
    vm_ref[...] = _dot(ckvn, wuv_ref[...]).astype(BF16)
    kr = _rope_slot(h[:, OFF_KR:OFF_KR + LANES], cm, sma, smb, mh)
    for s in range(MLA_HEADS):
        sl = slice(s * LANES, (s + 1) * LANES)
        qm_ref[:, sl] = (_rope_slot(qm[:, sl], cm, sma, smb, mh) * mla_scale).astype(BF16)
        km_ref[:, sl] = (kn[:, sl] + kr).astype(BF16)


def _proj_call(x, sh, sc, w_all, gq, gkv, wuq, wuk, wuv):
    B, T, D = x.shape
    tm = min(T, 512)
    G = NSA_KV_GROUPS
    base, tile_, signs = _rope_tables(T, tm)
    row = pl.BlockSpec((None, 1, D), lambda b, i: (b, 0, 0))

    def full(a):
        return pl.BlockSpec(a.shape, lambda b, i: (0,) * a.ndim)

    tile_spec = pl.BlockSpec((None, 1, tile_.shape[-1]), lambda b, i: (i, 0, 0))
    tok = lambda w: pl.BlockSpec((None, tm, w), lambda b, i: (b, i, 0))
    grp = pl.BlockSpec((G, None, tm, LANES), lambda b, i: (0, b, i, 0))
    sd = jax.ShapeDtypeStruct
    return pl.pallas_call(
        functools.partial(_proj_body, nsa_scale=NSA_HEAD_DIM ** -0.5 * LOG2E,
                          mla_scale=(MLA_NOPE_DIM + MLA_ROPE_DIM) ** -0.5 * LOG2E),
        grid=(B, T // tm),
        in_specs=[tok(D), row, row, full(w_all), full(gq), full(gkv), full(wuq), full(wuk), full(wuv)]
                 + [full(base), tile_spec, full(signs)],
        out_specs=[tok(NSA_HEADS * NSA_HEAD_DIM), grp, grp, grp, tok(G * LANES),
                   tok(MLA_HEADS * LANES), tok(MLA_HEADS * LANES), tok(MLA_HEADS * MLA_V_DIM)],
        out_shape=[sd((B, T, NSA_HEADS * NSA_HEAD_DIM), BF16), sd((G, B, T, LANES), BF16),
                   sd((G, B, T, LANES), BF16), sd((G, B, T, LANES), BF16), sd((B, T, G * LANES), F32),
                   sd((B, T, MLA_HEADS * LANES), BF16), sd((B, T, MLA_HEADS * LANES), BF16),
                   sd((B, T, MLA_HEADS * MLA_V_DIM), BF16)],
        compiler_params=_cparams(("parallel", "parallel")),
        name="proj",
    )(x, sh, sc, w_all, gq, gkv, wuq, wuk, wuv, base, tile_, signs)


def _gelu_tanh(x):
    return x * (0.5 * (1.0 + jnp.tanh(0.7978845608028654 * (x + 0.044715 * (x * x * x)))))


def _cmp_body(c_ref, w1a_ref, w1b_ref, pa_ref, pb_ref, w2_ref, o_ref):
    c = c_ref[...]
    nc = c.shape[0]
    w1a, w1b = w1a_ref[...], w1b_ref[...]
    a = _dot(c, w1a)
    b = _dot(c, w1b)
    bias = _dot(pa_ref[...], w1a) + _dot(pb_ref[...], w1b)
    pre = a + pltpu.roll(b, nc - 1, 0) + bias[0:1]
    o_ref[...] = _dot(_gelu_tanh(pre).astype(BF16), w2_ref[...]).astype(BF16)


def _cmp_call(kvc, w1a, w1b, pa, pb, w2):
    G, B, T, _ = kvc.shape
    nc = T // CMP_STRIDE
    c = kvc.reshape(G, B, nc, CMP_STRIDE * LANES)

    def full(a):
        return pl.BlockSpec(a.shape, lambda g, b: (0,) * a.ndim)

    return pl.pallas_call(
        _cmp_body,
        grid=(G, B),
        in_specs=[pl.BlockSpec((None, None, nc, CMP_STRIDE * LANES), lambda g, b: (g, b, 0, 0)),
                  full(w1a), full(w1b), full(pa), full(pb), full(w2)],
        out_specs=pl.BlockSpec((None, None, nc, LANES), lambda g, b: (g, b, 0, 0)),
        out_shape=jax.ShapeDtypeStruct((G, B, nc, LANES), BF16),
        compiler_params=_cparams(("parallel", "parallel")),
        name="cmp",
    )(c, w1a, w1b, pa, pb, w2)


def _nsa_body(q_ref, gate_ref, kvc_ref, kvs_ref, kvw_ref, ov_ref, et_ref, band_ref, o_ref,
              m_scr, l_scr, acc_scr, part_scr, *, n_slc):
    i = pl.program_id(2)
    nc = kvc_ref.shape[0]
    hpg, tq_n, rows, tk = NSA_HPG, NSA_TQ, NSA_HPG * NSA_TQ, NSA_TK
    t0 = pl.multiple_of(i * tq_n, tq_n)
    q = q_ref[...].astype(F32)
    low = _iota((tq_n, LANES), 1) < NSA_HEAD_DIM
    heads = []
    for h in range(hpg):
        tile_ = q[:, (h // 2) * LANES:(h // 2 + 1) * LANES]
        heads.append(jnp.where(low, tile_ if h % 2 == 0 else pltpu.roll(tile_, NSA_HEAD_DIM, 1), 0.0))
    qs = jnp.concatenate(heads, axis=0).astype(BF16)

    wk = WINDOW + tq_n
    start = pl.multiple_of(jnp.maximum(t0 - WINDOW, 0), tq_n)
    kvw = kvw_ref[pl.ds(start, wk), :]
    s_w = _dot_t(qs, kvw)
    s_w = s_w + _rep_rows(band_ref[...], hpg)
    e_w = jnp.exp2(s_w - jnp.max(s_w, -1, keepdims=True))
    o_win = _dot(e_w.astype(BF16), kvw) / jnp.sum(e_w, -1, keepdims=True)

    kvc = kvc_ref[...]
    s = _dot_t(qs, kvc)
    tq = t0 + _iota((tq_n, nc), 0)
    n = _iota((tq_n, nc), 1)
    valid = (n * CMP_STRIDE + (CMP_BLOCK - 1) <= tq) & (n < nc - 1)
    s = s + _rep_rows(jnp.where(valid, 0.0, NEG), hpg)
    e = jnp.exp2(s - jnp.maximum(jnp.max(s, -1, keepdims=True), M_INIT))
    p = e / jnp.maximum(jnp.sum(e, -1, keepdims=True), 1e-30)
    o_cmp = _dot(p.astype(BF16), kvc)

    ps = p[0:tq_n] + p[tq_n:2 * tq_n] + p[2 * tq_n:3 * tq_n] + p[3 * tq_n:4 * tq_n]
    ph = ps.astype(BF16)
    plo = (ps - ph.astype(F32)).astype(BF16)
    ov = ov_ref[...]
    imp_t = (_dot(ph, ov) + _dot(plo, ov)).T

    jr = _iota((LANES, tq_n), 0)
    tl = t0 + _iota((LANES, tq_n), 1)
    cur = lax.shift_right_logical(tl, 6)
    causal = jr * SLC_BLOCK <= tl
    forced = ((jr == 0) | (jr == cur) | (jr == cur - 1)) & causal
    v = jnp.where(forced, PICKED, jnp.where(causal, imp_t, -1.0))
    if n_slc < LANES:
        v = jnp.where(jr < n_slc, v, PICKED)
    jf = jr.astype(F32)
    sel_t = jnp.where(forced, 1.0, 0.0)
    n_forced = 3
    for _ in range(SLC_TOPK - n_forced):
        mx = jnp.max(v, axis=0, keepdims=True)
        idx = jnp.min(jnp.where(v == mx, jf, float(LANES)), axis=0, keepdims=True)
        hit = jf == idx
        sel_t = jnp.where(hit, 1.0, sel_t)
        v = jnp.where(hit, PICKED, v)
    sel = sel_t.T

    gt = gate_ref[...]
    gcol = lambda br: jnp.concatenate([gt[:, h * 3 + br:h * 3 + br + 1] for h in range(hpg)], axis=0)
    part_scr[...] = gcol(0) * o_cmp + gcol(2) * o_win

    upto = _iota((tq_n, LANES), 1) < lax.shift_right_logical(t0 + tq_n, 6)
    lhs = jnp.concatenate([qs, _rep_rows(jnp.where((sel > 0.5) & upto, 0.0, NEG).astype(BF16), hpg)], axis=1)
    tk_shift = tk.bit_length() - 1
    n_full = lax.shift_right_logical(t0 + tq_n - 1, tk_shift)
    tail_idx = lax.shift_right_logical(t0 + tq_n - 1 - n_full * tk, tq_n.bit_length() - 1)

    m_scr[...] = jnp.full(m_scr.shape, M_INIT, F32)
    l_scr[...] = jnp.zeros(l_scr.shape, F32)
    acc_scr[...] = jnp.zeros(acc_scr.shape, F32)
    rc = ROW_CHUNK

    def tile(k, width, last):
        k0 = pl.multiple_of(k * tk, tk)
        kv = kvs_ref[pl.ds(k0, width), :]
        rhs = jnp.concatenate([kv, et_ref[pl.ds(k0, width), :]], axis=1)
        s_all = _dot_t(lhs, rhs)
        ps, als = [], []
        for r in range(0, rows, rc):
            sl = slice(r, r + rc)
            sc = s_all[sl]
            if last:
                own = jnp.where(_iota((rc, tq_n), 1) <= (r % tq_n) + _iota((rc, tq_n), 0), 0.0, NEG)
                sc = sc + own if width == tq_n else jnp.concatenate(
                    [sc[:, :width - tq_n], sc[:, width - tq_n:] + own], axis=1)
            m_prev = m_scr[sl]
            m_new = jnp.maximum(m_prev, jnp.max(sc, -1, keepdims=True))
            alpha = jnp.exp2(m_prev - m_new)
            pp = jnp.exp2(sc - _rep_lanes(m_new, width // LANES))
            l_scr[sl] = alpha * l_scr[sl] + jnp.sum(pp, -1, keepdims=True)
            m_scr[sl] = m_new
            ps.append(pp.astype(BF16))
            als.append(alpha)
        acc_scr[...] = jnp.concatenate(als, axis=0) * acc_scr[...] + _dot(jnp.concatenate(ps, axis=0), kv)

    def body(k, carry):
        tile(k, tk, False)
        return carry

    lax.fori_loop(0, n_full, body, 0)
    for j in range(tk // tq_n):
        pl.when(tail_idx == j)(functools.partial(tile, n_full, (j + 1) * tq_n, True))
    acc = acc_scr[...]
    out = part_scr[...] + gcol(1) * (acc / l_scr[...])
    out = jnp.where(_iota((rows, LANES), 1) >= NSA_HEAD_DIM, out, 0.0)
    for hp in range(hpg // 2):
        even = out[2 * hp * tq_n:(2 * hp + 1) * tq_n]
        odd = out[(2 * hp + 1) * tq_n:(2 * hp + 2) * tq_n]
        o_ref[:, hp * LANES:(hp + 1) * LANES] = (pltpu.roll(even, NSA_HEAD_DIM, 1) + odd).astype(BF16)


def _nsa_call(qn, gates, kvcmp, kvs, kvw, ov, et, band):
    B, T, _ = qn.shape
    G = NSA_KV_GROUPS
    nc = kvcmp.shape[2]
    n_slc = T // SLC_BLOCK
    assert SLC_TOPK <= n_slc <= LANES and nc % LANES == 0 and T % NSA_TK == 0 and T >= WINDOW + NSA_TQ
    assert NSA_TK & (NSA_TK - 1) == 0 and NSA_TQ % SLC_BLOCK == 0 and NSA_TK % NSA_TQ == 0
    assert WINDOW % NSA_TQ == 0 and band.shape == (WINDOW // NSA_TQ + 1, NSA_TQ, WINDOW + NSA_TQ)
    gw = NSA_HPG * NSA_HEAD_DIM
    rows = NSA_HPG * NSA_TQ
    whole = lambda r: pl.BlockSpec((None, None, r, LANES), lambda b, g, i: (g, b, 0, 0))
    const = lambda a: pl.BlockSpec(a.shape, lambda b, g, i: (0, 0))
    return pl.pallas_call(
        functools.partial(_nsa_body, n_slc=n_slc),
        grid=(B, G, T // NSA_TQ),
        in_specs=[pl.BlockSpec((None, NSA_TQ, gw), lambda b, g, i: (b, i, g)),
                  pl.BlockSpec((None, NSA_TQ, LANES), lambda b, g, i: (b, i, g)),
                  whole(nc), whole(T), whole(T), const(ov), const(et),
                  pl.BlockSpec((None,) + band.shape[1:], lambda b, g, i: (jnp.minimum(i, band.shape[0] - 1), 0, 0))],
        out_specs=pl.BlockSpec((None, NSA_TQ, NSA_HPG * NSA_HEAD_DIM), lambda b, g, i: (b, i, g)),
        out_shape=jax.ShapeDtypeStruct((B, T, NSA_HEADS * NSA_HEAD_DIM), BF16),
        scratch_shapes=[pltpu.VMEM((rows, LANES), F32)] * 4,
        compiler_params=_cparams(("parallel", "parallel", "arbitrary")),
        name="nsa",
    )(qn, gates, kvcmp, kvs, kvw, ov, et, band)


def _mla_body(q_ref, k_ref, v_ref, o_ref, p_buf, a_buf, m_scr, l_scr, acc_scr):
    i = pl.program_id(2)
    tt, nh = MLA_T, MLA_HPS
    head = lambda hh: slice(hh * LANES, (hh + 1) * LANES)

    pair = lambda hh: slice((hh // 2) * LANES, (hh // 2 + 1) * LANES)

    def scores(kt, hh):
        k0 = pl.multiple_of(kt * tt, tt)
        return _dot_t(q_ref[:, head(hh)], k_ref[pl.ds(k0, tt), head(hh)])

    def values(kt, hh):
        return v_ref[pl.ds(pl.multiple_of(kt * tt, tt), tt), pair(hh)]

    m_scr[...] = jnp.full(m_scr.shape, M_INIT, F32)
    l_scr[...] = jnp.zeros(l_scr.shape, F32)
    acc_scr[...] = jnp.zeros(acc_scr.shape, F32)
    p_buf[...] = jnp.zeros(p_buf.shape, BF16)
    a_buf[...] = jnp.ones(a_buf.shape, F32)

    def stage(k, diagonal):
        kp = jnp.maximum(k - 1, 0)
        if diagonal:
            tri = jnp.where(_iota((tt, tt), 1) <= _iota((tt, tt), 0), 0.0, NEG)
        for hh in range(nh):
            s = scores(k, hh)
            if diagonal:
                s = s + tri
            acc_scr[hh] = a_buf[hh] * acc_scr[hh] + _dot(p_buf[hh], values(kp, hh))
            m_prev = m_scr[hh]
            m_new = jnp.maximum(m_prev, jnp.max(s, -1, keepdims=True))
            alpha = jnp.exp2(m_prev - m_new)
            p = jnp.exp2(s - _rep_lanes(m_new, tt // LANES))
            l_scr[hh] = alpha * l_scr[hh] + jnp.sum(p, -1, keepdims=True)
            m_scr[hh] = m_new
            a_buf[hh] = alpha
            p_buf[hh] = p.astype(BF16)

    def body(k, carry):
        stage(k, False)
        return carry

    lax.fori_loop(0, i, body, 0)
    stage(i, True)
    outs = [(a_buf[hh] * acc_scr[hh] + _dot(p_buf[hh], values(i, hh))) / l_scr[hh] for hh in range(nh)]
    first = _iota((tt, LANES), 1) < MLA_V_DIM
    for pp in range(nh // 2):
        o_ref[:, pp * LANES:(pp + 1) * LANES] = jnp.where(first, outs[2 * pp], outs[2 * pp + 1]).astype(BF16)


def _mla_call(qm, km, vm):
    B, T, _ = qm.shape
    nh = MLA_HPS
    assert T % MLA_T == 0 and MLA_HEADS % nh == 0 and nh % 2 == 0
    vw = nh * MLA_V_DIM
    return pl.pallas_call(
        _mla_body,
        grid=(B, MLA_HEADS // nh, T // MLA_T),
        in_specs=[pl.BlockSpec((None, MLA_T, nh * LANES), lambda b, p, i: (b, i, p)),
                  pl.BlockSpec((None, T, nh * LANES), lambda b, p, i: (b, 0, p)),
                  pl.BlockSpec((None, T, vw), lambda b, p, i: (b, 0, p))],
        out_specs=pl.BlockSpec((None, MLA_T, vw), lambda b, p, i: (b, i, p)),
        out_shape=jax.ShapeDtypeStruct((B, T, MLA_HEADS * MLA_V_DIM), BF16),
        scratch_shapes=[pltpu.VMEM((nh, MLA_T, MLA_T), BF16)] + [pltpu.VMEM((nh, MLA_T, LANES), F32)] * 4,
        compiler_params=_cparams(("parallel", "parallel", "arbitrary"), MLA_VMEM_LIMIT),
        name="mla",
    )(qm, km, vm)


def _out_body(x_ref, on_ref, om_ref, g_ref, wn_ref, wm_ref, lng_ref, lnb_ref, o_ref, *, alpha):
    y = _dot(on_ref[...], wn_ref[...]) + _dot(om_ref[...], wm_ref[...])
    z = alpha * x_ref[...] + (1.0 + g_ref[...]) * y
    o_ref[...] = _layer_norm(z, lng_ref[...], lnb_ref[...])


def _out_call(x, o_nsa, o_mla, g, w_on, w_om, lng, lnb, alpha):
    B, T, D = x.shape
    tm = min(T, 512)
    tok = lambda w: pl.BlockSpec((None, tm, w), lambda b, i: (b, i, 0))
    full = lambda a: pl.BlockSpec(a.shape, lambda b, i: (0,) * a.ndim)
    lng, lnb = lng.reshape(1, D), lnb.reshape(1, D)
    return pl.pallas_call(
        functools.partial(_out_body, alpha=alpha),
        grid=(B, T // tm),
        in_specs=[tok(D), tok(o_nsa.shape[-1]), tok(o_mla.shape[-1]),
                  pl.BlockSpec((None, 1, D), lambda b, i: (b, 0, 0)),
                  full(w_on), full(w_om), full(lng), full(lnb)],
        out_specs=tok(D),
        out_shape=jax.ShapeDtypeStruct((B, T, D), F32),
        compiler_params=_cparams(("parallel", "parallel")),
        name="out",
    )(x, o_nsa, o_mla, g, w_on, w_om, lng, lnb)


ROPE_SETS = (((0, NSA_HEAD_DIM), NSA_ROPE_DIM),
             ((0,), NSA_ROPE_DIM),
             ((MLA_NOPE_DIM,), MLA_ROPE_DIM))
def _rope_tables(T, tm):
    inv_rows, sign_rows = [], []
    for lane0s, dim in ROPE_SETS:
        half = dim // 2
        inv = ROPE_THETA ** (-jnp.arange(half, dtype=F32) / half)
        lane_inv = jnp.zeros((LANES,), F32)
        first, second = np.zeros((LANES,), np.float32), np.zeros((LANES,), np.float32)
        for lane0 in lane0s:
            lane_inv = lane_inv.at[lane0:lane0 + half].set(inv).at[lane0 + half:lane0 + dim].set(inv)
            first[lane0:lane0 + half] = -1.0
            second[lane0 + half:lane0 + dim] = 1.0
        inv_rows.append(lane_inv)
        sign_rows += [first, second]
    inv_all = jnp.stack(inv_rows)

    def trig(pos):
        ang = pos.astype(F32)[:, None, None] * inv_all[None]
        return jnp.stack([jnp.cos(ang), jnp.sin(ang)], axis=2).reshape(pos.shape[0], -1)

    signs = jnp.broadcast_to(jnp.asarray(np.concatenate(sign_rows))[None, :], (8, 2 * len(ROPE_SETS) * LANES))
    return trig(jnp.arange(tm)), trig(jnp.arange(0, T, tm))[:, None, :], signs


def _pad_slots(w, n, width):
    K = w.shape[0]
    return jnp.pad(w.reshape(K, n, width), ((0, 0), (0, 0), (0, LANES - width))).reshape(K, n * LANES)


def _proj_weights(w_in):
    D = w_in.shape[0]
    hd, G = NSA_HEAD_DIM, NSA_KV_GROUPS
    nq, nkv = NSA_HEADS * hd, G * hd
    off = np.cumsum([0, nq] + [nkv] * 6 + [3 * NSA_HEADS, MLA_Q_RANK, MLA_KV_RANK, MLA_ROPE_DIM])
    cols = [w_in[:, :nq]]
    for pair in range(3):
        ko, vo = off[1 + 2 * pair], off[2 + 2 * pair]
        for g in range(G):
            cols += [w_in[:, ko + g * hd:ko + (g + 1) * hd], w_in[:, vo + g * hd:vo + (g + 1) * hd]]
    per_g = 3 * NSA_HPG
    for g in range(G):
        cols.append(jnp.pad(w_in[:, off[7] + g * per_g:off[7] + (g + 1) * per_g], ((0, 0), (0, LANES - per_g))))
    cols += [w_in[:, off[8]:off[9]], w_in[:, off[9]:off[10]]]
    cols.append(jnp.pad(w_in[:, off[10]:off[11]],
                        ((0, 0), (MLA_NOPE_DIM, LANES - MLA_NOPE_DIM - MLA_ROPE_DIM))))
    w_all = jnp.concatenate(cols, axis=1).astype(BF16)
    assert w_all.shape == (D, PROJ_W)
    return w_all


def _cmp_weights(k_pos, k_w1, k_w2, v_pos, v_w1, v_w2):
    hd, st, hid = NSA_HEAD_DIM, CMP_STRIDE, CMP_HIDDEN

    def half(lo):
        w = jnp.zeros((st, LANES, 2 * hid), F32)
        w = w.at[:, :hd, :hid].set(k_w1[lo * hd:(lo + st) * hd].reshape(st, hd, hid))
        w = w.at[:, hd:, hid:].set(v_w1[lo * hd:(lo + st) * hd].reshape(st, hd, hid))
        pos = jnp.concatenate([k_pos[lo:lo + st], v_pos[lo:lo + st]], axis=1).reshape(1, st * LANES)
        return w.reshape(st * LANES, 2 * hid).astype(BF16), jnp.broadcast_to(pos, (8, st * LANES)).astype(BF16)

    w1a, pa = half(0)
    w1b, pb = half(st)
    w2 = jnp.zeros((2 * hid, LANES), F32).at[:hid, :hd].set(k_w2).at[hid:, hd:].set(v_w2).astype(BF16)
    return w1a, w1b, pa, pb, w2


def _block_tables(T):
    nc, n_slc = T // CMP_STRIDE, T // SLC_BLOCK
    cs = np.arange(nc)[:, None] * CMP_STRIDE
    blk = np.arange(LANES)[None, :]
    ov = (cs < blk * SLC_BLOCK + SLC_BLOCK) & (cs + CMP_BLOCK > blk * SLC_BLOCK) & (np.arange(nc)[:, None] < nc - 1) & (blk < n_slc)
    et = (np.arange(T)[:, None] // SLC_BLOCK) == blk
    off = np.arange(WINDOW // NSA_TQ + 1)[:, None, None] * NSA_TQ
    diff = off + np.arange(NSA_TQ)[None, :, None] - np.arange(WINDOW + NSA_TQ)[None, None, :]
    band = np.where((diff >= 0) & (diff < WINDOW), 0.0, NEG).astype(np.float32)
    return jnp.asarray(ov.astype(np.float32), BF16), jnp.asarray(et.astype(np.float32), BF16), jnp.asarray(band)


def kernel(x, c, w_ada, b_ada, ln_g, ln_b, ffn1_wg, ffn1_wu, ffn1_wd, w_in, cmp_k_pos, cmp_k_w1, cmp_k_w2,
           cmp_v_pos, cmp_v_w1, cmp_v_w2, mla_q_norm, mla_kv_norm, mla_w_uq, mla_w_ukv, w_out,
           ffn2_wg, ffn2_wu, ffn2_wd):
    B, T, D = x.shape
    depth = w_ada.shape[0]
    alpha = (2.0 * depth) ** 0.25
    ov, et, band = _block_tables(T)
    for l in range(depth):
        mod = _mod_call(c, w_ada[l], b_ada[l])
        sh1, sc1, g1, sh2, sc2, g2, sh3, sc3, g3 = [mod[:, k] for k in range(N_MOD)]
        x = _ffn_call(x, sh1, sc1, g1, ffn1_wg[l], ffn1_wu[l], ffn1_wd[l],
                      ln_g[l, 0], ln_b[l, 0], alpha, 0.5)

        wuq = _pad_slots(mla_w_uq[l], MLA_HEADS, MLA_NOPE_DIM + MLA_ROPE_DIM).astype(BF16)
        wukv = mla_w_ukv[l].reshape(MLA_KV_RANK, MLA_HEADS, MLA_NOPE_DIM + MLA_V_DIM)
        wuk = _pad_slots(wukv[:, :, :MLA_NOPE_DIM].reshape(MLA_KV_RANK, -1), MLA_HEADS, MLA_NOPE_DIM).astype(BF16)
        wuv = wukv[:, :, MLA_NOPE_DIM:].reshape(MLA_KV_RANK, -1).astype(BF16)
        qn, kvc, kvs, kvw, gates, qm, km, vm = _proj_call(
            x, sh2, sc2, _proj_weights(w_in[l]), mla_q_norm[l].reshape(1, -1), mla_kv_norm[l].reshape(1, -1),
            wuq, wuk, wuv)
        kvcmp = _cmp_call(kvc, *_cmp_weights(cmp_k_pos[l], cmp_k_w1[l], cmp_k_w2[l],
                                             cmp_v_pos[l], cmp_v_w1[l], cmp_v_w2[l]))
        o_nsa = _nsa_call(qn, gates, kvcmp, kvs, kvw, ov, et, band)
        o_mla = _mla_call(qm, km, vm)
        n_nsa = NSA_HEADS * NSA_HEAD_DIM
        w_o = w_out[l].astype(BF16)
        x = _out_call(x, o_nsa, o_mla, g2, w_o[:n_nsa], w_o[n_nsa:], ln_g[l, 1], ln_b[l, 1], alpha)

        x = _ffn_call(x, sh3, sc3, g3, ffn2_wg[l], ffn2_wu[l], ffn2_wd[l],
                      ln_g[l, 2], ln_b[l, 2], alpha, 0.5)
    return x
```

```python
import functools

import numpy as np
import jax
import jax.numpy as jnp
from jax import lax
from jax.experimental import pallas as pl
from jax.experimental.pallas import tpu as pltpu

F32 = jnp.float32
BF16 = jnp.bfloat16

ROPE_THETA = 500000.0
NSA_HEADS = 8
NSA_KV_GROUPS = 2
NSA_HPG = NSA_HEADS // NSA_KV_GROUPS
NSA_HEAD_DIM = 64
NSA_ROPE_DIM = 16
CMP_BLOCK = 32
CMP_STRIDE = 16
CMP_HIDDEN = 256
SLC_BLOCK = 64
SLC_TOPK = 16
WINDOW = 512
FORCE_SCORE = 1e9
MLA_HEADS = 8
MLA_NOPE_DIM = 64
MLA_ROPE_DIM = 32
MLA_V_DIM = 64
MLA_Q_RANK = 384
MLA_KV_RANK = 256
EPS = 1e-5
N_MOD = 9

LANES = 128
NEG = -1e30
M_INIT = -1e29
PICKED = -3e38
ROW_CHUNK = 128
LOG2E = 1.4426950408889634
VMEM_LIMIT = 48 * 1024 * 1024
MLA_VMEM_LIMIT = 56 * 1024 * 1024
FFN_VMEM_LIMIT = 56 * 1024 * 1024

NSA_TQ = 256
NSA_TK = 1024
MLA_T = 512
MLA_HPS = 4

OFF_Q, OFF_KV, OFF_GATE, OFF_CQ, OFF_CKV, OFF_KR, PROJ_W = 0, 512, 1280, 1536, 1920, 2176, 2304


def _cparams(sem, vmem=VMEM_LIMIT):
    return pltpu.CompilerParams(dimension_semantics=sem, vmem_limit_bytes=vmem)


def _layer_norm(z, g, b):
    mu = jnp.mean(z, -1, keepdims=True)
    zc = z - mu
    var = jnp.mean(zc * zc, -1, keepdims=True)
    return zc * lax.rsqrt(var + EPS) * g + b


def _dot(a, b):
    return jnp.dot(a, b, preferred_element_type=F32)


def _dot_t(a, b):
    return lax.dot_general(a, b, (((1,), (1,)), ((), ())), preferred_element_type=F32)


def _rep_rows(a, n):
    return jnp.concatenate([a] * n, axis=0)


def _rep_lanes(a, n):
    return jnp.concatenate([a] * n, axis=1)


def _iota(shape, d):
    return lax.broadcasted_iota(jnp.int32, shape, d)


def _mod_body(c_ref, w_ref, b_ref, o_ref):
    c = c_ref[...]
    s = (c * jax.nn.sigmoid(c)).astype(BF16)
    o_ref[...] = _dot(s, w_ref[...].astype(BF16)) + b_ref[...]


def _mod_call(c, w_ada, b_ada):
    B, D = c.shape
    rows = 8
    c8 = jnp.zeros((rows, D), F32).at[:B].set(c)
    out = pl.pallas_call(
        _mod_body,
        grid=(N_MOD,),
        in_specs=[pl.BlockSpec((rows, D), lambda j: (0, 0)),
                  pl.BlockSpec((D, D), lambda j: (0, j)),
                  pl.BlockSpec((1, D), lambda j: (0, j))],
        out_specs=pl.BlockSpec((rows, D), lambda j: (0, j)),
        out_shape=jax.ShapeDtypeStruct((rows, N_MOD * D), F32),
        compiler_params=_cparams(("parallel",)),
        name="mod",
    )(c8, w_ada, b_ada.reshape(1, N_MOD * D))
    return out[:B].reshape(B, N_MOD, 1, D)


def _ffn_body(x_ref, sh_ref, sc_ref, g_ref, wg_ref, wu_ref, wd_ref, lng_ref, lnb_ref, o_ref,
              u_scr, acc_scr, wg_s, wu_s, wd_s, *, alpha, coef, n_ff):
    j = pl.program_id(2)

    @pl.when((pl.program_id(0) == 0) & (pl.program_id(1) == 0))
    def _():
        wg_s[j] = wg_ref[...].astype(BF16)
        wu_s[j] = wu_ref[...].astype(BF16)
        wd_s[j] = wd_ref[...].astype(BF16)

    @pl.when(j == 0)
    def _():
        u_scr[...] = (x_ref[...] * (1.0 + sc_ref[...]) + sh_ref[...]).astype(BF16)
        acc_scr[...] = jnp.zeros_like(acc_scr)

    u = u_scr[...]
    a = _dot(u, wg_s[j])
    b = _dot(u, wu_s[j])
    h = (a * jax.nn.sigmoid(a)) * b
    acc_scr[...] += _dot(h.astype(BF16), wd_s[j])

    @pl.when(j == n_ff - 1)
    def _():
        z = alpha * x_ref[...] + coef * (1.0 + g_ref[...]) * acc_scr[...]
        o_ref[...] = _layer_norm(z, lng_ref[...], lnb_ref[...])


def _ffn_call(x, sh, sc, g, wg, wu, wd, lng, lnb, alpha, coef):
    B, T, D = x.shape
    FF = wg.shape[1]
    tm = min(T, 1024)
    tf = 256
    assert T % tm == 0 and FF % tf == 0
    n_ff = FF // tf
    row = pl.BlockSpec((None, 1, D), lambda b, i, j: (b, 0, 0))
    vec = pl.BlockSpec((1, D), lambda b, i, j: (0, 0))
    wtile = lambda b, i, j: jnp.where((b == 0) & (i == 0), j, n_ff - 1)
    return pl.pallas_call(
        functools.partial(_ffn_body, alpha=alpha, coef=coef, n_ff=n_ff),
        grid=(B, T // tm, n_ff),
        in_specs=[pl.BlockSpec((None, tm, D), lambda b, i, j: (b, i, 0)), row, row, row,
                  pl.BlockSpec((D, tf), lambda b, i, j: (0, wtile(b, i, j))),
                  pl.BlockSpec((D, tf), lambda b, i, j: (0, wtile(b, i, j))),
                  pl.BlockSpec((tf, D), lambda b, i, j: (wtile(b, i, j), 0)), vec, vec],
        out_specs=pl.BlockSpec((None, tm, D), lambda b, i, j: (b, i, 0)),
        out_shape=jax.ShapeDtypeStruct((B, T, D), F32),
        scratch_shapes=[pltpu.VMEM((tm, D), BF16), pltpu.VMEM((tm, D), F32),
                        pltpu.VMEM((n_ff, D, tf), BF16), pltpu.VMEM((n_ff, D, tf), BF16),
                        pltpu.VMEM((n_ff, tf, D), BF16)],
        compiler_params=_cparams(("arbitrary", "arbitrary", "arbitrary"), FFN_VMEM_LIMIT),
        name="ffn",
    )(x, sh, sc, g, wg, wu, wd, lng.reshape(1, D), lnb.reshape(1, D))


def _rope_slot(x, c, sa, sb, half):
    return x * c + pltpu.roll(x, LANES - half, 1) * sa + pltpu.roll(x, half, 1) * sb


def _rms_norm(x, g):
    return x * lax.rsqrt(jnp.mean(x * x, -1, keepdims=True) + EPS) * g


def _proj_body(x_ref, sh_ref, sc_ref, w_ref, gq_ref, gkv_ref, wuq_ref, wuk_ref, wuv_ref,
               base_ref, tile_ref, sgn_ref,
               qn_ref, kvc_ref, kvs_ref, kvw_ref, gate_ref, qm_ref, km_ref, vm_ref,
               *, nsa_scale, mla_scale):
    u = (x_ref[...] * (1.0 + sc_ref[...]) + sh_ref[...]).astype(BF16)
    h = _dot(u, w_ref[...])
    tabs = []
    for st in range(len(ROPE_SETS)):
        lc, ls = slice(2 * st * LANES, (2 * st + 1) * LANES), slice((2 * st + 1) * LANES, (2 * st + 2) * LANES)
        cb, sb_, ca, sa_ = base_ref[:, lc], base_ref[:, ls], tile_ref[:, lc], tile_ref[:, ls]
        sin = sa_ * cb + ca * sb_
        tabs += [ca * cb - sa_ * sb_, sin * sgn_ref[0:1, lc], sin * sgn_ref[0:1, ls]]
    cq, sqa, sqb, cn, sna, snb, cm, sma, smb = tabs
    nh = NSA_ROPE_DIM // 2
    mh = MLA_ROPE_DIM // 2

    for s in range(NSA_HEADS * NSA_HEAD_DIM // LANES):
        lo = OFF_Q + s * LANES
        qn_ref[:, s * LANES:(s + 1) * LANES] = (
            _rope_slot(h[:, lo:lo + LANES], cq, sqa, sqb, nh) * nsa_scale).astype(BF16)
    for idx, ref in enumerate((kvc_ref, kvs_ref, kvw_ref)):
        for g in range(NSA_KV_GROUPS):
            lo = OFF_KV + (idx * NSA_KV_GROUPS + g) * LANES
            ref[g] = _rope_slot(h[:, lo:lo + LANES], cn, sna, snb, nh).astype(BF16)
    gate_ref[...] = jax.nn.sigmoid(h[:, OFF_GATE:OFF_GATE + NSA_KV_GROUPS * LANES])

    cqn = _rms_norm(h[:, OFF_CQ:OFF_CQ + MLA_Q_RANK], gq_ref[...]).astype(BF16)
    qm = _dot(cqn, wuq_ref[...])
    ckvn = _rms_norm(h[:, OFF_CKV:OFF_CKV + MLA_KV_RANK], gkv_ref[...]).astype(BF16)
    kn = _dot(ckvn, wuk_ref[...])
    vm_ref[...] = _dot(ckvn, wuv_ref[...]).astype(BF16)
    kr = _rope_slot(h[:, OFF_KR:OFF_KR + LANES], cm, sma, smb, mh)
    for s in range(MLA_HEADS):
        sl = slice(s * LANES, (s + 1) * LANES)
        qm_ref[:, sl] = (_rope_slot(qm[:, sl], cm, sma, smb, mh) * mla_scale).astype(BF16)
        km_ref[:, sl] = (kn[:, sl] + kr).astype(BF16)


def _proj_call(x, sh, sc, w_all, gq, gkv, wuq, wuk, wuv):
    B, T, D = x.shape
    tm = min(T, 512)
    G = NSA_KV_GROUPS
    base, tile_, signs = _rope_tables(T, tm)
    row = pl.BlockSpec((None, 1, D), lambda b, i: (b, 0, 0))

    def full(a):
        return pl.BlockSpec(a.shape, lambda b, i: (0,) * a.ndim)

    tile_spec = pl.BlockSpec((None, 1, tile_.shape[-1]), lambda b, i: (i, 0, 0))
    tok = lambda w: pl.BlockSpec((None, tm, w), lambda b, i: (b, i, 0))
    grp = pl.BlockSpec((G, None, tm, LANES), lambda b, i: (0, b, i, 0))
    sd = jax.ShapeDtypeStruct
    return pl.pallas_call(
        functools.partial(_proj_body, nsa_scale=NSA_HEAD_DIM ** -0.5 * LOG2E,
                          mla_scale=(MLA_NOPE_DIM + MLA_ROPE_DIM) ** -0.5 * LOG2E),
        grid=(B, T // tm),
        in_specs=[tok(D), row, row, full(w_all), full(gq), full(gkv), full(wuq), full(wuk), full(wuv)]
                 + [full(base), tile_spec, full(signs)],
        out_specs=[tok(NSA_HEADS * NSA_HEAD_DIM), grp, grp, grp, tok(G * LANES),
                   tok(MLA_HEADS * LANES), tok(MLA_HEADS * LANES), tok(MLA_HEADS * MLA_V_DIM)],
        out_shape=[sd((B, T, NSA_HEADS * NSA_HEAD_DIM), BF16), sd((G, B, T, LANES), BF16),
                   sd((G, B, T, LANES), BF16), sd((G, B, T, LANES), BF16), sd((B, T, G * LANES), F32),
                   sd((B, T, MLA_HEADS * LANES), BF16), sd((B, T, MLA_HEADS * LANES), BF16),
                   sd((B, T, MLA_HEADS * MLA_V_DIM), BF16)],
        compiler_params=_cparams(("parallel", "parallel")),
        name="proj",
    )(x, sh, sc, w_all, gq, gkv, wuq, wuk, wuv, base, tile_, signs)


def _gelu_tanh(x):
    return x * (0.5 * (1.0 + jnp.tanh(0.7978845608028654 * (x + 0.044715 * (x * x * x)))))


def _cmp_body(c_ref, w1a_ref, w1b_ref, pa_ref, pb_ref, w2_ref, o_ref):
    c = c_ref[...]
    nc = c.shape[0]
    w1a, w1b = w1a_ref[...], w1b_ref[...]
    a = _dot(c, w1a)
    b = _dot(c, w1b)
    bias = _dot(pa_ref[...], w1a) + _dot(pb_ref[...], w1b)
    pre = a + pltpu.roll(b, nc - 1, 0) + bias[0:1]
    o_ref[...] = _dot(_gelu_tanh(pre).astype(BF16), w2_ref[...]).astype(BF16)


def _cmp_call(kvc, w1a, w1b, pa, pb, w2):
    G, B, T, _ = kvc.shape
    nc = T // CMP_STRIDE
    c = kvc.reshape(G, B, nc, CMP_STRIDE * LANES)

    def full(a):
        return pl.BlockSpec(a.shape, lambda g, b: (0,) * a.ndim)

    return pl.pallas_call(
        _cmp_body,
        grid=(G, B),
        in_specs=[pl.BlockSpec((None, None, nc, CMP_STRIDE * LANES), lambda g, b: (g, b, 0, 0)),
                  full(w1a), full(w1b), full(pa), full(pb), full(w2)],
        out_specs=pl.BlockSpec((None, None, nc, LANES), lambda g, b: (g, b, 0, 0)),
        out_shape=jax.ShapeDtypeStruct((G, B, nc, LANES), BF16),
        compiler_params=_cparams(("parallel", "parallel")),
        name="cmp",
    )(c, w1a, w1b, pa, pb, w2)


def _nsa_body(q_ref, gate_ref, kvc_ref, kvs_ref, kvw_ref, ov_ref, et_ref, band_ref, o_ref,
              m_scr, l_scr, acc_scr, part_scr, *, n_slc):
    i = pl.program_id(2)
    nc = kvc_ref.shape[0]
    hpg, tq_n, rows, tk = NSA_HPG, NSA_TQ, NSA_HPG * NSA_TQ, NSA_TK
    t0 = pl.multiple_of(i * tq_n, tq_n)
    q = q_ref[...].astype(F32)
    low = _iota((tq_n, LANES), 1) < NSA_HEAD_DIM
    heads = []
    for h in range(hpg):
        tile_ = q[:, (h // 2) * LANES:(h // 2 + 1) * LANES]
        heads.append(jnp.where(low, tile_ if h % 2 == 0 else pltpu.roll(tile_, NSA_HEAD_DIM, 1), 0.0))
    qs = jnp.concatenate(heads, axis=0).astype(BF16)

    wk = WINDOW + tq_n
    start = pl.multiple_of(jnp.maximum(t0 - WINDOW, 0), tq_n)
    kvw = kvw_ref[pl.ds(start, wk), :]
    s_w = _dot_t(qs, kvw)
    s_w = s_w + _rep_rows(band_ref[...], hpg)
    e_w = jnp.exp2(s_w - jnp.max(s_w, -1, keepdims=True))
    o_win = _dot(e_w.astype(BF16), kvw) / jnp.sum(e_w, -1, keepdims=True)

    kvc = kvc_ref[...]
    s = _dot_t(qs, kvc)
    tq = t0 + _iota((tq_n, nc), 0)
    n = _iota((tq_n, nc), 1)
    valid = (n * CMP_STRIDE + (CMP_BLOCK - 1) <= tq) & (n < nc - 1)
    s = s + _rep_rows(jnp.where(valid, 0.0, NEG), hpg)
    e = jnp.exp2(s - jnp.maximum(jnp.max(s, -1, keepdims=True), M_INIT))
    p = e / jnp.maximum(jnp.sum(e, -1, keepdims=True), 1e-30)
    o_cmp = _dot(p.astype(BF16), kvc)

    ps = p[0:tq_n] + p[tq_n:2 * tq_n] + p[2 * tq_n:3 * tq_n] + p[3 * tq_n:4 * tq_n]
    ph = ps.astype(BF16)
    plo = (ps - ph.astype(F32)).astype(BF16)
    ov = ov_ref[...]
    imp_t = (_dot(ph, ov) + _dot(plo, ov)).T

    jr = _iota((LANES, tq_n), 0)
    tl = t0 + _iota((LANES, tq_n), 1)
    cur = lax.shift_right_logical(tl, 6)
    causal = jr * SLC_BLOCK <= tl
    forced = ((jr == 0) | (jr == cur) | (jr == cur - 1)) & causal
    v = jnp.where(forced, PICKED, jnp.where(causal, imp_t, -1.0))
    if n_slc < LANES:
        v = jnp.where(jr < n_slc, v, PICKED)
    jf = jr.astype(F32)
    sel_t = jnp.where(forced, 1.0, 0.0)
    n_forced = 3
    for _ in range(SLC_TOPK - n_forced):
        mx = jnp.max(v, axis=0, keepdims=True)
        idx = jnp.min(jnp.where(v == mx, jf, float(LANES)), axis=0, keepdims=True)
        hit = jf == idx
        sel_t = jnp.where(hit, 1.0, sel_t)
        v = jnp.where(hit, PICKED, v)
    sel = sel_t.T

    gt = gate_ref[...]
    gcol = lambda br: jnp.concatenate([gt[:, h * 3 + br:h * 3 + br + 1] for h in range(hpg)], axis=0)
    part_scr[...] = gcol(0) * o_cmp + gcol(2) * o_win

    upto = _iota((tq_n, LANES), 1) < lax.shift_right_logical(t0 + tq_n, 6)
    lhs = jnp.concatenate([qs, _rep_rows(jnp.where((sel > 0.5) & upto, 0.0, NEG).astype(BF16), hpg)], axis=1)
    tk_shift = tk.bit_length() - 1
    n_full = lax.shift_right_logical(t0 + tq_n - 1, tk_shift)
    tail_idx = lax.shift_right_logical(t0 + tq_n - 1 - n_full * tk, tq_n.bit_length() - 1)

    m_scr[...] = jnp.full(m_scr.shape, M_INIT, F32)
    l_scr[...] = jnp.zeros(l_scr.shape, F32)
    acc_scr[...] = jnp.zeros(acc_scr.shape, F32)
    rc = ROW_CHUNK

    def tile(k, width, last):
        k0 = pl.multiple_of(k * tk, tk)
        kv = kvs_ref[pl.ds(k0, width), :]
        rhs = jnp.concatenate([kv, et_ref[pl.ds(k0, width), :]], axis=1)
        s_all = _dot_t(lhs, rhs)
        ps, als = [], []
        for r in range(0, rows, rc):
            sl = slice(r, r + rc)
            sc = s_all[sl]
            if last:
                own = jnp.where(_iota((rc, tq_n), 1) <= (r % tq_n) + _iota((rc, tq_n), 0), 0.0, NEG)
                sc = sc + own if width == tq_n else jnp.concatenate(
                    [sc[:, :width - tq_n], sc[:, width - tq_n:] + own], axis=1)
            m_prev = m_scr[sl]
            m_new = jnp.maximum(m_prev, jnp.max(sc, -1, keepdims=True))
            alpha = jnp.exp2(m_prev - m_new)
            pp = jnp.exp2(sc - _rep_lanes(m_new, width // LANES))
            l_scr[sl] = alpha * l_scr[sl] + jnp.sum(pp, -1, keepdims=True)
            m_scr[sl] = m_new
            ps.append(pp.astype(BF16))
            als.append(alpha)
        acc_scr[...] = jnp.concatenate(als, axis=0) * acc_scr[...] + _dot(jnp.concatenate(ps, axis=0), kv)

    def body(k, carry):
        tile(k, tk, False)
        return carry

    lax.fori_loop(0, n_full, body, 0)
    for j in range(tk // tq_n):
        pl.when(tail_idx == j)(functools.partial(tile, n_full, (j + 1) * tq_n, True))
    acc = acc_scr[...]
    out = part_scr[...] + gcol(1) * (acc / l_scr[...])
    out = jnp.where(_iota((rows, LANES), 1) >= NSA_HEAD_DIM, out, 0.0)
    for hp in range(hpg // 2):
        even = out[2 * hp * tq_n:(2 * hp + 1) * tq_n]
        odd = out[(2 * hp + 1) * tq_n:(2 * hp + 2) * tq_n]
        o_ref[:, hp * LANES:(hp + 1) * LANES] = (pltpu.roll(even, NSA_HEAD_DIM, 1) + odd).astype(BF16)


def _nsa_call(qn, gates, kvcmp, kvs, kvw, ov, et, band):
    B, T, _ = qn.shape
    G = NSA_KV_GROUPS
    nc = kvcmp.shape[2]
    n_slc = T // SLC_BLOCK
    assert SLC_TOPK <= n_slc <= LANES and nc % LANES == 0 and T % NSA_TK == 0 and T >= WINDOW + NSA_TQ
    assert NSA_TK & (NSA_TK - 1) == 0 and NSA_TQ % SLC_BLOCK == 0 and NSA_TK % NSA_TQ == 0
    assert WINDOW % NSA_TQ == 0 and band.shape == (WINDOW // NSA_TQ + 1, NSA_TQ, WINDOW + NSA_TQ)
    gw = NSA_HPG * NSA_HEAD_DIM
    rows = NSA_HPG * NSA_TQ
    whole = lambda r: pl.BlockSpec((None, None, r, LANES), lambda b, g, i: (g, b, 0, 0))
    const = lambda a: pl.BlockSpec(a.shape, lambda b, g, i: (0, 0))
    return pl.pallas_call(
        functools.partial(_nsa_body, n_slc=n_slc),
        grid=(B, G, T // NSA_TQ),
        in_specs=[pl.BlockSpec((None, NSA_TQ, gw), lambda b, g, i: (b, i, g)),
                  pl.BlockSpec((None, NSA_TQ, LANES), lambda b, g, i: (b, i, g)),
                  whole(nc), whole(T), whole(T), const(ov), const(et),
                  pl.BlockSpec((None,) + band.shape[1:], lambda b, g, i: (jnp.minimum(i, band.shape[0] - 1), 0, 0))],
        out_specs=pl.BlockSpec((None, NSA_TQ, NSA_HPG * NSA_HEAD_DIM), lambda b, g, i: (b, i, g)),
        out_shape=jax.ShapeDtypeStruct((B, T, NSA_HEADS * NSA_HEAD_DIM), BF16),
        scratch_shapes=[pltpu.VMEM((rows, LANES), F32)] * 4,
        compiler_params=_cparams(("parallel", "parallel", "arbitrary")),
        name="nsa",
    )(qn, gates, kvcmp, kvs, kvw, ov, et, band)


def _mla_body(q_ref, k_ref, v_ref, o_ref, p_buf, a_buf, m_scr, l_scr, acc_scr):
    i = pl.program_id(2)
    tt, nh = MLA_T, MLA_HPS
    head = lambda hh: slice(hh * LANES, (hh + 1) * LANES)

    pair = lambda hh: slice((hh // 2) * LANES, (hh // 2 + 1) * LANES)

    def scores(kt, hh):
        k0 = pl.multiple_of(kt * tt, tt)
        return _dot_t(q_ref[:, head(hh)], k_ref[pl.ds(k0, tt), head(hh)])

    def values(kt, hh):
        return v_ref[pl.ds(pl.multiple_of(kt * tt, tt), tt), pair(hh)]

    m_scr[...] = jnp.full(m_scr.shape, M_INIT, F32)
    l_scr[...] = jnp.zeros(l_scr.shape, F32)
    acc_scr[...] = jnp.zeros(acc_scr.shape, F32)
    p_buf[...] = jnp.zeros(p_buf.shape, BF16)
    a_buf[...] = jnp.ones(a_buf.shape, F32)

    def stage(k, diagonal):
        kp = jnp.maximum(k - 1, 0)
        if diagonal:
            tri = jnp.where(_iota((tt, tt), 1) <= _iota((tt, tt), 0), 0.0, NEG)
        for hh in range(nh):
            s = scores(k, hh)
            if diagonal:
                s = s + tri
            acc_scr[hh] = a_buf[hh] * acc_scr[hh] + _dot(p_buf[hh], values(kp, hh))
            m_prev = m_scr[hh]
            m_new = jnp.maximum(m_prev, jnp.max(s, -1, keepdims=True))
            alpha = jnp.exp2(m_prev - m_new)
            p = jnp.exp2(s - _rep_lanes(m_new, tt // LANES))
            l_scr[hh] = alpha * l_scr[hh] + jnp.sum(p, -1, keepdims=True)
            m_scr[hh] = m_new
            a_buf[hh] = alpha
            p_buf[hh] = p.astype(BF16)

    def body(k, carry):
        stage(k, False)
        return carry

    lax.fori_loop(0, i, body, 0)
    stage(i, True)
    outs = [(a_buf[hh] * acc_scr[hh] + _dot(p_buf[hh], values(i, hh))) / l_scr[hh] for hh in range(nh)]
    first = _iota((tt, LANES), 1) < MLA_V_DIM
    for pp in range(nh // 2):
        o_ref[:, pp * LANES:(pp + 1) * LANES] = jnp.where(first, outs[2 * pp], outs[2 * pp + 1]).astype(BF16)


def _mla_call(qm, km, vm):
    B, T, _ = qm.shape
    nh = MLA_HPS
    assert T % MLA_T == 0 and MLA_HEADS % nh == 0 and nh % 2 == 0
    vw = nh * MLA_V_DIM
    return pl.pallas_call(
        _mla_body,
        grid=(B, MLA_HEADS // nh, T // MLA_T),
        in_specs=[pl.BlockSpec((None, MLA_T, nh * LANES), lambda b, p, i: (b, i, p)),
                  pl.BlockSpec((None, T, nh * LANES), lambda b, p, i: (b, 0, p)),
                  pl.BlockSpec((None, T, vw), lambda b, p, i: (b, 0, p))],
        out_specs=pl.BlockSpec((None, MLA_T, vw), lambda b, p, i: (b, i, p)),
        out_shape=jax.ShapeDtypeStruct((B, T, MLA_HEADS * MLA_V_DIM), BF16),
        scratch_shapes=[pltpu.VMEM((nh, MLA_T, MLA_T), BF16)] + [pltpu.VMEM((nh, MLA_T, LANES), F32)] * 4,
        compiler_params=_cparams(("parallel", "parallel", "arbitrary"), MLA_VMEM_LIMIT),
        name="mla",
    )(qm, km, vm)


def _out_body(x_ref, on_ref, om_ref, g_ref, wn_ref, wm_ref, lng_ref, lnb_ref, o_ref, *, alpha):
    y = _dot(on_ref[...], wn_ref[...]) + _dot(om_ref[...], wm_ref[...])
    z = alpha * x_ref[...] + (1.0 + g_ref[...]) * y
    o_ref[...] = _layer_norm(z, lng_ref[...], lnb_ref[...])


def _out_call(x, o_nsa, o_mla, g, w_on, w_om, lng, lnb, alpha):
    B, T, D = x.shape
    tm = min(T, 512)
    tok = lambda w: pl.BlockSpec((None, tm, w), lambda b, i: (b, i, 0))
    full = lambda a: pl.BlockSpec(a.shape, lambda b, i: (0,) * a.ndim)
    lng, lnb = lng.reshape(1, D), lnb.reshape(1, D)
    return pl.pallas_call(
        functools.partial(_out_body, alpha=alpha),
        grid=(B, T // tm),
        in_specs=[tok(D), tok(o_nsa.shape[-1]), tok(o_mla.shape[-1]),
                  pl.BlockSpec((None, 1, D), lambda b, i: (b, 0, 0)),
                  full(w_on), full(w_om), full(lng), full(lnb)],
        out_specs=tok(D),
        out_shape=jax.ShapeDtypeStruct((B, T, D), F32),
        compiler_params=_cparams(("parallel", "parallel")),
        name="out",
    )(x, o_nsa, o_mla, g, w_on, w_om, lng, lnb)


ROPE_SETS = (((0, NSA_HEAD_DIM), NSA_ROPE_DIM),
             ((0,), NSA_ROPE_DIM),
             ((MLA_NOPE_DIM,), MLA_ROPE_DIM))
def _rope_tables(T, tm):
    inv_rows, sign_rows = [], []
    for lane0s, dim in ROPE_SETS:
        half = dim // 2
        inv = ROPE_THETA ** (-jnp.arange(half, dtype=F32) / half)
        lane_inv = jnp.zeros((LANES,), F32)
        first, second = np.zeros((LANES,), np.float32), np.zeros((LANES,), np.float32)
        for lane0 in lane0s:
            lane_inv = lane_inv.at[lane0:lane0 + half].set(inv).at[lane0 + half:lane0 + dim].set(inv)
            first[lane0:lane0 + half] = -1.0
            second[lane0 + half:lane0 + dim] = 1.0
        inv_rows.append(lane_inv)
        sign_rows += [first, second]
    inv_all = jnp.stack(inv_rows)

    def trig(pos):
        ang = pos.astype(F32)[:, None, None] * inv_all[None]
        return jnp.stack([jnp.cos(ang), jnp.sin(ang)], axis=2).reshape(pos.shape[0], -1)

    signs = jnp.broadcast_to(jnp.asarray(np.concatenate(sign_rows))[None, :], (8, 2 * len(ROPE_SETS) * LANES))
    return trig(jnp.arange(tm)), trig(jnp.arange(0, T, tm))[:, None, :], signs


def _pad_slots(w, n, width):
    K = w.shape[0]
    return jnp.pad(w.reshape(K, n, width), ((0, 0), (0, 0), (0, LANES - width))).reshape(K, n * LANES)


def _proj_weights(w_in):
    D = w_in.shape[0]
    hd, G = NSA_HEAD_DIM, NSA_KV_GROUPS
    nq, nkv = NSA_HEADS * hd, G * hd
    off = np.cumsum([0, nq] + [nkv] * 6 + [3 * NSA_HEADS, MLA_Q_RANK, MLA_KV_RANK, MLA_ROPE_DIM])
    cols = [w_in[:, :nq]]
    for pair in range(3):
        ko, vo = off[1 + 2 * pair], off[2 + 2 * pair]
        for g in range(G):
            cols += [w_in[:, ko + g * hd:ko + (g + 1) * hd], w_in[:, vo + g * hd:vo + (g + 1) * hd]]
    per_g = 3 * NSA_HPG
    for g in range(G):
        cols.append(jnp.pad(w_in[:, off[7] + g * per_g:off[7] + (g + 1) * per_g], ((0, 0), (0, LANES - per_g))))
    cols += [w_in[:, off[8]:off[9]], w_in[:, off[9]:off[10]]]
    cols.append(jnp.pad(w_in[:, off[10]:off[11]],
                        ((0, 0), (MLA_NOPE_DIM, LANES - MLA_NOPE_DIM - MLA_ROPE_DIM))))
    w_all = jnp.concatenate(cols, axis=1).astype(BF16)
    assert w_all.shape == (D, PROJ_W)
    return w_all


def _cmp_weights(k_pos, k_w1, k_w2, v_pos, v_w1, v_w2):
    hd, st, hid = NSA_HEAD_DIM, CMP_STRIDE, CMP_HIDDEN

    def half(lo):
        w = jnp.zeros((st, LANES, 2 * hid), F32)
        w = w.at[:, :hd, :hid].set(k_w1[lo * hd:(lo + st) * hd].reshape(st, hd, hid))
        w = w.at[:, hd:, hid:].set(v_w1[lo * hd:(lo + st) * hd].reshape(st, hd, hid))
        pos = jnp.concatenate([k_pos[lo:lo + st], v_pos[lo:lo + st]], axis=1).reshape(1, st * LANES)
        return w.reshape(st * LANES, 2 * hid).astype(BF16), jnp.broadcast_to(pos, (8, st * LANES)).astype(BF16)

    w1a, pa = half(0)
    w1b, pb = half(st)
    w2 = jnp.zeros((2 * hid, LANES), F32).at[:hid, :hd].set(k_w2).at[hid:, hd:].set(v_w2).astype(BF16)
    return w1a, w1b, pa, pb, w2


def _block_tables(T):
    nc, n_slc = T // CMP_STRIDE, T // SLC_BLOCK
    cs = np.arange(nc)[:, None] * CMP_STRIDE
    blk = np.arange(LANES)[None, :]
    ov = (cs < blk * SLC_BLOCK + SLC_BLOCK) & (cs + CMP_BLOCK > blk * SLC_BLOCK) & (np.arange(nc)[:, None] < nc - 1) & (blk < n_slc)
    et = (np.arange(T)[:, None] // SLC_BLOCK) == blk
    off = np.arange(WINDOW // NSA_TQ + 1)[:, None, None] * NSA_TQ
    diff = off + np.arange(NSA_TQ)[None, :, None] - np.arange(WINDOW + NSA_TQ)[None, None, :]
    band = np.where((diff >= 0) & (diff < WINDOW), 0.0, NEG).astype(np.float32)
    return jnp.asarray(ov.astype(np.float32), BF16), jnp.asarray(et.astype(np.float32), BF16), jnp.asarray(band)


def kernel(x, c, w_ada, b_ada, ln_g, ln_b, ffn1_wg, ffn1_wu, ffn1_wd, w_in, cmp_k_pos, cmp_k_w1, cmp_k_w2,
           cmp_v_pos, cmp_v_w1, cmp_v_w2, mla_q_norm, mla_kv_norm, mla_w_uq, mla_w_ukv, w_out,
           ffn2_wg, ffn2_wu, ffn2_wd):
    B, T, D = x.shape
    depth = w_ada.shape[0]
    alpha = (2.0 * depth) ** 0.25
    ov, et, band = _block_tables(T)
    for l in range(depth):
        mod = _mod_call(c, w_ada[l], b_ada[l])
        sh1, sc1, g1, sh2, sc2, g2, sh3, sc3, g3 = [mod[:, k] for k in range(N_MOD)]
        x = _ffn_call(x, sh1, sc1, g1, ffn1_wg[l], ffn1_wu[l], ffn1_wd[l],
                      ln_g[l, 0], ln_b[l, 0], alpha, 0.5)

        wuq = _pad_slots(mla_w_uq[l], MLA_HEADS, MLA_NOPE_DIM + MLA_ROPE_DIM).astype(BF16)
        wukv = mla_w_ukv[l].reshape(MLA_KV_RANK, MLA_HEADS, MLA_NOPE_DIM + MLA_V_DIM)
        wuk = _pad_slots(wukv[:, :, :MLA_NOPE_DIM].reshape(MLA_KV_RANK, -1), MLA_HEADS, MLA_NOPE_DIM).astype(BF16)
        wuv = wukv[:, :, MLA_NOPE_DIM:].reshape(MLA_KV_RANK, -1).astype(BF16)
        qn, kvc, kvs, kvw, gates, qm, km, vm = _proj_call(
            x, sh2, sc2, _proj_weights(w_in[l]), mla_q_norm[l].reshape(1, -1), mla_kv_norm[l].reshape(1, -1),
            wuq, wuk, wuv)
        kvcmp = _cmp_call(kvc, *_cmp_weights(cmp_k_pos[l], cmp_k_w1[l], cmp_k_w2[l],
                                             cmp_v_pos[l], cmp_v_w1[l], cmp_v_w2[l]))
        o_nsa = _nsa_call(qn, gates, kvcmp, kvs, kvw, ov, et, band)
        o_mla = _mla_call(qm, km, vm)
        n_nsa = NSA_HEADS * NSA_HEAD_DIM
        w_o = w_out[l].astype(BF16)
        x = _out_call(x, o_nsa, o_mla, g2, w_o[:n_nsa], w_o[n_nsa:], ln_g[l, 1], ln_b[l, 1], alpha)

        x = _ffn_call(x, sh3, sc3, g3, ffn2_wg[l], ffn2_wu[l], ffn2_wd[l],
                      ln_g[l, 2], ln_b[l, 2], alpha, 0.5)
    return x
```

```python
import functools

import numpy as np
import jax
import jax.numpy as jnp
from jax import lax
from jax.experimental import pallas as pl
from jax.experimental.pallas import tpu as pltpu

F32 = jnp.float32
BF16 = jnp.bfloat16

ROPE_THETA = 500000.0
NSA_HEADS = 8
NSA_KV_GROUPS = 2
NSA_HPG = NSA_HEADS // NSA_KV_GROUPS
NSA_HEAD_DIM = 64
NSA_ROPE_DIM = 16
CMP_BLOCK = 32
CMP_STRIDE = 16
CMP_HIDDEN = 256
SLC_BLOCK = 64
SLC_TOPK = 16
SLC_SHIFT = SLC_BLOCK.bit_length() - 1
WINDOW = 512
FORCE_SCORE = 1e9
MLA_HEADS = 8
MLA_NOPE_DIM = 64
MLA_ROPE_DIM = 32
MLA_V_DIM = 64
MLA_Q_RANK = 384
MLA_KV_RANK = 256
EPS = 1e-5
N_MOD = 9

LANES = 128
NEG = -1e30
M_INIT = -1e29
PICKED = -3e38
ROW_CHUNK = 128
LOG2E = 1.4426950408889634
VMEM_LIMIT = 48 * 1024 * 1024
MLA_VMEM_LIMIT = 56 * 1024 * 1024

NSA_TQ = 256
NSA_TK = 1024
MLA_T = 512
MLA_HPS = 4
OUT_ROWS = 256
PROJ_ROWS = 256

OFF_Q, OFF_KV, OFF_GATE, OFF_CQ, OFF_CKV, OFF_KR, PROJ_W = 0, 512, 1280, 1536, 1920, 2176, 2304


def _cparams(sem, vmem=VMEM_LIMIT):
    return pltpu.CompilerParams(dimension_semantics=sem, vmem_limit_bytes=vmem)


def _layer_norm(z, g, b):
    mu = jnp.mean(z, -1, keepdims=True)
    zc = z - mu
    var = jnp.mean(zc * zc, -1, keepdims=True)
    return zc * lax.rsqrt(var + EPS) * g + b


def _dot(a, b):
    return jnp.dot(a, b, preferred_element_type=F32)


def _dot_t(a, b):
    return lax.dot_general(a, b, (((1,), (1,)), ((), ())), preferred_element_type=F32)


def _rep_rows(a, n):
    return jnp.concatenate([a] * n, axis=0)


def _rep_lanes(a, n):
    return jnp.concatenate([a] * n, axis=1)


def _iota(shape, d):
    return lax.broadcasted_iota(jnp.int32, shape, d)


def _mod_body(c_ref, w_ref, b_ref, o_ref):
    c = c_ref[...]
    s = (c * jax.nn.sigmoid(c)).astype(BF16)
    o_ref[...] = _dot(s, w_ref[...].astype(BF16)) + b_ref[...]


def _mod_call(c, w_ada, b_ada):
    B, D = c.shape
    rows = 8
    c8 = jnp.zeros((rows, D), F32).at[:B].set(c)
    out = pl.pallas_call(
        _mod_body,
        grid=(N_MOD,),
        in_specs=[pl.BlockSpec((rows, D), lambda j: (0, 0)),
                  pl.BlockSpec((D, D), lambda j: (0, j)),
                  pl.BlockSpec((1, D), lambda j: (0, j))],
        out_specs=pl.BlockSpec((rows, D), lambda j: (0, j)),
        out_shape=jax.ShapeDtypeStruct((rows, N_MOD * D), F32),
        compiler_params=_cparams(("parallel",)),
        name="mod",
    )(c8, w_ada, b_ada.reshape(1, N_MOD * D))
    return out[:B].reshape(B, N_MOD, 1, D)


def _ffn_body(x_ref, sh_ref, sc_ref, g_ref, wg_ref, wu_ref, wd_ref, lng_ref, lnb_ref, o_ref,
              u_scr, acc_scr, *, alpha, coef, n_ff):
    j = pl.program_id(2)

    @pl.when(j == 0)
    def _():
        u_scr[...] = (x_ref[...] * (1.0 + sc_ref[...]) + sh_ref[...]).astype(BF16)
        acc_scr[...] = jnp.zeros_like(acc_scr)

    u = u_scr[...]
    a = _dot(u, wg_ref[...].astype(BF16))
    b = _dot(u, wu_ref[...].astype(BF16))
    h = (a * jax.nn.sigmoid(a)) * b
    acc_scr[...] += _dot(h.astype(BF16), wd_ref[...].astype(BF16))

    @pl.when(j == n_ff - 1)
    def _():
        z = alpha * x_ref[...] + coef * (1.0 + g_ref[...]) * acc_scr[...]
        o_ref[...] = _layer_norm(z, lng_ref[...], lnb_ref[...])


def _ffn_call(x, sh, sc, g, wg, wu, wd, lng, lnb, alpha, coef):
    B, T, D = x.shape
    FF = wg.shape[1]
    tm = min(T, 1024)
    tf = 256
    assert T % tm == 0 and FF % tf == 0
    n_ff = FF // tf
    row = pl.BlockSpec((None, 1, D), lambda b, i, j: (b, 0, 0))
    vec = pl.BlockSpec((1, D), lambda b, i, j: (0, 0))
    return pl.pallas_call(
        functools.partial(_ffn_body, alpha=alpha, coef=coef, n_ff=n_ff),
        grid=(B, T // tm, n_ff),
        in_specs=[pl.BlockSpec((None, tm, D), lambda b, i, j: (b, i, 0)), row, row, row,
                  pl.BlockSpec((D, tf), lambda b, i, j: (0, j)),
                  pl.BlockSpec((D, tf), lambda b, i, j: (0, j)),
                  pl.BlockSpec((tf, D), lambda b, i, j: (j, 0)), vec, vec],
        out_specs=pl.BlockSpec((None, tm, D), lambda b, i, j: (b, i, 0)),
        out_shape=jax.ShapeDtypeStruct((B, T, D), F32),
        scratch_shapes=[pltpu.VMEM((tm, D), BF16), pltpu.VMEM((tm, D), F32)],
        compiler_params=_cparams(("parallel", "parallel", "arbitrary")),
        name="ffn",
    )(x, sh, sc, g, wg, wu, wd, lng.reshape(1, D), lnb.reshape(1, D))


def _rope_slot(x, c, sa, sb, half):
    return x * c + pltpu.roll(x, LANES - half, 1) * sa + pltpu.roll(x, half, 1) * sb


def _rms_norm(x, g):
    return x * lax.rsqrt(jnp.mean(x * x, -1, keepdims=True) + EPS) * g


def _proj_body(x_ref, sh_ref, sc_ref, w_ref, gq_ref, gkv_ref, wuq_ref, wuk_ref, wuv_ref,
               base_ref, tile_ref, sgn_ref,
               qn_ref, kvc_ref, kvs_ref, kvw_ref, gate_ref, qm_ref, km_ref, vm_ref,
               *, nsa_scale, mla_scale):
    tm = x_ref.shape[0]
    rc = min(tm, PROJ_ROWS)
    starts = range(0, tm, rc)
    nh = NSA_ROPE_DIM // 2
    mh = MLA_ROPE_DIM // 2
    hs = [_dot((x_ref[r:r + rc, :] * (1.0 + sc_ref[...]) + sh_ref[...]).astype(BF16), w_ref[...])
          for r in starts]
    for c, r in enumerate(starts):
        h, rows = hs[c], slice(r, r + rc)
        tabs = []
        for st in range(len(ROPE_SETS)):
            lc, ls = slice(2 * st * LANES, (2 * st + 1) * LANES), slice((2 * st + 1) * LANES, (2 * st + 2) * LANES)
            cb, sb_, ca, sa_ = base_ref[rows, lc], base_ref[rows, ls], tile_ref[:, lc], tile_ref[:, ls]
            sin = sa_ * cb + ca * sb_
            tabs += [ca * cb - sa_ * sb_, sin * sgn_ref[0:1, lc], sin * sgn_ref[0:1, ls]]
        cq, sqa, sqb, cn, sna, snb, cm, sma, smb = tabs

        for s in range(NSA_HEADS * NSA_HEAD_DIM // LANES):
            lo = OFF_Q + s * LANES
            qn_ref[rows, s * LANES:(s + 1) * LANES] = (
                _rope_slot(h[:, lo:lo + LANES], cq, sqa, sqb, nh) * nsa_scale).astype(BF16)
        for idx, ref in enumerate((kvc_ref, kvs_ref, kvw_ref)):
            for g in range(NSA_KV_GROUPS):
                lo = OFF_KV + (idx * NSA_KV_GROUPS + g) * LANES
                ref[g, rows, :] = _rope_slot(h[:, lo:lo + LANES], cn, sna, snb, nh).astype(BF16)
        gate_ref[rows, :] = jax.nn.sigmoid(h[:, OFF_GATE:OFF_GATE + NSA_KV_GROUPS * LANES])

        cqn = _rms_norm(h[:, OFF_CQ:OFF_CQ + MLA_Q_RANK], gq_ref[...]).astype(BF16)
        qm = _dot(cqn, wuq_ref[...])
        ckvn = _rms_norm(h[:, OFF_CKV:OFF_CKV + MLA_KV_RANK], gkv_ref[...]).astype(BF16)
        kn = _dot(ckvn, wuk_ref[...])
        vm_ref[rows, :] = _dot(ckvn, wuv_ref[...]).astype(BF16)
        kr = _rope_slot(h[:, OFF_KR:OFF_KR + LANES], cm, sma, smb, mh)
        for s in range(MLA_HEADS):
            sl = slice(s * LANES, (s + 1) * LANES)
            qm_ref[rows, sl] = (_rope_slot(qm[:, sl], cm, sma, smb, mh) * mla_scale).astype(BF16)
            km_ref[rows, sl] = (kn[:, sl] + kr).astype(BF16)


def _proj_call(x, sh, sc, w_all, gq, gkv, wuq, wuk, wuv):
    B, T, D = x.shape
    tm = min(T, 1024)
    G = NSA_KV_GROUPS
    base, tile_, signs = _rope_tables(T, tm)
    row = pl.BlockSpec((None, 1, D), lambda b, i: (b, 0, 0))

    def full(a):
        return pl.BlockSpec(a.shape, lambda b, i: (0,) * a.ndim)

    tile_spec = pl.BlockSpec((None, 1, tile_.shape[-1]), lambda b, i: (i, 0, 0))
    tok = lambda w: pl.BlockSpec((None, tm, w), lambda b, i: (b, i, 0))
    grp = pl.BlockSpec((G, None, tm, LANES), lambda b, i: (0, b, i, 0))
    sd = jax.ShapeDtypeStruct
    return pl.pallas_call(
        functools.partial(_proj_body, nsa_scale=NSA_HEAD_DIM ** -0.5 * LOG2E,
                          mla_scale=(MLA_NOPE_DIM + MLA_ROPE_DIM) ** -0.5 * LOG2E),
        grid=(B, T // tm),
        in_specs=[tok(D), row, row, full(w_all), full(gq), full(gkv), full(wuq), full(wuk), full(wuv)]
                 + [full(base), tile_spec, full(signs)],
        out_specs=[tok(NSA_HEADS * NSA_HEAD_DIM), grp, grp, grp, tok(G * LANES),
                   tok(MLA_HEADS * LANES), tok(MLA_HEADS * LANES), tok(MLA_HEADS * MLA_V_DIM)],
        out_shape=[sd((B, T, NSA_HEADS * NSA_HEAD_DIM), BF16), sd((G, B, T, LANES), BF16),
                   sd((G, B, T, LANES), BF16), sd((G, B, T, LANES), BF16), sd((B, T, G * LANES), F32),
                   sd((B, T, MLA_HEADS * LANES), BF16), sd((B, T, MLA_HEADS * LANES), BF16),
                   sd((B, T, MLA_HEADS * MLA_V_DIM), BF16)],
        compiler_params=_cparams(("parallel", "parallel")),
        name="proj",
    )(x, sh, sc, w_all, gq, gkv, wuq, wuk, wuv, base, tile_, signs)


def _gelu_tanh(x):
    return x * (0.5 * (1.0 + jnp.tanh(0.7978845608028654 * (x + 0.044715 * (x * x * x)))))


def _cmp_body(c_ref, w1a_ref, w1b_ref, pa_ref, pb_ref, w2_ref, o_ref):
    c = c_ref[...]
    nc = c.shape[0]
    w1a, w1b = w1a_ref[...], w1b_ref[...]
    a = _dot(c, w1a)
    b = _dot(c, w1b)
    bias = _dot(pa_ref[...], w1a) + _dot(pb_ref[...], w1b)
    pre = a + pltpu.roll(b, nc - 1, 0) + bias[0:1]
    o_ref[...] = _dot(_gelu_tanh(pre).astype(BF16), w2_ref[...]).astype(BF16)


def _cmp_call(kvc, w1a, w1b, pa, pb, w2):
    G, B, T, _ = kvc.shape
    nc = T // CMP_STRIDE
    c = kvc.reshape(G, B, nc, CMP_STRIDE * LANES)

    def full(a):
        return pl.BlockSpec(a.shape, lambda g, b: (0,) * a.ndim)

    return pl.pallas_call(
        _cmp_body,
        grid=(G, B),
        in_specs=[pl.BlockSpec((None, None, nc, CMP_STRIDE * LANES), lambda g, b: (g, b, 0, 0)),
                  full(w1a), full(w1b), full(pa), full(pb), full(w2)],
        out_specs=pl.BlockSpec((None, None, nc, LANES), lambda g, b: (g, b, 0, 0)),
        out_shape=jax.ShapeDtypeStruct((G, B, nc, LANES), BF16),
        compiler_params=_cparams(("parallel", "parallel")),
        name="cmp",
    )(c, w1a, w1b, pa, pb, w2)


def _nsa_body(q_ref, gate_ref, kvc_ref, kvs_ref, kvw_ref, ov_ref, et_ref, band_ref, o_ref,
              m_scr, l_scr, acc_scr, part_scr, *, n_slc):
    i = pl.program_id(2)
    nc = kvc_ref.shape[0]
    hpg, tq_n, rows, tk = NSA_HPG, NSA_TQ, NSA_HPG * NSA_TQ, NSA_TK
    t0 = pl.multiple_of(i * tq_n, tq_n)
    q = q_ref[...].astype(F32)
    low = _iota((tq_n, LANES), 1) < NSA_HEAD_DIM
    heads = []
    for h in range(hpg):
        tile_ = q[:, (h // 2) * LANES:(h // 2 + 1) * LANES]
        heads.append(jnp.where(low, tile_ if h % 2 == 0 else pltpu.roll(tile_, NSA_HEAD_DIM, 1), 0.0))
    qs = jnp.concatenate(heads, axis=0).astype(BF16)

    wk = WINDOW + tq_n
    start = pl.multiple_of(jnp.maximum(t0 - WINDOW, 0), tq_n)
    kvw = kvw_ref[pl.ds(start, wk), :]
    s_w = _dot_t(qs, kvw)
    s_w = s_w + _rep_rows(band_ref[...], hpg)
    e_w = jnp.exp2(s_w - jnp.max(s_w, -1, keepdims=True))
    o_win = _dot(e_w.astype(BF16), kvw) / jnp.sum(e_w, -1, keepdims=True)

    kvc = kvc_ref[...]
    s = _dot_t(qs, kvc)
    tq = t0 + _iota((tq_n, nc), 0)
    n = _iota((tq_n, nc), 1)
    valid = (n * CMP_STRIDE + (CMP_BLOCK - 1) <= tq) & (n < nc - 1)
    s = s + _rep_rows(jnp.where(valid, 0.0, NEG), hpg)
    e = jnp.exp2(s - jnp.maximum(jnp.max(s, -1, keepdims=True), M_INIT))
    p = e / jnp.maximum(jnp.sum(e, -1, keepdims=True), 1e-30)
    o_cmp = _dot(p.astype(BF16), kvc)

    ps = p[0:tq_n] + p[tq_n:2 * tq_n] + p[2 * tq_n:3 * tq_n] + p[3 * tq_n:4 * tq_n]
    ph = ps.astype(BF16)
    plo = (ps - ph.astype(F32)).astype(BF16)
    ov = ov_ref[...]
    imp_t = (_dot(ph, ov) + _dot(plo, ov)).T

    jr = _iota((LANES, tq_n), 0)
    tl = t0 + _iota((LANES, tq_n), 1)
    cur = lax.shift_right_logical(tl, SLC_SHIFT)
    causal = jr * SLC_BLOCK <= tl
    forced = ((jr == 0) | (jr == cur) | (jr == cur - 1)) & causal
    v = jnp.where(forced, PICKED, jnp.where(causal, imp_t, -1.0))
    if n_slc < LANES:
        v = jnp.where(jr < n_slc, v, PICKED)
    jf = jr.astype(F32)
    sel_t = jnp.where(forced, 1.0, 0.0)
    n_forced = 3
    for _ in range(SLC_TOPK - n_forced):
        mx = jnp.max(v, axis=0, keepdims=True)
        idx = jnp.min(jnp.where(v == mx, jf, float(LANES)), axis=0, keepdims=True)
        hit = jf == idx
        sel_t = jnp.where(hit, 1.0, sel_t)
        v = jnp.where(hit, PICKED, v)
    sel = sel_t.T

    gt = gate_ref[...]
    gcol = lambda br: jnp.concatenate([gt[:, h * 3 + br:h * 3 + br + 1] for h in range(hpg)], axis=0)
    part_scr[...] = gcol(0) * o_cmp + gcol(2) * o_win

    upto = _iota((tq_n, LANES), 1) < lax.shift_right_logical(t0 + tq_n, SLC_SHIFT)
    lhs = jnp.concatenate([qs, _rep_rows(jnp.where((sel > 0.5) & upto, 0.0, NEG).astype(BF16), hpg)], axis=1)
    tk_shift = tk.bit_length() - 1
    n_full = lax.shift_right_logical(t0 + tq_n - 1, tk_shift)
    tail_idx = lax.shift_right_logical(t0 + tq_n - 1 - n_full * tk, tq_n.bit_length() - 1)

    m_scr[...] = jnp.full(m_scr.shape, M_INIT, F32)
    l_scr[...] = jnp.zeros(l_scr.shape, F32)
    acc_scr[...] = jnp.zeros(acc_scr.shape, F32)
    rc = ROW_CHUNK

    def tile(k, width, last):
        k0 = pl.multiple_of(k * tk, tk)
        kv = kvs_ref[pl.ds(k0, width), :]
        rhs = jnp.concatenate([kv, et_ref[pl.ds(k0, width), :]], axis=1)
        s_heads = [_dot_t(lhs[h * tq_n:(h + 1) * tq_n], rhs) for h in range(hpg)]
        for h in range(hpg):
            hs = slice(h * tq_n, (h + 1) * tq_n)
            ps, als = [], []
            for r in range(0, tq_n, rc):
                sl = slice(h * tq_n + r, h * tq_n + r + rc)
                sc = s_heads[h][r:r + rc]
                if last:
                    own = jnp.where(_iota((rc, tq_n), 1) <= r + _iota((rc, tq_n), 0), 0.0, NEG)
                    sc = sc + own if width == tq_n else jnp.concatenate(
                        [sc[:, :width - tq_n], sc[:, width - tq_n:] + own], axis=1)
                m_prev = m_scr[sl]
                m_new = jnp.maximum(m_prev, jnp.max(sc, -1, keepdims=True))
                alpha = jnp.exp2(m_prev - m_new)
                pp = jnp.exp2(sc - _rep_lanes(m_new, width // LANES))
                l_scr[sl] = alpha * l_scr[sl] + jnp.sum(pp, -1, keepdims=True)
                m_scr[sl] = m_new
                ps.append(pp.astype(BF16))
                als.append(alpha)
            acc_scr[hs] = jnp.concatenate(als, axis=0) * acc_scr[hs] + _dot(jnp.concatenate(ps, axis=0), kv)

    def body(k, carry):
        tile(k, tk, False)
        return carry

    lax.fori_loop(0, n_full, body, 0)
    for j in range(tk // tq_n):
        pl.when(tail_idx == j)(functools.partial(tile, n_full, (j + 1) * tq_n, True))
    acc = acc_scr[...]
    out = part_scr[...] + gcol(1) * (acc / l_scr[...])
    out = jnp.where(_iota((rows, LANES), 1) >= NSA_HEAD_DIM, out, 0.0)
    for hp in range(hpg // 2):
        even = out[2 * hp * tq_n:(2 * hp + 1) * tq_n]
        odd = out[(2 * hp + 1) * tq_n:(2 * hp + 2) * tq_n]
        o_ref[:, hp * LANES:(hp + 1) * LANES] = (pltpu.roll(even, NSA_HEAD_DIM, 1) + odd).astype(BF16)


def _nsa_call(qn, gates, kvcmp, kvs, kvw, ov, et, band):
    B, T, _ = qn.shape
    G = NSA_KV_GROUPS
    nc = kvcmp.shape[2]
    n_slc = T // SLC_BLOCK
    assert SLC_TOPK <= n_slc <= LANES and nc % LANES == 0 and T % NSA_TK == 0 and T >= WINDOW + NSA_TQ
    assert NSA_TK & (NSA_TK - 1) == 0 and NSA_TQ % SLC_BLOCK == 0 and NSA_TK % NSA_TQ == 0
    assert WINDOW % NSA_TQ == 0 and band.shape == (WINDOW // NSA_TQ + 1, NSA_TQ, WINDOW + NSA_TQ)
    gw = NSA_HPG * NSA_HEAD_DIM
    rows = NSA_HPG * NSA_TQ
    whole = lambda r: pl.BlockSpec((None, None, r, LANES), lambda b, g, i: (g, b, 0, 0))
    const = lambda a: pl.BlockSpec(a.shape, lambda b, g, i: (0, 0))
    return pl.pallas_call(
        functools.partial(_nsa_body, n_slc=n_slc),
        grid=(B, G, T // NSA_TQ),
        in_specs=[pl.BlockSpec((None, NSA_TQ, gw), lambda b, g, i: (b, i, g)),
                  pl.BlockSpec((None, NSA_TQ, LANES), lambda b, g, i: (b, i, g)),
                  whole(nc), whole(T), whole(T), const(ov), const(et),
                  pl.BlockSpec((None,) + band.shape[1:], lambda b, g, i: (jnp.minimum(i, band.shape[0] - 1), 0, 0))],
        out_specs=pl.BlockSpec((None, NSA_TQ, NSA_HPG * NSA_HEAD_DIM), lambda b, g, i: (b, i, g)),
        out_shape=jax.ShapeDtypeStruct((B, T, NSA_HEADS * NSA_HEAD_DIM), BF16),
        scratch_shapes=[pltpu.VMEM((rows, LANES), F32)] * 4,
        compiler_params=_cparams(("parallel", "parallel", "arbitrary")),
        name="nsa",
    )(qn, gates, kvcmp, kvs, kvw, ov, et, band)


def _mla_body(q_ref, k_ref, v_ref, o_ref, p_buf, a_buf, m_scr, l_scr, acc_scr):
    i = pl.program_id(2)
    tt, nh = MLA_T, MLA_HPS
    head = lambda hh: slice(hh * LANES, (hh + 1) * LANES)

    pair = lambda hh: slice((hh // 2) * LANES, (hh // 2 + 1) * LANES)

    def scores(kt, hh):
        k0 = pl.multiple_of(kt * tt, tt)
        return _dot_t(q_ref[:, head(hh)], k_ref[pl.ds(k0, tt), head(hh)])

    def values(kt, hh):
        return v_ref[pl.ds(pl.multiple_of(kt * tt, tt), tt), pair(hh)]

    m_scr[...] = jnp.full(m_scr.shape, M_INIT, F32)
    l_scr[...] = jnp.zeros(l_scr.shape, F32)
    acc_scr[...] = jnp.zeros(acc_scr.shape, F32)
    p_buf[...] = jnp.zeros(p_buf.shape, BF16)
    a_buf[...] = jnp.ones(a_buf.shape, F32)

    def stage(k, diagonal):
        kp = jnp.maximum(k - 1, 0)
        if diagonal:
            tri = jnp.where(_iota((tt, tt), 1) <= _iota((tt, tt), 0), 0.0, NEG)
        for hh in range(nh):
            s = scores(k, hh)
            if diagonal:
                s = s + tri
            acc_scr[hh] = a_buf[hh] * acc_scr[hh] + _dot(p_buf[hh], values(kp, hh))
            m_prev = m_scr[hh]
            m_new = jnp.maximum(m_prev, jnp.max(s, -1, keepdims=True))
            alpha = jnp.exp2(m_prev - m_new)
            p = jnp.exp2(s - _rep_lanes(m_new, tt // LANES))
            l_scr[hh] = alpha * l_scr[hh] + jnp.sum(p, -1, keepdims=True)
            m_scr[hh] = m_new
            a_buf[hh] = alpha
            p_buf[hh] = p.astype(BF16)

    def body(k, carry):
        stage(k, False)
        return carry

    lax.fori_loop(0, i, body, 0)
    stage(i, True)
    outs = [(a_buf[hh] * acc_scr[hh] + _dot(p_buf[hh], values(i, hh))) / l_scr[hh] for hh in range(nh)]
    first = _iota((tt, LANES), 1) < MLA_V_DIM
    for pp in range(nh // 2):
        o_ref[:, pp * LANES:(pp + 1) * LANES] = jnp.where(first, outs[2 * pp], outs[2 * pp + 1]).astype(BF16)


def _mla_call(qm, km, vm):
    B, T, _ = qm.shape
    nh = MLA_HPS
    assert T % MLA_T == 0 and MLA_HEADS % nh == 0 and nh % 2 == 0
    vw = nh * MLA_V_DIM
    return pl.pallas_call(
        _mla_body,
        grid=(B, MLA_HEADS // nh, T // MLA_T),
        in_specs=[pl.BlockSpec((None, MLA_T, nh * LANES), lambda b, p, i: (b, i, p)),
                  pl.BlockSpec((None, T, nh * LANES), lambda b, p, i: (b, 0, p)),
                  pl.BlockSpec((None, T, vw), lambda b, p, i: (b, 0, p))],
        out_specs=pl.BlockSpec((None, MLA_T, vw), lambda b, p, i: (b, i, p)),
        out_shape=jax.ShapeDtypeStruct((B, T, MLA_HEADS * MLA_V_DIM), BF16),
        scratch_shapes=[pltpu.VMEM((nh, MLA_T, MLA_T), BF16)] + [pltpu.VMEM((nh, MLA_T, LANES), F32)] * 4,
        compiler_params=_cparams(("parallel", "parallel", "arbitrary"), MLA_VMEM_LIMIT),
        name="mla",
    )(qm, km, vm)


def _out_body(x_ref, on_ref, om_ref, g_ref, wn_ref, wm_ref, lng_ref, lnb_ref, o_ref, *, alpha):
    tm = x_ref.shape[0]
    rc = min(tm, OUT_ROWS)
    ys = [_dot(on_ref[r:r + rc, :], wn_ref[...]) + _dot(om_ref[r:r + rc, :], wm_ref[...]) for r in range(0, tm, rc)]
    for c, r in enumerate(range(0, tm, rc)):
        z = alpha * x_ref[r:r + rc, :] + (1.0 + g_ref[...]) * ys[c]
        o_ref[r:r + rc, :] = _layer_norm(z, lng_ref[...], lnb_ref[...])


def _out_call(x, o_nsa, o_mla, g, w_on, w_om, lng, lnb, alpha):
    B, T, D = x.shape
    tm = min(T, 1024)
    tok = lambda w: pl.BlockSpec((None, tm, w), lambda b, i: (b, i, 0))
    full = lambda a: pl.BlockSpec(a.shape, lambda b, i: (0,) * a.ndim)
    lng, lnb = lng.reshape(1, D), lnb.reshape(1, D)
    return pl.pallas_call(
        functools.partial(_out_body, alpha=alpha),
        grid=(B, T // tm),
        in_specs=[tok(D), tok(o_nsa.shape[-1]), tok(o_mla.shape[-1]),
                  pl.BlockSpec((None, 1, D), lambda b, i: (b, 0, 0)),
                  full(w_on), full(w_om), full(lng), full(lnb)],
        out_specs=tok(D),
        out_shape=jax.ShapeDtypeStruct((B, T, D), F32),
        compiler_params=_cparams(("parallel", "parallel")),
        name="out",
    )(x, o_nsa, o_mla, g, w_on, w_om, lng, lnb)


ROPE_SETS = (((0, NSA_HEAD_DIM), NSA_ROPE_DIM),
             ((0,), NSA_ROPE_DIM),
             ((MLA_NOPE_DIM,), MLA_ROPE_DIM))
def _rope_tables(T, tm):
    inv_rows, sign_rows = [], []
    for lane0s, dim in ROPE_SETS:
        half = dim // 2
        inv = ROPE_THETA ** (-jnp.arange(half, dtype=F32) / half)
        lane_inv = jnp.zeros((LANES,), F32)
        first, second = np.zeros((LANES,), np.float32), np.zeros((LANES,), np.float32)
        for lane0 in lane0s:
            lane_inv = lane_inv.at[lane0:lane0 + half].set(inv).at[lane0 + half:lane0 + dim].set(inv)
            first[lane0:lane0 + half] = -1.0
            second[lane0 + half:lane0 + dim] = 1.0
        inv_rows.append(lane_inv)
        sign_rows += [first, second]
    inv_all = jnp.stack(inv_rows)

    def trig(pos):
        ang = pos.astype(F32)[:, None, None] * inv_all[None]
        return jnp.stack([jnp.cos(ang), jnp.sin(ang)], axis=2).reshape(pos.shape[0], -1)

    signs = jnp.broadcast_to(jnp.asarray(np.concatenate(sign_rows))[None, :], (8, 2 * len(ROPE_SETS) * LANES))
    return trig(jnp.arange(tm)), trig(jnp.arange(0, T, tm))[:, None, :], signs


def _pad_slots(w, n, width):
    K = w.shape[0]
    return jnp.pad(w.reshape(K, n, width), ((0, 0), (0, 0), (0, LANES - width))).reshape(K, n * LANES)


def _proj_weights(w_in):
    D = w_in.shape[0]
    hd, G = NSA_HEAD_DIM, NSA_KV_GROUPS
    nq, nkv = NSA_HEADS * hd, G * hd
    off = np.cumsum([0, nq] + [nkv] * 6 + [3 * NSA_HEADS, MLA_Q_RANK, MLA_KV_RANK, MLA_ROPE_DIM])
    cols = [w_in[:, :nq]]
    for pair in range(3):
        ko, vo = off[1 + 2 * pair], off[2 + 2 * pair]
        for g in range(G):
            cols += [w_in[:, ko + g * hd:ko + (g + 1) * hd], w_in[:, vo + g * hd:vo + (g + 1) * hd]]
    per_g = 3 * NSA_HPG
    for g in range(G):
        cols.append(jnp.pad(w_in[:, off[7] + g * per_g:off[7] + (g + 1) * per_g], ((0, 0), (0, LANES - per_g))))
    cols += [w_in[:, off[8]:off[9]], w_in[:, off[9]:off[10]]]
    cols.append(jnp.pad(w_in[:, off[10]:off[11]],
                        ((0, 0), (MLA_NOPE_DIM, LANES - MLA_NOPE_DIM - MLA_ROPE_DIM))))
    w_all = jnp.concatenate(cols, axis=1).astype(BF16)
    assert w_all.shape == (D, PROJ_W)
    return w_all


def _cmp_weights(k_pos, k_w1, k_w2, v_pos, v_w1, v_w2):
    hd, st, hid = NSA_HEAD_DIM, CMP_STRIDE, CMP_HIDDEN

    def half(lo):
        w = jnp.zeros((st, LANES, 2 * hid), F32)
        w = w.at[:, :hd, :hid].set(k_w1[lo * hd:(lo + st) * hd].reshape(st, hd, hid))
        w = w.at[:, hd:, hid:].set(v_w1[lo * hd:(lo + st) * hd].reshape(st, hd, hid))
        pos = jnp.concatenate([k_pos[lo:lo + st], v_pos[lo:lo + st]], axis=1).reshape(1, st * LANES)
        return w.reshape(st * LANES, 2 * hid).astype(BF16), jnp.broadcast_to(pos, (8, st * LANES)).astype(BF16)

    w1a, pa = half(0)
    w1b, pb = half(st)
    w2 = jnp.zeros((2 * hid, LANES), F32).at[:hid, :hd].set(k_w2).at[hid:, hd:].set(v_w2).astype(BF16)
    return w1a, w1b, pa, pb, w2


def _block_tables(T):
    nc, n_slc = T // CMP_STRIDE, T // SLC_BLOCK
    cs = np.arange(nc)[:, None] * CMP_STRIDE
    blk = np.arange(LANES)[None, :]
    ov = (cs < blk * SLC_BLOCK + SLC_BLOCK) & (cs + CMP_BLOCK > blk * SLC_BLOCK) & (np.arange(nc)[:, None] < nc - 1) & (blk < n_slc)
    et = (np.arange(T)[:, None] // SLC_BLOCK) == blk
    off = np.arange(WINDOW // NSA_TQ + 1)[:, None, None] * NSA_TQ
    diff = off + np.arange(NSA_TQ)[None, :, None] - np.arange(WINDOW + NSA_TQ)[None, None, :]
    band = np.where((diff >= 0) & (diff < WINDOW), 0.0, NEG).astype(np.float32)
    return jnp.asarray(ov.astype(np.float32), BF16), jnp.asarray(et.astype(np.float32), BF16), jnp.asarray(band)


def kernel(x, c, w_ada, b_ada, ln_g, ln_b, ffn1_wg, ffn1_wu, ffn1_wd, w_in, cmp_k_pos, cmp_k_w1, cmp_k_w2,
           cmp_v_pos, cmp_v_w1, cmp_v_w2, mla_q_norm, mla_kv_norm, mla_w_uq, mla_w_ukv, w_out,
           ffn2_wg, ffn2_wu, ffn2_wd):
    B, T, D = x.shape
    depth = w_ada.shape[0]
    alpha = (2.0 * depth) ** 0.25
    ov, et, band = _block_tables(T)
    for l in range(depth):
        mod = _mod_call(c, w_ada[l], b_ada[l])
        sh1, sc1, g1, sh2, sc2, g2, sh3, sc3, g3 = [mod[:, k] for k in range(N_MOD)]
        x = _ffn_call(x, sh1, sc1, g1, ffn1_wg[l], ffn1_wu[l], ffn1_wd[l],
                      ln_g[l, 0], ln_b[l, 0], alpha, 0.5)

        wuq = _pad_slots(mla_w_uq[l], MLA_HEADS, MLA_NOPE_DIM + MLA_ROPE_DIM).astype(BF16)
        wukv = mla_w_ukv[l].reshape(MLA_KV_RANK, MLA_HEADS, MLA_NOPE_DIM + MLA_V_DIM)
        wuk = _pad_slots(wukv[:, :, :MLA_NOPE_DIM].reshape(MLA_KV_RANK, -1), MLA_HEADS, MLA_NOPE_DIM).astype(BF16)
        wuv = wukv[:, :, MLA_NOPE_DIM:].reshape(MLA_KV_RANK, -1).astype(BF16)
        qn, kvc, kvs, kvw, gates, qm, km, vm = _proj_call(
            x, sh2, sc2, _proj_weights(w_in[l]), mla_q_norm[l].reshape(1, -1), mla_kv_norm[l].reshape(1, -1),
            wuq, wuk, wuv)
        kvcmp = _cmp_call(kvc, *_cmp_weights(cmp_k_pos[l], cmp_k_w1[l], cmp_k_w2[l],
                                             cmp_v_pos[l], cmp_v_w1[l], cmp_v_w2[l]))
        o_nsa = _nsa_call(qn, gates, kvcmp, kvs, kvw, ov, et, band)
        o_mla = _mla_call(qm, km, vm)
        n_nsa = NSA_HEADS * NSA_HEAD_DIM
        w_o = w_out[l].astype(BF16)
        x = _out_call(x, o_nsa, o_mla, g2, w_o[:n_nsa], w_o[n_nsa:], ln_g[l, 1], ln_b[l, 1], alpha)

        x = _ffn_call(x, sh3, sc3, g3, ffn2_wg[l], ffn2_wu[l], ffn2_wd[l],
                      ln_g[l, 2], ln_b[l, 2], alpha, 0.5)
    return x
```

```python
import functools

import numpy as np
import jax
import jax.numpy as jnp
from jax import lax
from jax.experimental import pallas as pl
from jax.experimental.pallas import tpu as pltpu

F32 = jnp.float32
BF16 = jnp.bfloat16

ROPE_THETA = 500000.0
NSA_HEADS = 8
NSA_KV_GROUPS = 2
NSA_HPG = NSA_HEADS // NSA_KV_GROUPS
NSA_HEAD_DIM = 64
NSA_ROPE_DIM = 16
CMP_BLOCK = 32
CMP_STRIDE = 16
CMP_HIDDEN = 256
SLC_BLOCK = 64
SLC_TOPK = 16
SLC_SHIFT = SLC_BLOCK.bit_length() - 1
WINDOW = 512
FORCE_SCORE = 1e9
MLA_HEADS = 8
MLA_NOPE_DIM = 64
MLA_ROPE_DIM = 32
MLA_V_DIM = 64
MLA_Q_RANK = 384
MLA_KV_RANK = 256
EPS = 1e-5
N_MOD = 9

LANES = 128
NEG = -1e30
M_INIT = -1e29
PICKED = -3e38
ROW_CHUNK = 128
LOG2E = 1.4426950408889634
VMEM_LIMIT = 48 * 1024 * 1024
MLA_VMEM_LIMIT = 56 * 1024 * 1024

NSA_TQ = 256
NSA_TK = 1024
MLA_T = 512
MLA_HPS = 4
OUT_ROWS = 256
PROJ_ROWS = 256

OFF_Q, OFF_KV, OFF_GATE, OFF_CQ, OFF_CKV, OFF_KR, PROJ_W = 0, 512, 1280, 1536, 1920, 2176, 2304


def _cparams(sem, vmem=VMEM_LIMIT):
    return pltpu.CompilerParams(dimension_semantics=sem, vmem_limit_bytes=vmem)


def _layer_norm(z, g, b):
    mu = jnp.mean(z, -1, keepdims=True)
    zc = z - mu
    var = jnp.mean(zc * zc, -1, keepdims=True)
    return zc * lax.rsqrt(var + EPS) * g + b


def _dot(a, b):
    return jnp.dot(a, b, preferred_element_type=F32)


def _dot_t(a, b):
    return lax.dot_general(a, b, (((1,), (1,)), ((), ())), preferred_element_type=F32)


def _rep_rows(a, n):
    return jnp.concatenate([a] * n, axis=0)


def _rep_lanes(a, n):
    return jnp.concatenate([a] * n, axis=1)


def _iota(shape, d):
    return lax.broadcasted_iota(jnp.int32, shape, d)


def _mod_body(c_ref, w_ref, b_ref, o_ref):
    c = c_ref[...]
    s = (c * jax.nn.sigmoid(c)).astype(BF16)
    o_ref[...] = _dot(s, w_ref[...].astype(BF16)) + b_ref[...]


def _mod_call(c, w_ada, b_ada):
    B, D = c.shape
    rows = 8
    c8 = jnp.zeros((rows, D), F32).at[:B].set(c)
    out = pl.pallas_call(
        _mod_body,
        grid=(N_MOD,),
        in_specs=[pl.BlockSpec((rows, D), lambda j: (0, 0)),
                  pl.BlockSpec((D, D), lambda j: (0, j)),
                  pl.BlockSpec((1, D), lambda j: (0, j))],
        out_specs=pl.BlockSpec((rows, D), lambda j: (0, j)),
        out_shape=jax.ShapeDtypeStruct((rows, N_MOD * D), F32),
        compiler_params=_cparams(("parallel",)),
        name="mod",
    )(c8, w_ada, b_ada.reshape(1, N_MOD * D))
    return out[:B].reshape(B, N_MOD, 1, D)


def _ffn_body(x_ref, sh_ref, sc_ref, g_ref, wg_ref, wu_ref, wd_ref, lng_ref, lnb_ref, o_ref,
              acc_scr, z_scr, *, alpha, coef, n_ff, n_tiles, ln_rows):
    s, j = pl.program_id(0), pl.program_id(1)
    tm = z_scr.shape[0]

    @pl.when((s == 0) & (j == 0))
    def _():
        z_scr[...] = jnp.zeros_like(z_scr)

    def norm_chunk():
        r0 = pl.multiple_of(jnp.minimum(j * ln_rows, tm - ln_rows), 8)
        o_ref[pl.ds(r0, ln_rows), :] = _layer_norm(z_scr[pl.ds(r0, ln_rows), :], lng_ref[...], lnb_ref[...])

    @pl.when(s < n_tiles)
    def _():
        @pl.when(j == 0)
        def _():
            acc_scr[...] = jnp.zeros_like(acc_scr)

        u = (x_ref[...] * (1.0 + sc_ref[...]) + sh_ref[...]).astype(BF16)
        a = _dot(u, wg_ref[...].astype(BF16))
        b = _dot(u, wu_ref[...].astype(BF16))
        h = (a * jax.nn.sigmoid(a)) * b
        acc_scr[...] += _dot(h.astype(BF16), wd_ref[...].astype(BF16))
        norm_chunk()

        @pl.when(j == n_ff - 1)
        def _():
            z_scr[...] = alpha * x_ref[...] + coef * (1.0 + g_ref[...]) * acc_scr[...]

    @pl.when(s == n_tiles)
    def _():
        norm_chunk()


def _ffn_call(x, sh, sc, g, wg, wu, wd, lng, lnb, alpha, coef):
    B, T, D = x.shape
    FF = wg.shape[1]
    tm = min(T, 1024)
    tf = 256
    assert T % tm == 0 and FF % tf == 0
    n_ff, nt = FF // tf, T // tm
    n_tiles = B * nt
    ln_rows = ((tm + n_ff - 1) // n_ff + 7) // 8 * 8
    assert ln_rows * n_ff >= tm and ln_rows <= tm
    cur = lambda s: jnp.minimum(s, n_tiles - 1)
    prev = lambda s: jnp.maximum(s - 1, 0)
    wt = lambda s, j: jnp.where(s == n_tiles, n_ff - 1, j)
    row = pl.BlockSpec((None, 1, D), lambda s, j: (cur(s) // nt, 0, 0))
    vec = pl.BlockSpec((1, D), lambda s, j: (0, 0))
    return pl.pallas_call(
        functools.partial(_ffn_body, alpha=alpha, coef=coef, n_ff=n_ff, n_tiles=n_tiles, ln_rows=ln_rows),
        grid=(n_tiles + 1, n_ff),
        in_specs=[pl.BlockSpec((None, tm, D), lambda s, j: (cur(s) // nt, cur(s) % nt, 0)), row, row, row,
                  pl.BlockSpec((D, tf), lambda s, j: (0, wt(s, j))),
                  pl.BlockSpec((D, tf), lambda s, j: (0, wt(s, j))),
                  pl.BlockSpec((tf, D), lambda s, j: (wt(s, j), 0)), vec, vec],
        out_specs=pl.BlockSpec((None, tm, D), lambda s, j: (prev(s) // nt, prev(s) % nt, 0)),
        out_shape=jax.ShapeDtypeStruct((B, T, D), F32),
        scratch_shapes=[pltpu.VMEM((tm, D), F32), pltpu.VMEM((tm, D), F32)],
        compiler_params=_cparams(("arbitrary", "arbitrary")),
        name="ffn",
    )(x, sh, sc, g, wg, wu, wd, lng.reshape(1, D), lnb.reshape(1, D))


def _rope_slot(x, c, sa, sb, half):
    return x * c + pltpu.roll(x, LANES - half, 1) * sa + pltpu.roll(x, half, 1) * sb


def _rms_norm(x, g):
    return x * lax.rsqrt(jnp.mean(x * x, -1, keepdims=True) + EPS) * g


def _proj_body(x_ref, sh_ref, sc_ref, w_ref, gq_ref, gkv_ref, wuq_ref, wuk_ref, wuv_ref,
               base_ref, tile_ref, sgn_ref,
               qn_ref, kvc_ref, kvs_ref, kvw_ref, gate_ref, qm_ref, km_ref, vm_ref,
               *, nsa_scale, mla_scale):
    tm = x_ref.shape[0]
    rc = min(tm, PROJ_ROWS)
    starts = range(0, tm, rc)
    nh = NSA_ROPE_DIM // 2
    mh = MLA_ROPE_DIM // 2
    hs = [_dot((x_ref[r:r + rc, :] * (1.0 + sc_ref[...]) + sh_ref[...]).astype(BF16), w_ref[...])
          for r in starts]
    for c, r in enumerate(starts):
        h, rows = hs[c], slice(r, r + rc)
        tabs = []
        for st in range(len(ROPE_SETS)):
            lc, ls = slice(2 * st * LANES, (2 * st + 1) * LANES), slice((2 * st + 1) * LANES, (2 * st + 2) * LANES)
            cb, sb_, ca, sa_ = base_ref[rows, lc], base_ref[rows, ls], tile_ref[:, lc], tile_ref[:, ls]
            sin = sa_ * cb + ca * sb_
            tabs += [ca * cb - sa_ * sb_, sin * sgn_ref[0:1, lc], sin * sgn_ref[0:1, ls]]
        cq, sqa, sqb, cn, sna, snb, cm, sma, smb = tabs

        for s in range(NSA_HEADS * NSA_HEAD_DIM // LANES):
            lo = OFF_Q + s * LANES
            qn_ref[rows, s * LANES:(s + 1) * LANES] = (
                _rope_slot(h[:, lo:lo + LANES], cq, sqa, sqb, nh) * nsa_scale).astype(BF16)
        for idx, ref in enumerate((kvc_ref, kvs_ref, kvw_ref)):
            for g in range(NSA_KV_GROUPS):
                lo = OFF_KV + (idx * NSA_KV_GROUPS + g) * LANES
                ref[g, rows, :] = _rope_slot(h[:, lo:lo + LANES], cn, sna, snb, nh).astype(BF16)
        gate_ref[rows, :] = jax.nn.sigmoid(h[:, OFF_GATE:OFF_GATE + NSA_KV_GROUPS * LANES])

        cqn = _rms_norm(h[:, OFF_CQ:OFF_CQ + MLA_Q_RANK], gq_ref[...]).astype(BF16)
        qm = _dot(cqn, wuq_ref[...])
        ckvn = _rms_norm(h[:, OFF_CKV:OFF_CKV + MLA_KV_RANK], gkv_ref[...]).astype(BF16)
        kn = _dot(ckvn, wuk_ref[...])
        vm_ref[rows, :] = _dot(ckvn, wuv_ref[...]).astype(BF16)
        kr = _rope_slot(h[:, OFF_KR:OFF_KR + LANES], cm, sma, smb, mh)
        for s in range(MLA_HEADS):
            sl = slice(s * LANES, (s + 1) * LANES)
            qm_ref[rows, sl] = (_rope_slot(qm[:, sl], cm, sma, smb, mh) * mla_scale).astype(BF16)
            km_ref[rows, sl] = (kn[:, sl] + kr).astype(BF16)


def _proj_call(x, sh, sc, w_all, gq, gkv, wuq, wuk, wuv):
    B, T, D = x.shape
    tm = min(T, 1024)
    G = NSA_KV_GROUPS
    base, tile_, signs = _rope_tables(T, tm)
    row = pl.BlockSpec((None, 1, D), lambda b, i: (b, 0, 0))

    def full(a):
        return pl.BlockSpec(a.shape, lambda b, i: (0,) * a.ndim)

    tile_spec = pl.BlockSpec((None, 1, tile_.shape[-1]), lambda b, i: (i, 0, 0))
    tok = lambda w: pl.BlockSpec((None, tm, w), lambda b, i: (b, i, 0))
    grp = pl.BlockSpec((G, None, tm, LANES), lambda b, i: (0, b, i, 0))
    sd = jax.ShapeDtypeStruct
    return pl.pallas_call(
        functools.partial(_proj_body, nsa_scale=NSA_HEAD_DIM ** -0.5 * LOG2E,
                          mla_scale=(MLA_NOPE_DIM + MLA_ROPE_DIM) ** -0.5 * LOG2E),
        grid=(B, T // tm),
        in_specs=[tok(D), row, row, full(w_all), full(gq), full(gkv), full(wuq), full(wuk), full(wuv)]
                 + [full(base), tile_spec, full(signs)],
        out_specs=[tok(NSA_HEADS * NSA_HEAD_DIM), grp, grp, grp, tok(G * LANES),
                   tok(MLA_HEADS * LANES), tok(MLA_HEADS * LANES), tok(MLA_HEADS * MLA_V_DIM)],
        out_shape=[sd((B, T, NSA_HEADS * NSA_HEAD_DIM), BF16), sd((G, B, T, LANES), BF16),
                   sd((G, B, T, LANES), BF16), sd((G, B, T, LANES), BF16), sd((B, T, G * LANES), F32),
                   sd((B, T, MLA_HEADS * LANES), BF16), sd((B, T, MLA_HEADS * LANES), BF16),
                   sd((B, T, MLA_HEADS * MLA_V_DIM), BF16)],
        compiler_params=_cparams(("parallel", "parallel")),
        name="proj",
    )(x, sh, sc, w_all, gq, gkv, wuq, wuk, wuv, base, tile_, signs)


def _gelu_tanh(x):
    return x * (0.5 * (1.0 + jnp.tanh(0.7978845608028654 * (x + 0.044715 * (x * x * x)))))


def _cmp_body(c_ref, w1a_ref, w1b_ref, pa_ref, pb_ref, w2_ref, o_ref):
    c = c_ref[...]
    nc = c.shape[0]
    w1a, w1b = w1a_ref[...], w1b_ref[...]
    a = _dot(c, w1a)
    b = _dot(c, w1b)
    bias = _dot(pa_ref[...], w1a) + _dot(pb_ref[...], w1b)
    pre = a + pltpu.roll(b, nc - 1, 0) + bias[0:1]
    o_ref[...] = _dot(_gelu_tanh(pre).astype(BF16), w2_ref[...]).astype(BF16)


def _cmp_call(kvc, w1a, w1b, pa, pb, w2):
    G, B, T, _ = kvc.shape
    nc = T // CMP_STRIDE
    c = kvc.reshape(G, B, nc, CMP_STRIDE * LANES)

    def full(a):
        return pl.BlockSpec(a.shape, lambda g, b: (0,) * a.ndim)

    return pl.pallas_call(
        _cmp_body,
        grid=(G, B),
        in_specs=[pl.BlockSpec((None, None, nc, CMP_STRIDE * LANES), lambda g, b: (g, b, 0, 0)),
                  full(w1a), full(w1b), full(pa), full(pb), full(w2)],
        out_specs=pl.BlockSpec((None, None, nc, LANES), lambda g, b: (g, b, 0, 0)),
        out_shape=jax.ShapeDtypeStruct((G, B, nc, LANES), BF16),
        compiler_params=_cparams(("parallel", "parallel")),
        name="cmp",
    )(c, w1a, w1b, pa, pb, w2)


def _nsa_body(q_ref, gate_ref, kvc_ref, kvs_ref, kvw_ref, ov_ref, et_ref, band_ref, o_ref,
              m_scr, l_scr, acc_scr, part_scr, *, n_slc):
    i = pl.program_id(2)
    nc = kvc_ref.shape[0]
    hpg, tq_n, rows, tk = NSA_HPG, NSA_TQ, NSA_HPG * NSA_TQ, NSA_TK
    t0 = pl.multiple_of(i * tq_n, tq_n)
    q = q_ref[...].astype(F32)
    low = _iota((tq_n, LANES), 1) < NSA_HEAD_DIM
    heads = []
    for h in range(hpg):
        tile_ = q[:, (h // 2) * LANES:(h // 2 + 1) * LANES]
        heads.append(jnp.where(low, tile_ if h % 2 == 0 else pltpu.roll(tile_, NSA_HEAD_DIM, 1), 0.0))
    qs = jnp.concatenate(heads, axis=0).astype(BF16)

    wk = WINDOW + tq_n
    start = pl.multiple_of(jnp.maximum(t0 - WINDOW, 0), tq_n)
    kvw = kvw_ref[pl.ds(start, wk), :]
    s_w = _dot_t(qs, kvw)
    s_w = s_w + _rep_rows(band_ref[...], hpg)
    e_w = jnp.exp2(s_w - jnp.max(s_w, -1, keepdims=True))
    o_win = _dot(e_w.astype(BF16), kvw) / jnp.sum(e_w, -1, keepdims=True)

    kvc = kvc_ref[...]
    s = _dot_t(qs, kvc)
    tq = t0 + _iota((tq_n, nc), 0)
    n = _iota((tq_n, nc), 1)
    valid = (n * CMP_STRIDE + (CMP_BLOCK - 1) <= tq) & (n < nc - 1)
    s = s + _rep_rows(jnp.where(valid, 0.0, NEG), hpg)
    e = jnp.exp2(s - jnp.maximum(jnp.max(s, -1, keepdims=True), M_INIT))
    p = e / jnp.maximum(jnp.sum(e, -1, keepdims=True), 1e-30)
    o_cmp = _dot(p.astype(BF16), kvc)

    ps = p[0:tq_n] + p[tq_n:2 * tq_n] + p[2 * tq_n:3 * tq_n] + p[3 * tq_n:4 * tq_n]
    ph = ps.astype(BF16)
    plo = (ps - ph.astype(F32)).astype(BF16)
    ov = ov_ref[...]
    imp_t = (_dot(ph, ov) + _dot(plo, ov)).T

    jr = _iota((LANES, tq_n), 0)
    tl = t0 + _iota((LANES, tq_n), 1)
    cur = lax.shift_right_logical(tl, SLC_SHIFT)
    causal = jr * SLC_BLOCK <= tl
    forced = ((jr == 0) | (jr == cur) | (jr == cur - 1)) & causal
    v = jnp.where(forced, PICKED, jnp.where(causal, imp_t, -1.0))
    if n_slc < LANES:
        v = jnp.where(jr < n_slc, v, PICKED)
    jf = jr.astype(F32)
    sel_t = jnp.where(forced, 1.0, 0.0)
    n_forced = 3
    for _ in range(SLC_TOPK - n_forced):
        mx = jnp.max(v, axis=0, keepdims=True)
        idx = jnp.min(jnp.where(v == mx, jf, float(LANES)), axis=0, keepdims=True)
        hit = jf == idx
        sel_t = jnp.where(hit, 1.0, sel_t)
        v = jnp.where(hit, PICKED, v)
    sel = sel_t.T

    gt = gate_ref[...]
    gcol = lambda br: jnp.concatenate([gt[:, h * 3 + br:h * 3 + br + 1] for h in range(hpg)], axis=0)
    part_scr[...] = gcol(0) * o_cmp + gcol(2) * o_win

    upto = _iota((tq_n, LANES), 1) < lax.shift_right_logical(t0 + tq_n, SLC_SHIFT)
    lhs = jnp.concatenate([qs, _rep_rows(jnp.where((sel > 0.5) & upto, 0.0, NEG).astype(BF16), hpg)], axis=1)
    tk_shift = tk.bit_length() - 1
    n_full = lax.shift_right_logical(t0 + tq_n - 1, tk_shift)
    tail_idx = lax.shift_right_logical(t0 + tq_n - 1 - n_full * tk, tq_n.bit_length() - 1)

    m_scr[...] = jnp.full(m_scr.shape, M_INIT, F32)
    l_scr[...] = jnp.zeros(l_scr.shape, F32)
    acc_scr[...] = jnp.zeros(acc_scr.shape, F32)
    rc = ROW_CHUNK

    def tile(k, width, last):
        k0 = pl.multiple_of(k * tk, tk)
        kv = kvs_ref[pl.ds(k0, width), :]
        rhs = jnp.concatenate([kv, et_ref[pl.ds(k0, width), :]], axis=1)
        s_heads = [_dot_t(lhs[h * tq_n:(h + 1) * tq_n], rhs) for h in range(hpg)]
        for h in range(hpg):
            hs = slice(h * tq_n, (h + 1) * tq_n)
            ps, als = [], []
            for r in range(0, tq_n, rc):
                sl = slice(h * tq_n + r, h * tq_n + r + rc)
                sc = s_heads[h][r:r + rc]
                if last:
                    own = jnp.where(_iota((rc, tq_n), 1) <= r + _iota((rc, tq_n), 0), 0.0, NEG)
                    sc = sc + own if width == tq_n else jnp.concatenate(
                        [sc[:, :width - tq_n], sc[:, width - tq_n:] + own], axis=1)
                m_prev = m_scr[sl]
                m_new = jnp.maximum(m_prev, jnp.max(sc, -1, keepdims=True))
                alpha = jnp.exp2(m_prev - m_new)
                pp = jnp.exp2(sc - _rep_lanes(m_new, width // LANES))
                l_scr[sl] = alpha * l_scr[sl] + jnp.sum(pp, -1, keepdims=True)
                m_scr[sl] = m_new
                ps.append(pp.astype(BF16))
                als.append(alpha)
            acc_scr[hs] = jnp.concatenate(als, axis=0) * acc_scr[hs] + _dot(jnp.concatenate(ps, axis=0), kv)

    def body(k, carry):
        tile(k, tk, False)
        return carry

    lax.fori_loop(0, n_full, body, 0)
    for j in range(tk // tq_n):
        pl.when(tail_idx == j)(functools.partial(tile, n_full, (j + 1) * tq_n, True))
    acc = acc_scr[...]
    out = part_scr[...] + gcol(1) * (acc / l_scr[...])
    out = jnp.where(_iota((rows, LANES), 1) >= NSA_HEAD_DIM, out, 0.0)
    for hp in range(hpg // 2):
        even = out[2 * hp * tq_n:(2 * hp + 1) * tq_n]
        odd = out[(2 * hp + 1) * tq_n:(2 * hp + 2) * tq_n]
        o_ref[:, hp * LANES:(hp + 1) * LANES] = (pltpu.roll(even, NSA_HEAD_DIM, 1) + odd).astype(BF16)


def _nsa_call(qn, gates, kvcmp, kvs, kvw, ov, et, band):
    B, T, _ = qn.shape
    G = NSA_KV_GROUPS
    nc = kvcmp.shape[2]
    n_slc = T // SLC_BLOCK
    assert SLC_TOPK <= n_slc <= LANES and nc % LANES == 0 and T % NSA_TK == 0 and T >= WINDOW + NSA_TQ
    assert NSA_TK & (NSA_TK - 1) == 0 and NSA_TQ % SLC_BLOCK == 0 and NSA_TK % NSA_TQ == 0
    assert WINDOW % NSA_TQ == 0 and band.shape == (WINDOW // NSA_TQ + 1, NSA_TQ, WINDOW + NSA_TQ)
    gw = NSA_HPG * NSA_HEAD_DIM
    rows = NSA_HPG * NSA_TQ
    whole = lambda r: pl.BlockSpec((None, None, r, LANES), lambda b, g, i: (g, b, 0, 0))
    const = lambda a: pl.BlockSpec(a.shape, lambda b, g, i: (0, 0))
    return pl.pallas_call(
        functools.partial(_nsa_body, n_slc=n_slc),
        grid=(B, G, T // NSA_TQ),
        in_specs=[pl.BlockSpec((None, NSA_TQ, gw), lambda b, g, i: (b, i, g)),
                  pl.BlockSpec((None, NSA_TQ, LANES), lambda b, g, i: (b, i, g)),
                  whole(nc), whole(T), whole(T), const(ov), const(et),
                  pl.BlockSpec((None,) + band.shape[1:], lambda b, g, i: (jnp.minimum(i, band.shape[0] - 1), 0, 0))],
        out_specs=pl.BlockSpec((None, NSA_TQ, NSA_HPG * NSA_HEAD_DIM), lambda b, g, i: (b, i, g)),
        out_shape=jax.ShapeDtypeStruct((B, T, NSA_HEADS * NSA_HEAD_DIM), BF16),
        scratch_shapes=[pltpu.VMEM((rows, LANES), F32)] * 4,
        compiler_params=_cparams(("parallel", "parallel", "arbitrary")),
        name="nsa",
    )(qn, gates, kvcmp, kvs, kvw, ov, et, band)


def _mla_body(q_ref, k_ref, v_ref, o_ref, p_buf, a_buf, m_scr, l_scr, acc_scr):
    i = pl.program_id(2)
    tt, nh = MLA_T, MLA_HPS
    head = lambda hh: slice(hh * LANES, (hh + 1) * LANES)

    pair = lambda hh: slice((hh // 2) * LANES, (hh // 2 + 1) * LANES)

    def scores(kt, hh):
        k0 = pl.multiple_of(kt * tt, tt)
        return _dot_t(q_ref[:, head(hh)], k_ref[pl.ds(k0, tt), head(hh)])

    def values(kt, hh):
        return v_ref[pl.ds(pl.multiple_of(kt * tt, tt), tt), pair(hh)]

    m_scr[...] = jnp.full(m_scr.shape, M_INIT, F32)
    l_scr[...] = jnp.zeros(l_scr.shape, F32)
    acc_scr[...] = jnp.zeros(acc_scr.shape, F32)
    p_buf[...] = jnp.zeros(p_buf.shape, BF16)
    a_buf[...] = jnp.ones(a_buf.shape, F32)

    def stage(k, diagonal):
        kp = jnp.maximum(k - 1, 0)
        if diagonal:
            tri = jnp.where(_iota((tt, tt), 1) <= _iota((tt, tt), 0), 0.0, NEG)
        for hh in range(nh):
            s = scores(k, hh)
            if diagonal:
                s = s + tri
            acc_scr[hh] = a_buf[hh] * acc_scr[hh] + _dot(p_buf[hh], values(kp, hh))
            m_prev = m_scr[hh]
            m_new = jnp.maximum(m_prev, jnp.max(s, -1, keepdims=True))
            alpha = jnp.exp2(m_prev - m_new)
            p = jnp.exp2(s - _rep_lanes(m_new, tt // LANES))
            l_scr[hh] = alpha * l_scr[hh] + jnp.sum(p, -1, keepdims=True)
            m_scr[hh] = m_new
            a_buf[hh] = alpha
            p_buf[hh] = p.astype(BF16)

    def body(k, carry):
        stage(k, False)
        return carry

    lax.fori_loop(0, i, body, 0)
    stage(i, True)
    outs = [(a_buf[hh] * acc_scr[hh] + _dot(p_buf[hh], values(i, hh))) / l_scr[hh] for hh in range(nh)]
    first = _iota((tt, LANES), 1) < MLA_V_DIM
    for pp in range(nh // 2):
        o_ref[:, pp * LANES:(pp + 1) * LANES] = jnp.where(first, outs[2 * pp], outs[2 * pp + 1]).astype(BF16)


def _mla_call(qm, km, vm):
    B, T, _ = qm.shape
    nh = MLA_HPS
    assert T % MLA_T == 0 and MLA_HEADS % nh == 0 and nh % 2 == 0
    vw = nh * MLA_V_DIM
    return pl.pallas_call(
        _mla_body,
        grid=(B, MLA_HEADS // nh, T // MLA_T),
        in_specs=[pl.BlockSpec((None, MLA_T, nh * LANES), lambda b, p, i: (b, i, p)),
                  pl.BlockSpec((None, T, nh * LANES), lambda b, p, i: (b, 0, p)),
                  pl.BlockSpec((None, T, vw), lambda b, p, i: (b, 0, p))],
        out_specs=pl.BlockSpec((None, MLA_T, vw), lambda b, p, i: (b, i, p)),
        out_shape=jax.ShapeDtypeStruct((B, T, MLA_HEADS * MLA_V_DIM), BF16),
        scratch_shapes=[pltpu.VMEM((nh, MLA_T, MLA_T), BF16)] + [pltpu.VMEM((nh, MLA_T, LANES), F32)] * 4,
        compiler_params=_cparams(("parallel", "parallel", "arbitrary"), MLA_VMEM_LIMIT),
        name="mla",
    )(qm, km, vm)


def _out_body(x_ref, on_ref, om_ref, g_ref, wn_ref, wm_ref, lng_ref, lnb_ref, o_ref, *, alpha):
    tm = x_ref.shape[0]
    rc = min(tm, OUT_ROWS)
    ys = [_dot(on_ref[r:r + rc, :], wn_ref[...]) + _dot(om_ref[r:r + rc, :], wm_ref[...]) for r in range(0, tm, rc)]
    for c, r in enumerate(range(0, tm, rc)):
        z = alpha * x_ref[r:r + rc, :] + (1.0 + g_ref[...]) * ys[c]
        o_ref[r:r + rc, :] = _layer_norm(z, lng_ref[...], lnb_ref[...])


def _out_call(x, o_nsa, o_mla, g, w_on, w_om, lng, lnb, alpha):
    B, T, D = x.shape
    tm = min(T, 1024)
    tok = lambda w: pl.BlockSpec((None, tm, w), lambda b, i: (b, i, 0))
    full = lambda a: pl.BlockSpec(a.shape, lambda b, i: (0,) * a.ndim)
    lng, lnb = lng.reshape(1, D), lnb.reshape(1, D)
    return pl.pallas_call(
        functools.partial(_out_body, alpha=alpha),
        grid=(B, T // tm),
        in_specs=[tok(D), tok(o_nsa.shape[-1]), tok(o_mla.shape[-1]),
                  pl.BlockSpec((None, 1, D), lambda b, i: (b, 0, 0)),
                  full(w_on), full(w_om), full(lng), full(lnb)],
        out_specs=tok(D),
        out_shape=jax.ShapeDtypeStruct((B, T, D), F32),
        compiler_params=_cparams(("parallel", "parallel")),
        name="out",
    )(x, o_nsa, o_mla, g, w_on, w_om, lng, lnb)


ROPE_SETS = (((0, NSA_HEAD_DIM), NSA_ROPE_DIM),
             ((0,), NSA_ROPE_DIM),
             ((MLA_NOPE_DIM,), MLA_ROPE_DIM))
def _rope_tables(T, tm):
    inv_rows, sign_rows = [], []
    for lane0s, dim in ROPE_SETS:
        half = dim // 2
        inv = ROPE_THETA ** (-jnp.arange(half, dtype=F32) / half)
        lane_inv = jnp.zeros((LANES,), F32)
        first, second = np.zeros((LANES,), np.float32), np.zeros((LANES,), np.float32)
        for lane0 in lane0s:
            lane_inv = lane_inv.at[lane0:lane0 + half].set(inv).at[lane0 + half:lane0 + dim].set(inv)
            first[lane0:lane0 + half] = -1.0
            second[lane0 + half:lane0 + dim] = 1.0
        inv_rows.append(lane_inv)
        sign_rows += [first, second]
    inv_all = jnp.stack(inv_rows)

    def trig(pos):
        ang = pos.astype(F32)[:, None, None] * inv_all[None]
        return jnp.stack([jnp.cos(ang), jnp.sin(ang)], axis=2).reshape(pos.shape[0], -1)

    signs = jnp.broadcast_to(jnp.asarray(np.concatenate(sign_rows))[None, :], (8, 2 * len(ROPE_SETS) * LANES))
    return trig(jnp.arange(tm)), trig(jnp.arange(0, T, tm))[:, None, :], signs


def _pad_slots(w, n, width):
    K = w.shape[0]
    return jnp.pad(w.reshape(K, n, width), ((0, 0), (0, 0), (0, LANES - width))).reshape(K, n * LANES)


def _proj_weights(w_in):
    D = w_in.shape[0]
    hd, G = NSA_HEAD_DIM, NSA_KV_GROUPS
    nq, nkv = NSA_HEADS * hd, G * hd
    off = np.cumsum([0, nq] + [nkv] * 6 + [3 * NSA_HEADS, MLA_Q_RANK, MLA_KV_RANK, MLA_ROPE_DIM])
    cols = [w_in[:, :nq]]
    for pair in range(3):
        ko, vo = off[1 + 2 * pair], off[2 + 2 * pair]
        for g in range(G):
            cols += [w_in[:, ko + g * hd:ko + (g + 1) * hd], w_in[:, vo + g * hd:vo + (g + 1) * hd]]
    per_g = 3 * NSA_HPG
    for g in range(G):
        cols.append(jnp.pad(w_in[:, off[7] + g * per_g:off[7] + (g + 1) * per_g], ((0, 0), (0, LANES - per_g))))
    cols += [w_in[:, off[8]:off[9]], w_in[:, off[9]:off[10]]]
    cols.append(jnp.pad(w_in[:, off[10]:off[11]],
                        ((0, 0), (MLA_NOPE_DIM, LANES - MLA_NOPE_DIM - MLA_ROPE_DIM))))
    w_all = jnp.concatenate(cols, axis=1).astype(BF16)
    assert w_all.shape == (D, PROJ_W)
    return w_all


def _cmp_weights(k_pos, k_w1, k_w2, v_pos, v_w1, v_w2):
    hd, st, hid = NSA_HEAD_DIM, CMP_STRIDE, CMP_HIDDEN

    def half(lo):
        w = jnp.zeros((st, LANES, 2 * hid), F32)
        w = w.at[:, :hd, :hid].set(k_w1[lo * hd:(lo + st) * hd].reshape(st, hd, hid))
        w = w.at[:, hd:, hid:].set(v_w1[lo * hd:(lo + st) * hd].reshape(st, hd, hid))
        pos = jnp.concatenate([k_pos[lo:lo + st], v_pos[lo:lo + st]], axis=1).reshape(1, st * LANES)
        return w.reshape(st * LANES, 2 * hid).astype(BF16), jnp.broadcast_to(pos, (8, st * LANES)).astype(BF16)

    w1a, pa = half(0)
    w1b, pb = half(st)
    w2 = jnp.zeros((2 * hid, LANES), F32).at[:hid, :hd].set(k_w2).at[hid:, hd:].set(v_w2).astype(BF16)
    return w1a, w1b, pa, pb, w2


def _block_tables(T):
    nc, n_slc = T // CMP_STRIDE, T // SLC_BLOCK
    cs = np.arange(nc)[:, None] * CMP_STRIDE
    blk = np.arange(LANES)[None, :]
    ov = (cs < blk * SLC_BLOCK + SLC_BLOCK) & (cs + CMP_BLOCK > blk * SLC_BLOCK) & (np.arange(nc)[:, None] < nc - 1) & (blk < n_slc)
    et = (np.arange(T)[:, None] // SLC_BLOCK) == blk
    off = np.arange(WINDOW // NSA_TQ + 1)[:, None, None] * NSA_TQ
    diff = off + np.arange(NSA_TQ)[None, :, None] - np.arange(WINDOW + NSA_TQ)[None, None, :]
    band = np.where((diff >= 0) & (diff < WINDOW), 0.0, NEG).astype(np.float32)
    return jnp.asarray(ov.astype(np.float32), BF16), jnp.asarray(et.astype(np.float32), BF16), jnp.asarray(band)


def kernel(x, c, w_ada, b_ada, ln_g, ln_b, ffn1_wg, ffn1_wu, ffn1_wd, w_in, cmp_k_pos, cmp_k_w1, cmp_k_w2,
           cmp_v_pos, cmp_v_w1, cmp_v_w2, mla_q_norm, mla_kv_norm, mla_w_uq, mla_w_ukv, w_out,
           ffn2_wg, ffn2_wu, ffn2_wd):
    B, T, D = x.shape
    depth = w_ada.shape[0]
    alpha = (2.0 * depth) ** 0.25
    ov, et, band = _block_tables(T)
    for l in range(depth):
        mod = _mod_call(c, w_ada[l], b_ada[l])
        sh1, sc1, g1, sh2, sc2, g2, sh3, sc3, g3 = [mod[:, k] for k in range(N_MOD)]
        x = _ffn_call(x, sh1, sc1, g1, ffn1_wg[l], ffn1_wu[l], ffn1_wd[l],
                      ln_g[l, 0], ln_b[l, 0], alpha, 0.5)

        wuq = _pad_slots(mla_w_uq[l], MLA_HEADS, MLA_NOPE_DIM + MLA_ROPE_DIM).astype(BF16)
        wukv = mla_w_ukv[l].reshape(MLA_KV_RANK, MLA_HEADS, MLA_NOPE_DIM + MLA_V_DIM)
        wuk = _pad_slots(wukv[:, :, :MLA_NOPE_DIM].reshape(MLA_KV_RANK, -1), MLA_HEADS, MLA_NOPE_DIM).astype(BF16)
        wuv = wukv[:, :, MLA_NOPE_DIM:].reshape(MLA_KV_RANK, -1).astype(BF16)
        qn, kvc, kvs, kvw, gates, qm, km, vm = _proj_call(
            x, sh2, sc2, _proj_weights(w_in[l]), mla_q_norm[l].reshape(1, -1), mla_kv_norm[l].reshape(1, -1),
            wuq, wuk, wuv)
        kvcmp = _cmp_call(kvc, *_cmp_weights(cmp_k_pos[l], cmp_k_w1[l], cmp_k_w2[l],
                                             cmp_v_pos[l], cmp_v_w1[l], cmp_v_w2[l]))
        o_nsa = _nsa_call(qn, gates, kvcmp, kvs, kvw, ov, et, band)
        o_mla = _mla_call(qm, km, vm)
        n_nsa = NSA_HEADS * NSA_HEAD_DIM
        w_o = w_out[l].astype(BF16)
        x = _out_call(x, o_nsa, o_mla, g2, w_o[:n_nsa], w_o[n_nsa:], ln_g[l, 1], ln_b[l, 1], alpha)

        x = _ffn_call(x, sh3, sc3, g3, ffn2_wg[l], ffn2_wu[l], ffn2_wd[l],
                      ln_g[l, 2], ln_b[l, 2], alpha, 0.5)
    return x
```

```python
import functools

import numpy as np
import jax
import jax.numpy as jnp
from jax import lax
from jax.experimental import pallas as pl
from jax.experimental.pallas import tpu as pltpu

F32 = jnp.float32
BF16 = jnp.bfloat16

ROPE_THETA = 500000.0
NSA_HEADS = 8
NSA_KV_GROUPS = 2
NSA_HPG = NSA_HEADS // NSA_KV_GROUPS
NSA_HEAD_DIM = 64
NSA_ROPE_DIM = 16
CMP_BLOCK = 32
CMP_STRIDE = 16
CMP_HIDDEN = 256
SLC_BLOCK = 64
SLC_TOPK = 16
SLC_SHIFT = SLC_BLOCK.bit_length() - 1
WINDOW = 512
FORCE_SCORE = 1e9
MLA_HEADS = 8
MLA_NOPE_DIM = 64
MLA_ROPE_DIM = 32
MLA_V_DIM = 64
MLA_Q_RANK = 384
MLA_KV_RANK = 256
EPS = 1e-5
N_MOD = 9

LANES = 128
NEG = -1e30
M_INIT = -1e29
PICKED = -3e38
ROW_CHUNK = 128
LOG2E = 1.4426950408889634
VMEM_LIMIT = 48 * 1024 * 1024
MLA_VMEM_LIMIT = 56 * 1024 * 1024

NSA_TQ = 256
NSA_TK = 1024
MLA_T = 512
MLA_HPS = 4
OUT_ROWS = 256
PROJ_ROWS = 256

OFF_Q, OFF_KV, OFF_GATE, OFF_CQ, OFF_CKV, OFF_KR, PROJ_W = 0, 512, 1280, 1536, 1920, 2176, 2304


def _cparams(sem, vmem=VMEM_LIMIT):
    return pltpu.CompilerParams(dimension_semantics=sem, vmem_limit_bytes=vmem)


def _layer_norm(z, g, b):
    mu = jnp.mean(z, -1, keepdims=True)
    zc = z - mu
    var = jnp.mean(zc * zc, -1, keepdims=True)
    return zc * lax.rsqrt(var + EPS) * g + b


def _dot(a, b):
    return jnp.dot(a, b, preferred_element_type=F32)


def _dot_t(a, b):
    return lax.dot_general(a, b, (((1,), (1,)), ((), ())), preferred_element_type=F32)


def _rep_rows(a, n):
    return jnp.concatenate([a] * n, axis=0)


def _rep_lanes(a, n):
    return jnp.concatenate([a] * n, axis=1)


def _iota(shape, d):
    return lax.broadcasted_iota(jnp.int32, shape, d)


def _mod_body(c_ref, w_ref, b_ref, o_ref):
    c = c_ref[...]
    s = (c * jax.nn.sigmoid(c)).astype(BF16)
    o_ref[...] = _dot(s, w_ref[...].astype(BF16)) + b_ref[...]


def _mod_call(c, w_ada, b_ada):
    B, D = c.shape
    rows = 8
    c8 = jnp.zeros((rows, D), F32).at[:B].set(c)
    out = pl.pallas_call(
        _mod_body,
        grid=(N_MOD,),
        in_specs=[pl.BlockSpec((rows, D), lambda j: (0, 0)),
                  pl.BlockSpec((D, D), lambda j: (0, j)),
                  pl.BlockSpec((1, D), lambda j: (0, j))],
        out_specs=pl.BlockSpec((rows, D), lambda j: (0, j)),
        out_shape=jax.ShapeDtypeStruct((rows, N_MOD * D), F32),
        compiler_params=_cparams(("parallel",)),
        name="mod",
    )(c8, w_ada, b_ada.reshape(1, N_MOD * D))
    return out[:B].reshape(B, N_MOD, 1, D)


def _ffn_body(x_ref, sh_ref, sc_ref, g_ref, wg_ref, wu_ref, wd_ref, lng_ref, lnb_ref, o_ref,
              u_scr, acc_scr, *, alpha, coef, n_ff):
    j = pl.program_id(2)

    @pl.when(j == 0)
    def _():
        u_scr[...] = (x_ref[...] * (1.0 + sc_ref[...]) + sh_ref[...]).astype(BF16)
        acc_scr[...] = jnp.zeros_like(acc_scr)

    u = u_scr[...]
    a = _dot(u, wg_ref[...].astype(BF16))
    b = _dot(u, wu_ref[...].astype(BF16))
    h = (a * jax.nn.sigmoid(a)) * b
    acc_scr[...] += _dot(h.astype(BF16), wd_ref[...].astype(BF16))

    @pl.when(j == n_ff - 1)
    def _():
        z = alpha * x_ref[...] + coef * (1.0 + g_ref[...]) * acc_scr[...]
        o_ref[...] = _layer_norm(z, lng_ref[...], lnb_ref[...])


def _ffn_call(x, sh, sc, g, wg, wu, wd, lng, lnb, alpha, coef):
    B, T, D = x.shape
    FF = wg.shape[1]
    tm = min(T, 1024)
    tf = 256
    assert T % tm == 0 and FF % tf == 0
    n_ff = FF // tf
    row = pl.BlockSpec((None, 1, D), lambda b, i, j: (b, 0, 0))
    vec = pl.BlockSpec((1, D), lambda b, i, j: (0, 0))
    return pl.pallas_call(
        functools.partial(_ffn_body, alpha=alpha, coef=coef, n_ff=n_ff),
        grid=(B, T // tm, n_ff),
        in_specs=[pl.BlockSpec((None, tm, D), lambda b, i, j: (b, i, 0)), row, row, row,
                  pl.BlockSpec((D, tf), lambda b, i, j: (0, j)),
                  pl.BlockSpec((D, tf), lambda b, i, j: (0, j)),
                  pl.BlockSpec((tf, D), lambda b, i, j: (j, 0)), vec, vec],
        out_specs=pl.BlockSpec((None, tm, D), lambda b, i, j: (b, i, 0)),
        out_shape=jax.ShapeDtypeStruct((B, T, D), F32),
        scratch_shapes=[pltpu.VMEM((tm, D), BF16), pltpu.VMEM((tm, D), F32)],
        compiler_params=_cparams(("parallel", "parallel", "arbitrary")),
        name="ffn",
    )(x, sh, sc, g, wg, wu, wd, lng.reshape(1, D), lnb.reshape(1, D))


def _rope_slot(x, c, sa, sb, half):
    return x * c + pltpu.roll(x, LANES - half, 1) * sa + pltpu.roll(x, half, 1) * sb


def _rms_norm(x, g):
    return x * lax.rsqrt(jnp.mean(x * x, -1, keepdims=True) + EPS) * g


def _proj_body(x_ref, sh_ref, sc_ref, w_ref, gq_ref, gkv_ref, wuq_ref, wuk_ref, wuv_ref,
               base_ref, tile_ref, sgn_ref,
               qn_ref, kvc_ref, kvs_ref, kvw_ref, gate_ref, qm_ref, km_ref, vm_ref,
               *, nsa_scale, mla_scale):
    tm = x_ref.shape[0]
    rc = min(tm, PROJ_ROWS)
    starts = range(0, tm, rc)
    nh = NSA_ROPE_DIM // 2
    mh = MLA_ROPE_DIM // 2
    hs = [_dot((x_ref[r:r + rc, :] * (1.0 + sc_ref[...]) + sh_ref[...]).astype(BF16), w_ref[...])
          for r in starts]
    for c, r in enumerate(starts):
        h, rows = hs[c], slice(r, r + rc)
        tabs = []
        for st in range(len(ROPE_SETS)):
            lc, ls = slice(2 * st * LANES, (2 * st + 1) * LANES), slice((2 * st + 1) * LANES, (2 * st + 2) * LANES)
            cb, sb_, ca, sa_ = base_ref[rows, lc], base_ref[rows, ls], tile_ref[:, lc], tile_ref[:, ls]
            sin = sa_ * cb + ca * sb_
            tabs += [ca * cb - sa_ * sb_, sin * sgn_ref[0:1, lc], sin * sgn_ref[0:1, ls]]
        cq, sqa, sqb, cn, sna, snb, cm, sma, smb = tabs

        for s in range(NSA_HEADS * NSA_HEAD_DIM // LANES):
            lo = OFF_Q + s * LANES
            qn_ref[rows, s * LANES:(s + 1) * LANES] = (
                _rope_slot(h[:, lo:lo + LANES], cq, sqa, sqb, nh) * nsa_scale).astype(BF16)
        for idx, ref in enumerate((kvc_ref, kvs_ref, kvw_ref)):
            for g in range(NSA_KV_GROUPS):
                lo = OFF_KV + (idx * NSA_KV_GROUPS + g) * LANES
                ref[g, rows, :] = _rope_slot(h[:, lo:lo + LANES], cn, sna, snb, nh).astype(BF16)
        gate_ref[rows, :] = jax.nn.sigmoid(h[:, OFF_GATE:OFF_GATE + NSA_KV_GROUPS * LANES])

        cqn = _rms_norm(h[:, OFF_CQ:OFF_CQ + MLA_Q_RANK], gq_ref[...]).astype(BF16)
        qm = _dot(cqn, wuq_ref[...])
        ckvn = _rms_norm(h[:, OFF_CKV:OFF_CKV + MLA_KV_RANK], gkv_ref[...]).astype(BF16)
        kn = _dot(ckvn, wuk_ref[...])
        vm_ref[rows, :] = _dot(ckvn, wuv_ref[...]).astype(BF16)
        kr = _rope_slot(h[:, OFF_KR:OFF_KR + LANES], cm, sma, smb, mh)
        for s in range(MLA_HEADS):
            sl = slice(s * LANES, (s + 1) * LANES)
            qm_ref[rows, sl] = (_rope_slot(qm[:, sl], cm, sma, smb, mh) * mla_scale).astype(BF16)
            km_ref[rows, sl] = (kn[:, sl] + kr).astype(BF16)


def _proj_call(x, sh, sc, w_all, gq, gkv, wuq, wuk, wuv):
    B, T, D = x.shape
    tm = min(T, 1024)
    G = NSA_KV_GROUPS
    base, tile_, signs = _rope_tables(T, tm)
    row = pl.BlockSpec((None, 1, D), lambda b, i: (b, 0, 0))

    def full(a):
        return pl.BlockSpec(a.shape, lambda b, i: (0,) * a.ndim)

    tile_spec = pl.BlockSpec((None, 1, tile_.shape[-1]), lambda b, i: (i, 0, 0))
    tok = lambda w: pl.BlockSpec((None, tm, w), lambda b, i: (b, i, 0))
    grp = pl.BlockSpec((G, None, tm, LANES), lambda b, i: (0, b, i, 0))
    sd = jax.ShapeDtypeStruct
    return pl.pallas_call(
        functools.partial(_proj_body, nsa_scale=NSA_HEAD_DIM ** -0.5 * LOG2E,
                          mla_scale=(MLA_NOPE_DIM + MLA_ROPE_DIM) ** -0.5 * LOG2E),
        grid=(B, T // tm),
        in_specs=[tok(D), row, row, full(w_all), full(gq), full(gkv), full(wuq), full(wuk), full(wuv)]
                 + [full(base), tile_spec, full(signs)],
        out_specs=[tok(NSA_HEADS * NSA_HEAD_DIM), grp, grp, grp, tok(G * LANES),
                   tok(MLA_HEADS * LANES), tok(MLA_HEADS * LANES), tok(MLA_HEADS * MLA_V_DIM)],
        out_shape=[sd((B, T, NSA_HEADS * NSA_HEAD_DIM), BF16), sd((G, B, T, LANES), BF16),
                   sd((G, B, T, LANES), BF16), sd((G, B, T, LANES), BF16), sd((B, T, G * LANES), F32),
                   sd((B, T, MLA_HEADS * LANES), BF16), sd((B, T, MLA_HEADS * LANES), BF16),
                   sd((B, T, MLA_HEADS * MLA_V_DIM), BF16)],
        compiler_params=_cparams(("parallel", "parallel")),
        name="proj",
    )(x, sh, sc, w_all, gq, gkv, wuq, wuk, wuv, base, tile_, signs)


def _gelu_tanh(x):
    return x * (0.5 * (1.0 + jnp.tanh(0.7978845608028654 * (x + 0.044715 * (x * x * x)))))


def _cmp_body(c_ref, w1a_ref, w1b_ref, pa_ref, pb_ref, w2_ref, o_ref):
    c = c_ref[...]
    nc = c.shape[0]
    w1a, w1b = w1a_ref[...], w1b_ref[...]
    a = _dot(c, w1a)
    b = _dot(c, w1b)
    bias = _dot(pa_ref[...], w1a) + _dot(pb_ref[...], w1b)
    pre = a + pltpu.roll(b, nc - 1, 0) + bias[0:1]
    o_ref[...] = _dot(_gelu_tanh(pre).astype(BF16), w2_ref[...]).astype(BF16)


def _cmp_call(kvc, w1a, w1b, pa, pb, w2):
    G, B, T, _ = kvc.shape
    nc = T // CMP_STRIDE
    c = kvc.reshape(G, B, nc, CMP_STRIDE * LANES)

    def full(a):
        return pl.BlockSpec(a.shape, lambda g, b: (0,) * a.ndim)

    return pl.pallas_call(
        _cmp_body,
        grid=(G, B),
        in_specs=[pl.BlockSpec((None, None, nc, CMP_STRIDE * LANES), lambda g, b: (g, b, 0, 0)),
                  full(w1a), full(w1b), full(pa), full(pb), full(w2)],
        out_specs=pl.BlockSpec((None, None, nc, LANES), lambda g, b: (g, b, 0, 0)),
        out_shape=jax.ShapeDtypeStruct((G, B, nc, LANES), BF16),
        compiler_params=_cparams(("parallel", "parallel")),
        name="cmp",
    )(c, w1a, w1b, pa, pb, w2)


def _nsa_body(q_ref, gate_ref, kvc_ref, kvs_ref, kvw_ref, ov_ref, et_ref, band_ref, o_ref,
              m_scr, l_scr, acc_scr, part_scr, *, n_slc):
    i = pl.program_id(2)
    nc = kvc_ref.shape[0]
    hpg, tq_n, rows, tk = NSA_HPG, NSA_TQ, NSA_HPG * NSA_TQ, NSA_TK
    t0 = pl.multiple_of(i * tq_n, tq_n)
    q = q_ref[...].astype(F32)
    low = _iota((tq_n, LANES), 1) < NSA_HEAD_DIM
    heads = []
    for h in range(hpg):
        tile_ = q[:, (h // 2) * LANES:(h // 2 + 1) * LANES]
        heads.append(jnp.where(low, tile_ if h % 2 == 0 else pltpu.roll(tile_, NSA_HEAD_DIM, 1), 0.0))
    qs = jnp.concatenate(heads, axis=0).astype(BF16)

    wk = WINDOW + tq_n
    start = pl.multiple_of(jnp.maximum(t0 - WINDOW, 0), tq_n)
    kvw = kvw_ref[pl.ds(start, wk), :]
    s_w = _dot_t(qs, kvw)
    s_w = s_w + _rep_rows(band_ref[...], hpg)
    e_w = jnp.exp2(s_w - jnp.max(s_w, -1, keepdims=True))
    o_win = _dot(e_w.astype(BF16), kvw) / jnp.sum(e_w, -1, keepdims=True)

    kvc = kvc_ref[...]
    s = _dot_t(qs, kvc)
    tq = t0 + _iota((tq_n, nc), 0)
    n = _iota((tq_n, nc), 1)
    valid = (n * CMP_STRIDE + (CMP_BLOCK - 1) <= tq) & (n < nc - 1)
    s = s + _rep_rows(jnp.where(valid, 0.0, NEG), hpg)
    e = jnp.exp2(s - jnp.maximum(jnp.max(s, -1, keepdims=True), M_INIT))
    p = e / jnp.maximum(jnp.sum(e, -1, keepdims=True), 1e-30)
    o_cmp = _dot(p.astype(BF16), kvc)

    ps = p[0:tq_n] + p[tq_n:2 * tq_n] + p[2 * tq_n:3 * tq_n] + p[3 * tq_n:4 * tq_n]
    ph = ps.astype(BF16)
    plo = (ps - ph.astype(F32)).astype(BF16)
    ov = ov_ref[...]
    imp_t = (_dot(ph, ov) + _dot(plo, ov)).T

    jr = _iota((LANES, tq_n), 0)
    tl = t0 + _iota((LANES, tq_n), 1)
    cur = lax.shift_right_logical(tl, SLC_SHIFT)
    causal = jr * SLC_BLOCK <= tl
    forced = ((jr == 0) | (jr == cur) | (jr == cur - 1)) & causal
    v = jnp.where(forced, PICKED, jnp.where(causal, imp_t, -1.0))
    if n_slc < LANES:
        v = jnp.where(jr < n_slc, v, PICKED)
    jf = jr.astype(F32)
    sel_t = jnp.where(forced, 1.0, 0.0)
    n_forced = 3
    for _ in range(SLC_TOPK - n_forced):
        mx = jnp.max(v, axis=0, keepdims=True)
        idx = jnp.min(jnp.where(v == mx, jf, float(LANES)), axis=0, keepdims=True)
        hit = jf == idx
        sel_t = jnp.where(hit, 1.0, sel_t)
        v = jnp.where(hit, PICKED, v)
    sel = sel_t.T

    gt = gate_ref[...]
    gcol = lambda br: jnp.concatenate([gt[:, h * 3 + br:h * 3 + br + 1] for h in range(hpg)], axis=0)
    part_scr[...] = gcol(0) * o_cmp + gcol(2) * o_win

    upto = _iota((tq_n, LANES), 1) < lax.shift_right_logical(t0 + tq_n, SLC_SHIFT)
    lhs = jnp.concatenate([qs, _rep_rows(jnp.where((sel > 0.5) & upto, 0.0, NEG).astype(BF16), hpg)], axis=1)
    tk_shift = tk.bit_length() - 1
    n_full = lax.shift_right_logical(t0 + tq_n - 1, tk_shift)
    tail_idx = lax.shift_right_logical(t0 + tq_n - 1 - n_full * tk, tq_n.bit_length() - 1)

    m_scr[...] = jnp.full(m_scr.shape, M_INIT, F32)
    l_scr[...] = jnp.zeros(l_scr.shape, F32)
    acc_scr[...] = jnp.zeros(acc_scr.shape, F32)
    rc = ROW_CHUNK

    def tile(k, width, last):
        k0 = pl.multiple_of(k * tk, tk)
        kv = kvs_ref[pl.ds(k0, width), :]
        rhs = jnp.concatenate([kv, et_ref[pl.ds(k0, width), :]], axis=1)
        s_heads = [_dot_t(lhs[h * tq_n:(h + 1) * tq_n], rhs) for h in range(hpg)]
        for h in range(hpg):
            hs = slice(h * tq_n, (h + 1) * tq_n)
            ps, als = [], []
            for r in range(0, tq_n, rc):
                sl = slice(h * tq_n + r, h * tq_n + r + rc)
                sc = s_heads[h][r:r + rc]
                if last:
                    own = jnp.where(_iota((rc, tq_n), 1) <= r + _iota((rc, tq_n), 0), 0.0, NEG)
                    sc = sc + own if width == tq_n else jnp.concatenate(
                        [sc[:, :width - tq_n], sc[:, width - tq_n:] + own], axis=1)
                m_prev = m_scr[sl]
                m_new = jnp.maximum(m_prev, jnp.max(sc, -1, keepdims=True))
                alpha = jnp.exp2(m_prev - m_new)
                pp = jnp.exp2(sc - _rep_lanes(m_new, width // LANES))
                l_scr[sl] = alpha * l_scr[sl] + jnp.sum(pp, -1, keepdims=True)
                m_scr[sl] = m_new
                ps.append(pp.astype(BF16))
                als.append(alpha)
            acc_scr[hs] = jnp.concatenate(als, axis=0) * acc_scr[hs] + _dot(jnp.concatenate(ps, axis=0), kv)

    def body(k, carry):
        tile(k, tk, False)
        return carry

    lax.fori_loop(0, n_full, body, 0)
    for j in range(tk // tq_n):
        pl.when(tail_idx == j)(functools.partial(tile, n_full, (j + 1) * tq_n, True))
    acc = acc_scr[...]
    out = part_scr[...] + gcol(1) * (acc / l_scr[...])
    out = jnp.where(_iota((rows, LANES), 1) >= NSA_HEAD_DIM, out, 0.0)
    for hp in range(hpg // 2):
        even = out[2 * hp * tq_n:(2 * hp + 1) * tq_n]
        odd = out[(2 * hp + 1) * tq_n:(2 * hp + 2) * tq_n]
        o_ref[:, hp * LANES:(hp + 1) * LANES] = (pltpu.roll(even, NSA_HEAD_DIM, 1) + odd).astype(BF16)


def _nsa_call(qn, gates, kvcmp, kvs, kvw, ov, et, band):
    B, T, _ = qn.shape
    G = NSA_KV_GROUPS
    nc = kvcmp.shape[2]
    n_slc = T // SLC_BLOCK
    assert SLC_TOPK <= n_slc <= LANES and nc % LANES == 0 and T % NSA_TK == 0 and T >= WINDOW + NSA_TQ
    assert NSA_TK & (NSA_TK - 1) == 0 and NSA_TQ % SLC_BLOCK == 0 and NSA_TK % NSA_TQ == 0
    assert WINDOW % NSA_TQ == 0 and band.shape == (WINDOW // NSA_TQ + 1, NSA_TQ, WINDOW + NSA_TQ)
    gw = NSA_HPG * NSA_HEAD_DIM
    rows = NSA_HPG * NSA_TQ
    whole = lambda r: pl.BlockSpec((None, None, r, LANES), lambda b, g, i: (g, b, 0, 0))
    const = lambda a: pl.BlockSpec(a.shape, lambda b, g, i: (0, 0))
    return pl.pallas_call(
        functools.partial(_nsa_body, n_slc=n_slc),
        grid=(B, G, T // NSA_TQ),
        in_specs=[pl.BlockSpec((None, NSA_TQ, gw), lambda b, g, i: (b, i, g)),
                  pl.BlockSpec((None, NSA_TQ, LANES), lambda b, g, i: (b, i, g)),
                  whole(nc), whole(T), whole(T), const(ov), const(et),
                  pl.BlockSpec((None,) + band.shape[1:], lambda b, g, i: (jnp.minimum(i, band.shape[0] - 1), 0, 0))],
        out_specs=pl.BlockSpec((None, NSA_TQ, NSA_HPG * NSA_HEAD_DIM), lambda b, g, i: (b, i, g)),
        out_shape=jax.ShapeDtypeStruct((B, T, NSA_HEADS * NSA_HEAD_DIM), BF16),
        scratch_shapes=[pltpu.VMEM((rows, LANES), F32)] * 4,
        compiler_params=_cparams(("parallel", "parallel", "arbitrary")),
        name="nsa",
    )(qn, gates, kvcmp, kvs, kvw, ov, et, band)


def _mla_body(q_ref, k_ref, v_ref, o_ref, p_buf, a_buf, m_scr, l_scr, acc_scr):
    i = pl.program_id(2)
    tt, nh = MLA_T, MLA_HPS
    head = lambda hh: slice(hh * LANES, (hh + 1) * LANES)

    pair = lambda hh: slice((hh // 2) * LANES, (hh // 2 + 1) * LANES)

    def scores(kt, hh):
        k0 = pl.multiple_of(kt * tt, tt)
        return _dot_t(q_ref[:, head(hh)], k_ref[pl.ds(k0, tt), head(hh)])

    def values(kt, hh):
        return v_ref[pl.ds(pl.multiple_of(kt * tt, tt), tt), pair(hh)]

    m_scr[...] = jnp.full(m_scr.shape, M_INIT, F32)
    l_scr[...] = jnp.zeros(l_scr.shape, F32)
    acc_scr[...] = jnp.zeros(acc_scr.shape, F32)

    def stage(k, diagonal, with_prev):
        if diagonal:
            tri = jnp.where(_iota((tt, tt), 1) <= _iota((tt, tt), 0), 0.0, NEG)
        for hh in range(nh):
            s = scores(k, hh)
            if diagonal:
                s = s + tri
            if with_prev:
                acc_scr[hh] = a_buf[hh] * acc_scr[hh] + _dot(p_buf[hh], values(k - 1, hh))
            m_prev = m_scr[hh]
            m_new = jnp.maximum(m_prev, jnp.max(s, -1, keepdims=True))
            alpha = jnp.exp2(m_prev - m_new)
            p = jnp.exp2(s - _rep_lanes(m_new, tt // LANES))
            l_scr[hh] = alpha * l_scr[hh] + jnp.sum(p, -1, keepdims=True)
            m_scr[hh] = m_new
            a_buf[hh] = alpha
            p_buf[hh] = p.astype(BF16)

    def body(k, carry):
        stage(k, False, True)
        return carry

    @pl.when(i == 0)
    def _():
        stage(0, True, False)

    @pl.when(i > 0)
    def _():
        stage(0, False, False)
        lax.fori_loop(1, i, body, 0)
        stage(i, True, True)

    outs = [(a_buf[hh] * acc_scr[hh] + _dot(p_buf[hh], values(i, hh))) / l_scr[hh] for hh in range(nh)]
    first = _iota((tt, LANES), 1) < MLA_V_DIM
    for pp in range(nh // 2):
        o_ref[:, pp * LANES:(pp + 1) * LANES] = jnp.where(first, outs[2 * pp], outs[2 * pp + 1]).astype(BF16)


def _mla_call(qm, km, vm):
    B, T, _ = qm.shape
    nh = MLA_HPS
    assert T % MLA_T == 0 and MLA_HEADS % nh == 0 and nh % 2 == 0
    vw = nh * MLA_V_DIM
    return pl.pallas_call(
        _mla_body,
        grid=(B, MLA_HEADS // nh, T // MLA_T),
        in_specs=[pl.BlockSpec((None, MLA_T, nh * LANES), lambda b, p, i: (b, i, p)),
                  pl.BlockSpec((None, T, nh * LANES), lambda b, p, i: (b, 0, p)),
                  pl.BlockSpec((None, T, vw), lambda b, p, i: (b, 0, p))],
        out_specs=pl.BlockSpec((None, MLA_T, vw), lambda b, p, i: (b, i, p)),
        out_shape=jax.ShapeDtypeStruct((B, T, MLA_HEADS * MLA_V_DIM), BF16),
        scratch_shapes=[pltpu.VMEM((nh, MLA_T, MLA_T), BF16)] + [pltpu.VMEM((nh, MLA_T, LANES), F32)] * 4,
        compiler_params=_cparams(("parallel", "parallel", "arbitrary"), MLA_VMEM_LIMIT),
        name="mla",
    )(qm, km, vm)


def _out_body(x_ref, on_ref, om_ref, g_ref, wn_ref, wm_ref, lng_ref, lnb_ref, o_ref, *, alpha):
    tm = x_ref.shape[0]
    rc = min(tm, OUT_ROWS)
    ys = [_dot(on_ref[r:r + rc, :], wn_ref[...]) + _dot(om_ref[r:r + rc, :], wm_ref[...]) for r in range(0, tm, rc)]
    for c, r in enumerate(range(0, tm, rc)):
        z = alpha * x_ref[r:r + rc, :] + (1.0 + g_ref[...]) * ys[c]
        o_ref[r:r + rc, :] = _layer_norm(z, lng_ref[...], lnb_ref[...])


def _out_call(x, o_nsa, o_mla, g, w_on, w_om, lng, lnb, alpha):
    B, T, D = x.shape
    tm = min(T, 1024)
    tok = lambda w: pl.BlockSpec((None, tm, w), lambda b, i: (b, i, 0))
    full = lambda a: pl.BlockSpec(a.shape, lambda b, i: (0,) * a.ndim)
    lng, lnb = lng.reshape(1, D), lnb.reshape(1, D)
    return pl.pallas_call(
        functools.partial(_out_body, alpha=alpha),
        grid=(B, T // tm),
        in_specs=[tok(D), tok(o_nsa.shape[-1]), tok(o_mla.shape[-1]),
                  pl.BlockSpec((None, 1, D), lambda b, i: (b, 0, 0)),
                  full(w_on), full(w_om), full(lng), full(lnb)],
        out_specs=tok(D),
        out_shape=jax.ShapeDtypeStruct((B, T, D), F32),
        compiler_params=_cparams(("parallel", "parallel")),
        name="out",
    )(x, o_nsa, o_mla, g, w_on, w_om, lng, lnb)


ROPE_SETS = (((0, NSA_HEAD_DIM), NSA_ROPE_DIM),
             ((0,), NSA_ROPE_DIM),
             ((MLA_NOPE_DIM,), MLA_ROPE_DIM))
def _rope_tables(T, tm):
    inv_rows, sign_rows = [], []
    for lane0s, dim in ROPE_SETS:
        half = dim // 2
        inv = ROPE_THETA ** (-jnp.arange(half, dtype=F32) / half)
        lane_inv = jnp.zeros((LANES,), F32)
        first, second = np.zeros((LANES,), np.float32), np.zeros((LANES,), np.float32)
        for lane0 in lane0s:
            lane_inv = lane_inv.at[lane0:lane0 + half].set(inv).at[lane0 + half:lane0 + dim].set(inv)
            first[lane0:lane0 + half] = -1.0
            second[lane0 + half:lane0 + dim] = 1.0
        inv_rows.append(lane_inv)
        sign_rows += [first, second]
    inv_all = jnp.stack(inv_rows)

    def trig(pos):
        ang = pos.astype(F32)[:, None, None] * inv_all[None]
        return jnp.stack([jnp.cos(ang), jnp.sin(ang)], axis=2).reshape(pos.shape[0], -1)

    signs = jnp.broadcast_to(jnp.asarray(np.concatenate(sign_rows))[None, :], (8, 2 * len(ROPE_SETS) * LANES))
    return trig(jnp.arange(tm)), trig(jnp.arange(0, T, tm))[:, None, :], signs


def _pad_slots(w, n, width):
    K = w.shape[0]
    return jnp.pad(w.reshape(K, n, width), ((0, 0), (0, 0), (0, LANES - width))).reshape(K, n * LANES)


def _proj_weights(w_in):
    D = w_in.shape[0]
    hd, G = NSA_HEAD_DIM, NSA_KV_GROUPS
    nq, nkv = NSA_HEADS * hd, G * hd
    off = np.cumsum([0, nq] + [nkv] * 6 + [3 * NSA_HEADS, MLA_Q_RANK, MLA_KV_RANK, MLA_ROPE_DIM])
    cols = [w_in[:, :nq]]
    for pair in range(3):
        ko, vo = off[1 + 2 * pair], off[2 + 2 * pair]
        for g in range(G):
            cols += [w_in[:, ko + g * hd:ko + (g + 1) * hd], w_in[:, vo + g * hd:vo + (g + 1) * hd]]
    per_g = 3 * NSA_HPG
    for g in range(G):
        cols.append(jnp.pad(w_in[:, off[7] + g * per_g:off[7] + (g + 1) * per_g], ((0, 0), (0, LANES - per_g))))
    cols += [w_in[:, off[8]:off[9]], w_in[:, off[9]:off[10]]]
    cols.append(jnp.pad(w_in[:, off[10]:off[11]],
                        ((0, 0), (MLA_NOPE_DIM, LANES - MLA_NOPE_DIM - MLA_ROPE_DIM))))
    w_all = jnp.concatenate(cols, axis=1).astype(BF16)
    assert w_all.shape == (D, PROJ_W)
    return w_all


def _cmp_weights(k_pos, k_w1, k_w2, v_pos, v_w1, v_w2):
    hd, st, hid = NSA_HEAD_DIM, CMP_STRIDE, CMP_HIDDEN

    def half(lo):
        w = jnp.zeros((st, LANES, 2 * hid), F32)
        w = w.at[:, :hd, :hid].set(k_w1[lo * hd:(lo + st) * hd].reshape(st, hd, hid))
        w = w.at[:, hd:, hid:].set(v_w1[lo * hd:(lo + st) * hd].reshape(st, hd, hid))
        pos = jnp.concatenate([k_pos[lo:lo + st], v_pos[lo:lo + st]], axis=1).reshape(1, st * LANES)
        return w.reshape(st * LANES, 2 * hid).astype(BF16), jnp.broadcast_to(pos, (8, st * LANES)).astype(BF16)

    w1a, pa = half(0)
    w1b, pb = half(st)
    w2 = jnp.zeros((2 * hid, LANES), F32).at[:hid, :hd].set(k_w2).at[hid:, hd:].set(v_w2).astype(BF16)
    return w1a, w1b, pa, pb, w2


def _block_tables(T):
    nc, n_slc = T // CMP_STRIDE, T // SLC_BLOCK
    cs = np.arange(nc)[:, None] * CMP_STRIDE
    blk = np.arange(LANES)[None, :]
    ov = (cs < blk * SLC_BLOCK + SLC_BLOCK) & (cs + CMP_BLOCK > blk * SLC_BLOCK) & (np.arange(nc)[:, None] < nc - 1) & (blk < n_slc)
    et = (np.arange(T)[:, None] // SLC_BLOCK) == blk
    off = np.arange(WINDOW // NSA_TQ + 1)[:, None, None] * NSA_TQ
    diff = off + np.arange(NSA_TQ)[None, :, None] - np.arange(WINDOW + NSA_TQ)[None, None, :]
    band = np.where((diff >= 0) & (diff < WINDOW), 0.0, NEG).astype(np.float32)
    return jnp.asarray(ov.astype(np.float32), BF16), jnp.asarray(et.astype(np.float32), BF16), jnp.asarray(band)


def kernel(x, c, w_ada, b_ada, ln_g, ln_b, ffn1_wg, ffn1_wu, ffn1_wd, w_in, cmp_k_pos, cmp_k_w1, cmp_k_w2,
           cmp_v_pos, cmp_v_w1, cmp_v_w2, mla_q_norm, mla_kv_norm, mla_w_uq, mla_w_ukv, w_out,
           ffn2_wg, ffn2_wu, ffn2_wd):
    B, T, D = x.shape
    depth = w_ada.shape[0]
    alpha = (2.0 * depth) ** 0.25
    ov, et, band = _block_tables(T)
    for l in range(depth):
        mod = _mod_call(c, w_ada[l], b_ada[l])
        sh1, sc1, g1, sh2, sc2, g2, sh3, sc3, g3 = [mod[:, k] for k in range(N_MOD)]
        x = _ffn_call(x, sh1, sc1, g1, ffn1_wg[l], ffn1_wu[l], ffn1_wd[l],
                      ln_g[l, 0], ln_b[l, 0], alpha, 0.5)

        wuq = _pad_slots(mla_w_uq[l], MLA_HEADS, MLA_NOPE_DIM + MLA_ROPE_DIM).astype(BF16)
        wukv = mla_w_ukv[l].reshape(MLA_KV_RANK, MLA_HEADS, MLA_NOPE_DIM + MLA_V_DIM)
        wuk = _pad_slots(wukv[:, :, :MLA_NOPE_DIM].reshape(MLA_KV_RANK, -1), MLA_HEADS, MLA_NOPE_DIM).astype(BF16)
        wuv = wukv[:, :, MLA_NOPE_DIM:].reshape(MLA_KV_RANK, -1).astype(BF16)
        qn, kvc, kvs, kvw, gates, qm, km, vm = _proj_call(
            x, sh2, sc2, _proj_weights(w_in[l]), mla_q_norm[l].reshape(1, -1), mla_kv_norm[l].reshape(1, -1),
            wuq, wuk, wuv)
        kvcmp = _cmp_call(kvc, *_cmp_weights(cmp_k_pos[l], cmp_k_w1[l], cmp_k_w2[l],
                                             cmp_v_pos[l], cmp_v_w1[l], cmp_v_w2[l]))
        o_nsa = _nsa_call(qn, gates, kvcmp, kvs, kvw, ov, et, band)
        o_mla = _mla_call(qm, km, vm)
        n_nsa = NSA_HEADS * NSA_HEAD_DIM
        w_o = w_out[l].astype(BF16)
        x = _out_call(x, o_nsa, o_mla, g2, w_o[:n_nsa], w_o[n_nsa:], ln_g[l, 1], ln_b[l, 1], alpha)

        x = _ffn_call(x, sh3, sc3, g3, ffn2_wg[l], ffn2_wu[l], ffn2_wd[l],
                      ln_g[l, 2], ln_b[l, 2], alpha, 0.5)
    return x
```

```python
import functools

import numpy as np
import jax
import jax.numpy as jnp
from jax import lax
from jax.experimental import pallas as pl
from jax.experimental.pallas import tpu as pltpu

F32 = jnp.float32
BF16 = jnp.bfloat16

ROPE_THETA = 500000.0
NSA_HEADS = 8
NSA_KV_GROUPS = 2
NSA_HPG = NSA_HEADS // NSA_KV_GROUPS
NSA_HEAD_DIM = 64
NSA_ROPE_DIM = 16
CMP_BLOCK = 32
CMP_STRIDE = 16
CMP_HIDDEN = 256
SLC_BLOCK = 64
SLC_TOPK = 16
SLC_SHIFT = SLC_BLOCK.bit_length() - 1
WINDOW = 512
FORCE_SCORE = 1e9
MLA_HEADS = 8
MLA_NOPE_DIM = 64
MLA_ROPE_DIM = 32
MLA_V_DIM = 64
MLA_Q_RANK = 384
MLA_KV_RANK = 256
EPS = 1e-5
N_MOD = 9

LANES = 128
NEG = -1e30
M_INIT = -1e29
PICKED = -3e38
ROW_CHUNK = 128
LOG2E = 1.4426950408889634
VMEM_LIMIT = 48 * 1024 * 1024
MLA_VMEM_LIMIT = 56 * 1024 * 1024

NSA_TQ = 256
NSA_TK = 1024
MLA_T = 512
MLA_HPS = 4
OUT_ROWS = 256
PROJ_ROWS = 256

OFF_Q, OFF_KV, OFF_GATE, OFF_CQ, OFF_CKV, OFF_KR, PROJ_W = np.cumsum(
    [0, NSA_HEADS * NSA_HEAD_DIM, 3 * NSA_KV_GROUPS * LANES, NSA_KV_GROUPS * LANES, MLA_Q_RANK, MLA_KV_RANK,
     LANES]).tolist()


def _cparams(sem, vmem=VMEM_LIMIT):
    return pltpu.CompilerParams(dimension_semantics=sem, vmem_limit_bytes=vmem)


def _layer_norm(z, g, b):
    mu = jnp.mean(z, -1, keepdims=True)
    zc = z - mu
    var = jnp.mean(zc * zc, -1, keepdims=True)
    return zc * lax.rsqrt(var + EPS) * g + b


def _dot(a, b):
    return jnp.dot(a, b, preferred_element_type=F32)


def _dot_t(a, b):
    return lax.dot_general(a, b, (((1,), (1,)), ((), ())), preferred_element_type=F32)


def _rep_rows(a, n):
    return jnp.concatenate([a] * n, axis=0)


def _rep_lanes(a, n):
    return jnp.concatenate([a] * n, axis=1)


def _iota(shape, d):
    return lax.broadcasted_iota(jnp.int32, shape, d)


def _mod_body(c_ref, w_ref, b_ref, o_ref):
    c = c_ref[...]
    s = (c * jax.nn.sigmoid(c)).astype(BF16)
    o_ref[...] = _dot(s, w_ref[...].astype(BF16)) + b_ref[...]


def _mod_call(c, w_ada, b_ada):
    B, D = c.shape
    rows = 8
    c8 = jnp.zeros((rows, D), F32).at[:B].set(c)
    out = pl.pallas_call(
        _mod_body,
        grid=(N_MOD,),
        in_specs=[pl.BlockSpec((rows, D), lambda j: (0, 0)),
                  pl.BlockSpec((D, D), lambda j: (0, j)),
                  pl.BlockSpec((1, D), lambda j: (0, j))],
        out_specs=pl.BlockSpec((rows, D), lambda j: (0, j)),
        out_shape=jax.ShapeDtypeStruct((rows, N_MOD * D), F32),
        compiler_params=_cparams(("parallel",)),
        name="mod",
    )(c8, w_ada, b_ada.reshape(1, N_MOD * D))
    return out[:B].reshape(B, N_MOD, 1, D)


def _ffn_body(x_ref, sh_ref, sc_ref, g_ref, wg_ref, wu_ref, wd_ref, lng_ref, lnb_ref, o_ref,
              u_scr, acc_scr, *, alpha, coef, n_ff):
    j = pl.program_id(2)

    @pl.when(j == 0)
    def _():
        u_scr[...] = (x_ref[...] * (1.0 + sc_ref[...]) + sh_ref[...]).astype(BF16)
        acc_scr[...] = jnp.zeros_like(acc_scr)

    u = u_scr[...]
    a = _dot(u, wg_ref[...].astype(BF16))
    b = _dot(u, wu_ref[...].astype(BF16))
    h = (a * jax.nn.sigmoid(a)) * b
    acc_scr[...] += _dot(h.astype(BF16), wd_ref[...].astype(BF16))

    @pl.when(j == n_ff - 1)
    def _():
        z = alpha * x_ref[...] + coef * (1.0 + g_ref[...]) * acc_scr[...]
        o_ref[...] = _layer_norm(z, lng_ref[...], lnb_ref[...])


def _ffn_call(x, sh, sc, g, wg, wu, wd, lng, lnb, alpha, coef):
    B, T, D = x.shape
    FF = wg.shape[1]
    tm = min(T, 1024)
    tf = 256
    assert T % tm == 0 and FF % tf == 0
    n_ff = FF // tf
    row = pl.BlockSpec((None, 1, D), lambda b, i, j: (b, 0, 0))
    vec = pl.BlockSpec((1, D), lambda b, i, j: (0, 0))
    return pl.pallas_call(
        functools.partial(_ffn_body, alpha=alpha, coef=coef, n_ff=n_ff),
        grid=(B, T // tm, n_ff),
        in_specs=[pl.BlockSpec((None, tm, D), lambda b, i, j: (b, i, 0)), row, row, row,
                  pl.BlockSpec((D, tf), lambda b, i, j: (0, j)),
                  pl.BlockSpec((D, tf), lambda b, i, j: (0, j)),
                  pl.BlockSpec((tf, D), lambda b, i, j: (j, 0)), vec, vec],
        out_specs=pl.BlockSpec((None, tm, D), lambda b, i, j: (b, i, 0)),
        out_shape=jax.ShapeDtypeStruct((B, T, D), F32),
        scratch_shapes=[pltpu.VMEM((tm, D), BF16), pltpu.VMEM((tm, D), F32)],
        compiler_params=_cparams(("parallel", "parallel", "arbitrary")),
        name="ffn",
    )(x, sh, sc, g, wg, wu, wd, lng.reshape(1, D), lnb.reshape(1, D))


def _rope_slot(x, c, sa, sb, half):
    return x * c + pltpu.roll(x, LANES - half, 1) * sa + pltpu.roll(x, half, 1) * sb


def _rms_norm(x, g):
    return x * lax.rsqrt(jnp.mean(x * x, -1, keepdims=True) + EPS) * g


def _proj_body(x_ref, sh_ref, sc_ref, w_ref, gq_ref, gkv_ref, wuq_ref, wuk_ref, wuv_ref,
               base_ref, tile_ref, sgn_ref,
               qn_ref, kvc_ref, kvs_ref, kvw_ref, gate_ref, qm_ref, km_ref, vm_ref,
               *, nsa_scale, mla_scale):
    tm = x_ref.shape[0]
    rc = min(tm, PROJ_ROWS)
    starts = range(0, tm, rc)
    nh = NSA_ROPE_DIM // 2
    mh = MLA_ROPE_DIM // 2
    hs = [_dot((x_ref[r:r + rc, :] * (1.0 + sc_ref[...]) + sh_ref[...]).astype(BF16), w_ref[...])
          for r in starts]
    for c, r in enumerate(starts):
        h, rows = hs[c], slice(r, r + rc)
        tabs = []
        for st in range(len(ROPE_SETS)):
            lc, ls = slice(2 * st * LANES, (2 * st + 1) * LANES), slice((2 * st + 1) * LANES, (2 * st + 2) * LANES)
            cb, sb_, ca, sa_ = base_ref[rows, lc], base_ref[rows, ls], tile_ref[:, lc], tile_ref[:, ls]
            sin = sa_ * cb + ca * sb_
            tabs += [ca * cb - sa_ * sb_, sin * sgn_ref[0:1, lc], sin * sgn_ref[0:1, ls]]
        cq, sqa, sqb, cn, sna, snb, cm, sma, smb = tabs

        for s in range(NSA_HEADS * NSA_HEAD_DIM // LANES):
            lo = OFF_Q + s * LANES
            qn_ref[rows, s * LANES:(s + 1) * LANES] = (
                _rope_slot(h[:, lo:lo + LANES], cq, sqa, sqb, nh) * nsa_scale).astype(BF16)
        for idx, ref in enumerate((kvc_ref, kvs_ref, kvw_ref)):
            for g in range(NSA_KV_GROUPS):
                lo = OFF_KV + (idx * NSA_KV_GROUPS + g) * LANES
                ref[g, rows, :] = _rope_slot(h[:, lo:lo + LANES], cn, sna, snb, nh).astype(BF16)
        gate_ref[rows, :] = jax.nn.sigmoid(h[:, OFF_GATE:OFF_GATE + NSA_KV_GROUPS * LANES])

        cqn = _rms_norm(h[:, OFF_CQ:OFF_CQ + MLA_Q_RANK], gq_ref[...]).astype(BF16)
        qm = _dot(cqn, wuq_ref[...])
        ckvn = _rms_norm(h[:, OFF_CKV:OFF_CKV + MLA_KV_RANK], gkv_ref[...]).astype(BF16)
        kn = _dot(ckvn, wuk_ref[...])
        vm_ref[rows, :] = _dot(ckvn, wuv_ref[...]).astype(BF16)
        kr = _rope_slot(h[:, OFF_KR:OFF_KR + LANES], cm, sma, smb, mh)
        for s in range(MLA_HEADS):
            sl = slice(s * LANES, (s + 1) * LANES)
            qm_ref[rows, sl] = (_rope_slot(qm[:, sl], cm, sma, smb, mh) * mla_scale).astype(BF16)
            km_ref[rows, sl] = (kn[:, sl] + kr).astype(BF16)


def _proj_call(x, sh, sc, w_all, gq, gkv, wuq, wuk, wuv):
    B, T, D = x.shape
    tm = min(T, 1024)
    G = NSA_KV_GROUPS
    base, tile_, signs = _rope_tables(T, tm)
    row = pl.BlockSpec((None, 1, D), lambda b, i: (b, 0, 0))

    def full(a):
        return pl.BlockSpec(a.shape, lambda b, i: (0,) * a.ndim)

    tile_spec = pl.BlockSpec((None, 1, tile_.shape[-1]), lambda b, i: (i, 0, 0))
    tok = lambda w: pl.BlockSpec((None, tm, w), lambda b, i: (b, i, 0))
    grp = pl.BlockSpec((G, None, tm, LANES), lambda b, i: (0, b, i, 0))
    sd = jax.ShapeDtypeStruct
    return pl.pallas_call(
        functools.partial(_proj_body, nsa_scale=NSA_HEAD_DIM ** -0.5 * LOG2E,
                          mla_scale=(MLA_NOPE_DIM + MLA_ROPE_DIM) ** -0.5 * LOG2E),
        grid=(B, T // tm),
        in_specs=[tok(D), row, row, full(w_all), full(gq), full(gkv), full(wuq), full(wuk), full(wuv)]
                 + [full(base), tile_spec, full(signs)],
        out_specs=[tok(NSA_HEADS * NSA_HEAD_DIM), grp, grp, grp, tok(G * LANES),
                   tok(MLA_HEADS * LANES), tok(MLA_HEADS * LANES), tok(MLA_HEADS * MLA_V_DIM)],
        out_shape=[sd((B, T, NSA_HEADS * NSA_HEAD_DIM), BF16), sd((G, B, T, LANES), BF16),
                   sd((G, B, T, LANES), BF16), sd((G, B, T, LANES), BF16), sd((B, T, G * LANES), F32),
                   sd((B, T, MLA_HEADS * LANES), BF16), sd((B, T, MLA_HEADS * LANES), BF16),
                   sd((B, T, MLA_HEADS * MLA_V_DIM), BF16)],
        compiler_params=_cparams(("parallel", "parallel")),
        name="proj",
    )(x, sh, sc, w_all, gq, gkv, wuq, wuk, wuv, base, tile_, signs)


def _gelu_tanh(x):
    return x * (0.5 * (1.0 + jnp.tanh(0.7978845608028654 * (x + 0.044715 * (x * x * x)))))


def _cmp_body(c_ref, w1a_ref, w1b_ref, pa_ref, pb_ref, w2_ref, o_ref):
    c = c_ref[...]
    nc = c.shape[0]
    w1a, w1b = w1a_ref[...], w1b_ref[...]
    a = _dot(c, w1a)
    b = _dot(c, w1b)
    bias = _dot(pa_ref[...], w1a) + _dot(pb_ref[...], w1b)
    pre = a + pltpu.roll(b, nc - 1, 0) + bias[0:1]
    o_ref[...] = _dot(_gelu_tanh(pre).astype(BF16), w2_ref[...]).astype(BF16)


def _cmp_call(kvc, w1a, w1b, pa, pb, w2):
    G, B, T, _ = kvc.shape
    nc = T // CMP_STRIDE
    c = kvc.reshape(G, B, nc, CMP_STRIDE * LANES)

    def full(a):
        return pl.BlockSpec(a.shape, lambda g, b: (0,) * a.ndim)

    return pl.pallas_call(
        _cmp_body,
        grid=(G, B),
        in_specs=[pl.BlockSpec((None, None, nc, CMP_STRIDE * LANES), lambda g, b: (g, b, 0, 0)),
                  full(w1a), full(w1b), full(pa), full(pb), full(w2)],
        out_specs=pl.BlockSpec((None, None, nc, LANES), lambda g, b: (g, b, 0, 0)),
        out_shape=jax.ShapeDtypeStruct((G, B, nc, LANES), BF16),
        compiler_params=_cparams(("parallel", "parallel")),
        name="cmp",
    )(c, w1a, w1b, pa, pb, w2)


def _nsa_body(q_ref, gate_ref, kvc_ref, kvs_ref, kvw_ref, ov_ref, et_ref, band_ref, o_ref,
              m_scr, l_scr, acc_scr, part_scr, *, n_slc):
    i = pl.program_id(2)
    nc = kvc_ref.shape[0]
    hpg, tq_n, rows, tk = NSA_HPG, NSA_TQ, NSA_HPG * NSA_TQ, NSA_TK
    t0 = pl.multiple_of(i * tq_n, tq_n)
    q = q_ref[...].astype(F32)
    low = _iota((tq_n, LANES), 1) < NSA_HEAD_DIM
    heads = []
    for h in range(hpg):
        tile_ = q[:, (h // 2) * LANES:(h // 2 + 1) * LANES]
        heads.append(jnp.where(low, tile_ if h % 2 == 0 else pltpu.roll(tile_, NSA_HEAD_DIM, 1), 0.0))
    qs = jnp.concatenate(heads, axis=0).astype(BF16)

    wk = WINDOW + tq_n
    start = pl.multiple_of(jnp.maximum(t0 - WINDOW, 0), tq_n)
    kvw = kvw_ref[pl.ds(start, wk), :]
    s_w = _dot_t(qs, kvw)
    s_w = s_w + _rep_rows(band_ref[...], hpg)
    e_w = jnp.exp2(s_w - jnp.max(s_w, -1, keepdims=True))
    o_win = _dot(e_w.astype(BF16), kvw) / jnp.sum(e_w, -1, keepdims=True)

    kvc = kvc_ref[...]
    s = _dot_t(qs, kvc)
    tq = t0 + _iota((tq_n, nc), 0)
    n = _iota((tq_n, nc), 1)
    valid = (n * CMP_STRIDE + (CMP_BLOCK - 1) <= tq) & (n < nc - 1)
    s = s + _rep_rows(jnp.where(valid, 0.0, NEG), hpg)
    e = jnp.exp2(s - jnp.maximum(jnp.max(s, -1, keepdims=True), M_INIT))
    p = e / jnp.maximum(jnp.sum(e, -1, keepdims=True), 1e-30)
    o_cmp = _dot(p.astype(BF16), kvc)

    ps = p[0:tq_n] + p[tq_n:2 * tq_n] + p[2 * tq_n:3 * tq_n] + p[3 * tq_n:4 * tq_n]
    ph = ps.astype(BF16)
    plo = (ps - ph.astype(F32)).astype(BF16)
    ov = ov_ref[...]
    imp_t = (_dot(ph, ov) + _dot(plo, ov)).T

    jr = _iota((LANES, tq_n), 0)
    tl = t0 + _iota((LANES, tq_n), 1)
    cur = lax.shift_right_logical(tl, SLC_SHIFT)
    causal = jr * SLC_BLOCK <= tl
    forced = ((jr == 0) | (jr == cur) | (jr == cur - 1)) & causal
    v = jnp.where(forced, PICKED, jnp.where(causal, imp_t, -1.0))
    if n_slc < LANES:
        v = jnp.where(jr < n_slc, v, PICKED)
    jf = jr.astype(F32)
    sel_t = jnp.where(forced, 1.0, 0.0)
    n_forced = 3
    for _ in range(SLC_TOPK - n_forced):
        mx = jnp.max(v, axis=0, keepdims=True)
        idx = jnp.min(jnp.where(v == mx, jf, float(LANES)), axis=0, keepdims=True)
        hit = jf == idx
        sel_t = jnp.where(hit, 1.0, sel_t)
        v = jnp.where(hit, PICKED, v)
    sel = sel_t.T

    gt = gate_ref[...]
    gcol = lambda br: jnp.concatenate([gt[:, h * 3 + br:h * 3 + br + 1] for h in range(hpg)], axis=0)
    part_scr[...] = gcol(0) * o_cmp + gcol(2) * o_win

    upto = _iota((tq_n, LANES), 1) < lax.shift_right_logical(t0 + tq_n, SLC_SHIFT)
    lhs = jnp.concatenate([qs, _rep_rows(jnp.where((sel > 0.5) & upto, 0.0, NEG).astype(BF16), hpg)], axis=1)
    tk_shift = tk.bit_length() - 1
    n_full = lax.shift_right_logical(t0 + tq_n - 1, tk_shift)
    tail_idx = lax.shift_right_logical(t0 + tq_n - 1 - n_full * tk, tq_n.bit_length() - 1)

    m_scr[...] = jnp.full(m_scr.shape, M_INIT, F32)
    l_scr[...] = jnp.zeros(l_scr.shape, F32)
    acc_scr[...] = jnp.zeros(acc_scr.shape, F32)
    rc = ROW_CHUNK

    def tile(k, width, last):
        k0 = pl.multiple_of(k * tk, tk)
        kv = kvs_ref[pl.ds(k0, width), :]
        rhs = jnp.concatenate([kv, et_ref[pl.ds(k0, width), :]], axis=1)
        s_heads = [_dot_t(lhs[h * tq_n:(h + 1) * tq_n], rhs) for h in range(hpg)]
        for h in range(hpg):
            hs = slice(h * tq_n, (h + 1) * tq_n)
            ps, als = [], []
            for r in range(0, tq_n, rc):
                sl = slice(h * tq_n + r, h * tq_n + r + rc)
                sc = s_heads[h][r:r + rc]
                if last:
                    own = jnp.where(_iota((rc, tq_n), 1) <= r + _iota((rc, tq_n), 0), 0.0, NEG)
                    sc = sc + own if width == tq_n else jnp.concatenate(
                        [sc[:, :width - tq_n], sc[:, width - tq_n:] + own], axis=1)
                m_prev = m_scr[sl]
                m_new = jnp.maximum(m_prev, jnp.max(sc, -1, keepdims=True))
                alpha = jnp.exp2(m_prev - m_new)
                pp = jnp.exp2(sc - _rep_lanes(m_new, width // LANES))
                l_scr[sl] = alpha * l_scr[sl] + jnp.sum(pp, -1, keepdims=True)
                m_scr[sl] = m_new
                ps.append(pp.astype(BF16))
                als.append(alpha)
            acc_scr[hs] = jnp.concatenate(als, axis=0) * acc_scr[hs] + _dot(jnp.concatenate(ps, axis=0), kv)

    def body(k, carry):
        tile(k, tk, False)
        return carry

    lax.fori_loop(0, n_full, body, 0)
    for j in range(tk // tq_n):
        pl.when(tail_idx == j)(functools.partial(tile, n_full, (j + 1) * tq_n, True))
    acc = acc_scr[...]
    out = part_scr[...] + gcol(1) * (acc / l_scr[...])
    out = jnp.where(_iota((rows, LANES), 1) >= NSA_HEAD_DIM, out, 0.0)
    for hp in range(hpg // 2):
        even = out[2 * hp * tq_n:(2 * hp + 1) * tq_n]
        odd = out[(2 * hp + 1) * tq_n:(2 * hp + 2) * tq_n]
        o_ref[:, hp * LANES:(hp + 1) * LANES] = (pltpu.roll(even, NSA_HEAD_DIM, 1) + odd).astype(BF16)


def _nsa_call(qn, gates, kvcmp, kvs, kvw, ov, et, band):
    B, T, _ = qn.shape
    G = NSA_KV_GROUPS
    nc = kvcmp.shape[2]
    n_slc = T // SLC_BLOCK
    assert SLC_TOPK <= n_slc <= LANES and nc % LANES == 0 and T % NSA_TK == 0 and T >= WINDOW + NSA_TQ
    assert NSA_TK & (NSA_TK - 1) == 0 and NSA_TQ % SLC_BLOCK == 0 and NSA_TK % NSA_TQ == 0
    assert WINDOW % NSA_TQ == 0 and band.shape == (WINDOW // NSA_TQ + 1, NSA_TQ, WINDOW + NSA_TQ)
    gw = NSA_HPG * NSA_HEAD_DIM
    rows = NSA_HPG * NSA_TQ
    whole = lambda r: pl.BlockSpec((None, None, r, LANES), lambda b, g, i: (g, b, 0, 0))
    const = lambda a: pl.BlockSpec(a.shape, lambda b, g, i: (0, 0))
    return pl.pallas_call(
        functools.partial(_nsa_body, n_slc=n_slc),
        grid=(B, G, T // NSA_TQ),
        in_specs=[pl.BlockSpec((None, NSA_TQ, gw), lambda b, g, i: (b, i, g)),
                  pl.BlockSpec((None, NSA_TQ, LANES), lambda b, g, i: (b, i, g)),
                  whole(nc), whole(T), whole(T), const(ov), const(et),
                  pl.BlockSpec((None,) + band.shape[1:], lambda b, g, i: (jnp.minimum(i, band.shape[0] - 1), 0, 0))],
        out_specs=pl.BlockSpec((None, NSA_TQ, NSA_HPG * NSA_HEAD_DIM), lambda b, g, i: (b, i, g)),
        out_shape=jax.ShapeDtypeStruct((B, T, NSA_HEADS * NSA_HEAD_DIM), BF16),
        scratch_shapes=[pltpu.VMEM((rows, LANES), F32)] * 4,
        compiler_params=_cparams(("parallel", "parallel", "arbitrary")),
        name="nsa",
    )(qn, gates, kvcmp, kvs, kvw, ov, et, band)


def _mla_body(q_ref, k_ref, v_ref, o_ref, p_buf, a_buf, m_scr, l_scr, acc_scr):
    i = pl.program_id(2)
    tt, nh = MLA_T, MLA_HPS
    head = lambda hh: slice(hh * LANES, (hh + 1) * LANES)

    pair = lambda hh: slice((hh // 2) * LANES, (hh // 2 + 1) * LANES)

    def scores(kt, hh):
        k0 = pl.multiple_of(kt * tt, tt)
        return _dot_t(q_ref[:, head(hh)], k_ref[pl.ds(k0, tt), head(hh)])

    def values(kt, hh):
        return v_ref[pl.ds(pl.multiple_of(kt * tt, tt), tt), pair(hh)]

    m_scr[...] = jnp.full(m_scr.shape, M_INIT, F32)
    l_scr[...] = jnp.zeros(l_scr.shape, F32)
    acc_scr[...] = jnp.zeros(acc_scr.shape, F32)

    def stage(k, diagonal, with_prev):
        if diagonal:
            tri = jnp.where(_iota((tt, tt), 1) <= _iota((tt, tt), 0), 0.0, NEG)
        for hh in range(nh):
            s = scores(k, hh)
            if diagonal:
                s = s + tri
            if with_prev:
                acc_scr[hh] = a_buf[hh] * acc_scr[hh] + _dot(p_buf[hh], values(k - 1, hh))
            m_prev = m_scr[hh]
            m_new = jnp.maximum(m_prev, jnp.max(s, -1, keepdims=True))
            alpha = jnp.exp2(m_prev - m_new)
            p = jnp.exp2(s - _rep_lanes(m_new, tt // LANES))
            l_scr[hh] = alpha * l_scr[hh] + jnp.sum(p, -1, keepdims=True)
            m_scr[hh] = m_new
            a_buf[hh] = alpha
            p_buf[hh] = p.astype(BF16)

    def body(k, carry):
        stage(k, False, True)
        return carry

    @pl.when(i == 0)
    def _():
        stage(0, True, False)

    @pl.when(i > 0)
    def _():
        stage(0, False, False)
        lax.fori_loop(1, i, body, 0)
        stage(i, True, True)

    outs = [(a_buf[hh] * acc_scr[hh] + _dot(p_buf[hh], values(i, hh))) / l_scr[hh] for hh in range(nh)]
    first = _iota((tt, LANES), 1) < MLA_V_DIM
    for pp in range(nh // 2):
        o_ref[:, pp * LANES:(pp + 1) * LANES] = jnp.where(first, outs[2 * pp], outs[2 * pp + 1]).astype(BF16)


def _mla_call(qm, km, vm):
    B, T, _ = qm.shape
    nh = MLA_HPS
    assert T % MLA_T == 0 and MLA_HEADS % nh == 0 and nh % 2 == 0
    vw = nh * MLA_V_DIM
    return pl.pallas_call(
        _mla_body,
        grid=(B, MLA_HEADS // nh, T // MLA_T),
        in_specs=[pl.BlockSpec((None, MLA_T, nh * LANES), lambda b, p, i: (b, i, p)),
                  pl.BlockSpec((None, T, nh * LANES), lambda b, p, i: (b, 0, p)),
                  pl.BlockSpec((None, T, vw), lambda b, p, i: (b, 0, p))],
        out_specs=pl.BlockSpec((None, MLA_T, vw), lambda b, p, i: (b, i, p)),
        out_shape=jax.ShapeDtypeStruct((B, T, MLA_HEADS * MLA_V_DIM), BF16),
        scratch_shapes=[pltpu.VMEM((nh, MLA_T, MLA_T), BF16)] + [pltpu.VMEM((nh, MLA_T, LANES), F32)] * 4,
        compiler_params=_cparams(("parallel", "parallel", "arbitrary"), MLA_VMEM_LIMIT),
        name="mla",
    )(qm, km, vm)


def _out_body(x_ref, on_ref, om_ref, g_ref, wn_ref, wm_ref, lng_ref, lnb_ref, o_ref, *, alpha):
    tm = x_ref.shape[0]
    rc = min(tm, OUT_ROWS)
    ys = [_dot(on_ref[r:r + rc, :], wn_ref[...]) + _dot(om_ref[r:r + rc, :], wm_ref[...]) for r in range(0, tm, rc)]
    for c, r in enumerate(range(0, tm, rc)):
        z = alpha * x_ref[r:r + rc, :] + (1.0 + g_ref[...]) * ys[c]
        o_ref[r:r + rc, :] = _layer_norm(z, lng_ref[...], lnb_ref[...])


def _out_call(x, o_nsa, o_mla, g, w_on, w_om, lng, lnb, alpha):
    B, T, D = x.shape
    tm = min(T, 1024)
    tok = lambda w: pl.BlockSpec((None, tm, w), lambda b, i: (b, i, 0))
    full = lambda a: pl.BlockSpec(a.shape, lambda b, i: (0,) * a.ndim)
    lng, lnb = lng.reshape(1, D), lnb.reshape(1, D)
    return pl.pallas_call(
        functools.partial(_out_body, alpha=alpha),
        grid=(B, T // tm),
        in_specs=[tok(D), tok(o_nsa.shape[-1]), tok(o_mla.shape[-1]),
                  pl.BlockSpec((None, 1, D), lambda b, i: (b, 0, 0)),
                  full(w_on), full(w_om), full(lng), full(lnb)],
        out_specs=tok(D),
        out_shape=jax.ShapeDtypeStruct((B, T, D), F32),
        compiler_params=_cparams(("parallel", "parallel")),
        name="out",
    )(x, o_nsa, o_mla, g, w_on, w_om, lng, lnb)


ROPE_SETS = (((0, NSA_HEAD_DIM), NSA_ROPE_DIM),
             ((0,), NSA_ROPE_DIM),
             ((MLA_NOPE_DIM,), MLA_ROPE_DIM))
def _rope_tables(T, tm):
    inv_rows, sign_rows = [], []
    for lane0s, dim in ROPE_SETS:
        half = dim // 2
        inv = ROPE_THETA ** (-jnp.arange(half, dtype=F32) / half)
        lane_inv = jnp.zeros((LANES,), F32)
        first, second = np.zeros((LANES,), np.float32), np.zeros((LANES,), np.float32)
        for lane0 in lane0s:
            lane_inv = lane_inv.at[lane0:lane0 + half].set(inv).at[lane0 + half:lane0 + dim].set(inv)
            first[lane0:lane0 + half] = -1.0
            second[lane0 + half:lane0 + dim] = 1.0
        inv_rows.append(lane_inv)
        sign_rows += [first, second]
    inv_all = jnp.stack(inv_rows)

    def trig(pos):
        ang = pos.astype(F32)[:, None, None] * inv_all[None]
        return jnp.stack([jnp.cos(ang), jnp.sin(ang)], axis=2).reshape(pos.shape[0], -1)

    signs = jnp.broadcast_to(jnp.asarray(np.concatenate(sign_rows))[None, :], (8, 2 * len(ROPE_SETS) * LANES))
    return trig(jnp.arange(tm)), trig(jnp.arange(0, T, tm))[:, None, :], signs


def _pad_slots(w, n, width):
    K = w.shape[0]
    return jnp.pad(w.reshape(K, n, width), ((0, 0), (0, 0), (0, LANES - width))).reshape(K, n * LANES)


def _proj_weights(w_in):
    D = w_in.shape[0]
    hd, G = NSA_HEAD_DIM, NSA_KV_GROUPS
    nq, nkv = NSA_HEADS * hd, G * hd
    off = np.cumsum([0, nq] + [nkv] * 6 + [3 * NSA_HEADS, MLA_Q_RANK, MLA_KV_RANK, MLA_ROPE_DIM])
    cols = [w_in[:, :nq]]
    for pair in range(3):
        ko, vo = off[1 + 2 * pair], off[2 + 2 * pair]
        for g in range(G):
            cols += [w_in[:, ko + g * hd:ko + (g + 1) * hd], w_in[:, vo + g * hd:vo + (g + 1) * hd]]
    per_g = 3 * NSA_HPG
    for g in range(G):
        cols.append(jnp.pad(w_in[:, off[7] + g * per_g:off[7] + (g + 1) * per_g], ((0, 0), (0, LANES - per_g))))
    cols += [w_in[:, off[8]:off[9]], w_in[:, off[9]:off[10]]]
    cols.append(jnp.pad(w_in[:, off[10]:off[11]],
                        ((0, 0), (MLA_NOPE_DIM, LANES - MLA_NOPE_DIM - MLA_ROPE_DIM))))
    w_all = jnp.concatenate(cols, axis=1).astype(BF16)
    assert w_all.shape == (D, PROJ_W)
    return w_all


def _cmp_weights(k_pos, k_w1, k_w2, v_pos, v_w1, v_w2):
    hd, st, hid = NSA_HEAD_DIM, CMP_STRIDE, CMP_HIDDEN

    def half(lo):
        w = jnp.zeros((st, LANES, 2 * hid), F32)
        w = w.at[:, :hd, :hid].set(k_w1[lo * hd:(lo + st) * hd].reshape(st, hd, hid))
        w = w.at[:, hd:, hid:].set(v_w1[lo * hd:(lo + st) * hd].reshape(st, hd, hid))
        pos = jnp.concatenate([k_pos[lo:lo + st], v_pos[lo:lo + st]], axis=1).reshape(1, st * LANES)
        return w.reshape(st * LANES, 2 * hid).astype(BF16), jnp.broadcast_to(pos, (8, st * LANES)).astype(BF16)

    w1a, pa = half(0)
    w1b, pb = half(st)
    w2 = jnp.zeros((2 * hid, LANES), F32).at[:hid, :hd].set(k_w2).at[hid:, hd:].set(v_w2).astype(BF16)
    return w1a, w1b, pa, pb, w2


def _block_tables(T):
    nc, n_slc = T // CMP_STRIDE, T // SLC_BLOCK
    cs = np.arange(nc)[:, None] * CMP_STRIDE
    blk = np.arange(LANES)[None, :]
    ov = (cs < blk * SLC_BLOCK + SLC_BLOCK) & (cs + CMP_BLOCK > blk * SLC_BLOCK) & (np.arange(nc)[:, None] < nc - 1) & (blk < n_slc)
    et = (np.arange(T)[:, None] // SLC_BLOCK) == blk
    off = np.arange(WINDOW // NSA_TQ + 1)[:, None, None] * NSA_TQ
    diff = off + np.arange(NSA_TQ)[None, :, None] - np.arange(WINDOW + NSA_TQ)[None, None, :]
    band = np.where((diff >= 0) & (diff < WINDOW), 0.0, NEG).astype(np.float32)
    return jnp.asarray(ov.astype(np.float32), BF16), jnp.asarray(et.astype(np.float32), BF16), jnp.asarray(band)


def kernel(x, c, w_ada, b_ada, ln_g, ln_b, ffn1_wg, ffn1_wu, ffn1_wd, w_in, cmp_k_pos, cmp_k_w1, cmp_k_w2,
           cmp_v_pos, cmp_v_w1, cmp_v_w2, mla_q_norm, mla_kv_norm, mla_w_uq, mla_w_ukv, w_out,
           ffn2_wg, ffn2_wu, ffn2_wd):
    B, T, D = x.shape
    depth = w_ada.shape[0]
    alpha = (2.0 * depth) ** 0.25
    ov, et, band = _block_tables(T)
    for l in range(depth):
        mod = _mod_call(c, w_ada[l], b_ada[l])
        sh1, sc1, g1, sh2, sc2, g2, sh3, sc3, g3 = [mod[:, k] for k in range(N_MOD)]
        x = _ffn_call(x, sh1, sc1, g1, ffn1_wg[l], ffn1_wu[l], ffn1_wd[l],
                      ln_g[l, 0], ln_b[l, 0], alpha, 0.5)

        wuq = _pad_slots(mla_w_uq[l], MLA_HEADS, MLA_NOPE_DIM + MLA_ROPE_DIM).astype(BF16)
        wukv = mla_w_ukv[l].reshape(MLA_KV_RANK, MLA_HEADS, MLA_NOPE_DIM + MLA_V_DIM)
        wuk = _pad_slots(wukv[:, :, :MLA_NOPE_DIM].reshape(MLA_KV_RANK, -1), MLA_HEADS, MLA_NOPE_DIM).astype(BF16)
        wuv = wukv[:, :, MLA_NOPE_DIM:].reshape(MLA_KV_RANK, -1).astype(BF16)
        qn, kvc, kvs, kvw, gates, qm, km, vm = _proj_call(
            x, sh2, sc2, _proj_weights(w_in[l]), mla_q_norm[l].reshape(1, -1), mla_kv_norm[l].reshape(1, -1),
            wuq, wuk, wuv)
        kvcmp = _cmp_call(kvc, *_cmp_weights(cmp_k_pos[l], cmp_k_w1[l], cmp_k_w2[l],
                                             cmp_v_pos[l], cmp_v_w1[l], cmp_v_w2[l]))
        o_nsa = _nsa_call(qn, gates, kvcmp, kvs, kvw, ov, et, band)
        o_mla = _mla_call(qm, km, vm)
        n_nsa = NSA_HEADS * NSA_HEAD_DIM
        w_o = w_out[l].astype(BF16)
        x = _out_call(x, o_nsa, o_mla, g2, w_o[:n_nsa], w_o[n_nsa:], ln_g[l, 1], ln_b[l, 1], alpha)

        x = _ffn_call(x, sh3, sc3, g3, ffn2_wg[l], ffn2_wu[l], ffn2_wd[l],
                      ln_g[l, 2], ln_b[l, 2], alpha, 0.5)
    return x
```

```python
import functools

import numpy as np
import jax
import jax.numpy as jnp
from jax import lax
from jax.experimental import pallas as pl
from jax.experimental.pallas import tpu as pltpu

F32 = jnp.float32
BF16 = jnp.bfloat16

ROPE_THETA = 500000.0
NSA_HEADS = 8
NSA_KV_GROUPS = 2
NSA_HPG = NSA_HEADS // NSA_KV_GROUPS
NSA_HEAD_DIM = 64
NSA_ROPE_DIM = 16
CMP_BLOCK = 32
CMP_STRIDE = 16
CMP_HIDDEN = 256
SLC_BLOCK = 64
SLC_TOPK = 16
SLC_SHIFT = SLC_BLOCK.bit_length() - 1
WINDOW = 512
FORCE_SCORE = 1e9
MLA_HEADS = 8
MLA_NOPE_DIM = 64
MLA_ROPE_DIM = 32
MLA_V_DIM = 64
MLA_Q_RANK = 384
MLA_KV_RANK = 256
EPS = 1e-5
N_MOD = 9

LANES = 128
NEG = -1e30
M_INIT = -1e29
PICKED = -3e38
ROW_CHUNK = 128
LOG2E = 1.4426950408889634
VMEM_LIMIT = 48 * 1024 * 1024
MLA_VMEM_LIMIT = 56 * 1024 * 1024

NSA_TQ = 256
NSA_TK = 1024
MLA_T = 512
MLA_HPS = 4
OUT_ROWS = 256
PROJ_ROWS = 256

OFF_Q, OFF_KV, OFF_GATE, OFF_CQ, OFF_CKV, OFF_KR, PROJ_W = 0, 512, 1280, 1536, 1920, 2176, 2304


def _cparams(sem, vmem=VMEM_LIMIT):
    return pltpu.CompilerParams(dimension_semantics=sem, vmem_limit_bytes=vmem)


def _layer_norm(z, g, b):
    mu = jnp.mean(z, -1, keepdims=True)
    zc = z - mu
    var = jnp.mean(zc * zc, -1, keepdims=True)
    return zc * lax.rsqrt(var + EPS) * g + b


def _dot(a, b):
    return jnp.dot(a, b, preferred_element_type=F32)


def _dot_t(a, b):
    return lax.dot_general(a, b, (((1,), (1,)), ((), ())), preferred_element_type=F32)


def _rep_rows(a, n):
    return jnp.concatenate([a] * n, axis=0)


def _rep_lanes(a, n):
    return jnp.concatenate([a] * n, axis=1)


def _iota(shape, d):
    return lax.broadcasted_iota(jnp.int32, shape, d)


def _mod_body(c_ref, w_ref, b_ref, o_ref):
    c = c_ref[...]
    s = (c * jax.nn.sigmoid(c)).astype(BF16)
    o_ref[...] = _dot(s, w_ref[...].astype(BF16)) + b_ref[...]


def _mod_call(c, w_ada, b_ada):
    B, D = c.shape
    rows = 8
    c8 = jnp.zeros((rows, D), F32).at[:B].set(c)
    out = pl.pallas_call(
        _mod_body,
        grid=(N_MOD,),
        in_specs=[pl.BlockSpec((rows, D), lambda j: (0, 0)),
                  pl.BlockSpec((D, D), lambda j: (0, j)),
                  pl.BlockSpec((1, D), lambda j: (0, j))],
        out_specs=pl.BlockSpec((rows, D), lambda j: (0, j)),
        out_shape=jax.ShapeDtypeStruct((rows, N_MOD * D), F32),
        compiler_params=_cparams(("parallel",)),
        name="mod",
    )(c8, w_ada, b_ada.reshape(1, N_MOD * D))
    return out[:B].reshape(B, N_MOD, 1, D)


def _ffn_body(x_ref, sh_ref, sc_ref, g_ref, wg_ref, wu_ref, wd_ref, lng_ref, lnb_ref, o_ref,
              u_scr, acc_scr, *, alpha, coef, n_ff):
    j = pl.program_id(2)

    @pl.when(j == 0)
    def _():
        u_scr[...] = (x_ref[...] * (1.0 + sc_ref[...]) + sh_ref[...]).astype(BF16)
        acc_scr[...] = jnp.zeros_like(acc_scr)

    u = u_scr[...]
    a = _dot(u, wg_ref[...].astype(BF16))
    b = _dot(u, wu_ref[...].astype(BF16))
    h = (a * jax.nn.sigmoid(a)) * b
    acc_scr[...] += _dot(h.astype(BF16), wd_ref[...].astype(BF16))

    @pl.when(j == n_ff - 1)
    def _():
        z = alpha * x_ref[...] + coef * (1.0 + g_ref[...]) * acc_scr[...]
        o_ref[...] = _layer_norm(z, lng_ref[...], lnb_ref[...])


def _ffn_call(x, sh, sc, g, wg, wu, wd, lng, lnb, alpha, coef):
    B, T, D = x.shape
    FF = wg.shape[1]
    tm = min(T, 1024)
    tf = 256
    assert T % tm == 0 and FF % tf == 0
    n_ff = FF // tf
    row = pl.BlockSpec((None, 1, D), lambda b, i, j: (b, 0, 0))
    vec = pl.BlockSpec((1, D), lambda b, i, j: (0, 0))
    return pl.pallas_call(
        functools.partial(_ffn_body, alpha=alpha, coef=coef, n_ff=n_ff),
        grid=(B, T // tm, n_ff),
        in_specs=[pl.BlockSpec((None, tm, D), lambda b, i, j: (b, i, 0)), row, row, row,
                  pl.BlockSpec((D, tf), lambda b, i, j: (0, j)),
                  pl.BlockSpec((D, tf), lambda b, i, j: (0, j)),
                  pl.BlockSpec((tf, D), lambda b, i, j: (j, 0)), vec, vec],
        out_specs=pl.BlockSpec((None, tm, D), lambda b, i, j: (b, i, 0)),
        out_shape=jax.ShapeDtypeStruct((B, T, D), F32),
        scratch_shapes=[pltpu.VMEM((tm, D), BF16), pltpu.VMEM((tm, D), F32)],
        compiler_params=_cparams(("parallel", "parallel", "arbitrary")),
        name="ffn",
    )(x, sh, sc, g, wg, wu, wd, lng.reshape(1, D), lnb.reshape(1, D))


def _rope_slot(x, c, sa, sb, half):
    return x * c + pltpu.roll(x, LANES - half, 1) * sa + pltpu.roll(x, half, 1) * sb


def _rms_norm(x, g):
    return x * lax.rsqrt(jnp.mean(x * x, -1, keepdims=True) + EPS) * g


def _proj_body(x_ref, sh_ref, sc_ref, w_ref, gq_ref, gkv_ref, wuq_ref, wuk_ref, wuv_ref,
               base_ref, tile_ref, sgn_ref,
               qn_ref, kvc_ref, kvs_ref, kvw_ref, gate_ref, qm_ref, km_ref, vm_ref,
               *, nsa_scale, mla_scale):
    tm = x_ref.shape[0]
    rc = min(tm, PROJ_ROWS)
    starts = range(0, tm, rc)
    nh = NSA_ROPE_DIM // 2
    mh = MLA_ROPE_DIM // 2
    hs = [_dot((x_ref[r:r + rc, :] * (1.0 + sc_ref[...]) + sh_ref[...]).astype(BF16), w_ref[...])
          for r in starts]
    for c, r in enumerate(starts):
        h, rows = hs[c], slice(r, r + rc)
        tabs = []
        for st in range(len(ROPE_SETS)):
            lc, ls = slice(2 * st * LANES, (2 * st + 1) * LANES), slice((2 * st + 1) * LANES, (2 * st + 2) * LANES)
            cb, sb_, ca, sa_ = base_ref[rows, lc], base_ref[rows, ls], tile_ref[:, lc], tile_ref[:, ls]
            sin = sa_ * cb + ca * sb_
            tabs += [ca * cb - sa_ * sb_, sin * sgn_ref[0:1, lc], sin * sgn_ref[0:1, ls]]
        cq, sqa, sqb, cn, sna, snb, cm, sma, smb = tabs

        for s in range(NSA_HEADS * NSA_HEAD_DIM // LANES):
            lo = OFF_Q + s * LANES
            qn_ref[rows, s * LANES:(s + 1) * LANES] = (
                _rope_slot(h[:, lo:lo + LANES], cq, sqa, sqb, nh) * nsa_scale).astype(BF16)
        for idx, ref in enumerate((kvc_ref, kvs_ref, kvw_ref)):
            for g in range(NSA_KV_GROUPS):
                lo = OFF_KV + (idx * NSA_KV_GROUPS + g) * LANES
                ref[g, rows, :] = _rope_slot(h[:, lo:lo + LANES], cn, sna, snb, nh).astype(BF16)
        gate_ref[rows, :] = jax.nn.sigmoid(h[:, OFF_GATE:OFF_GATE + NSA_KV_GROUPS * LANES])

        cqn = _rms_norm(h[:, OFF_CQ:OFF_CQ + MLA_Q_RANK], gq_ref[...]).astype(BF16)
        qm = _dot(cqn, wuq_ref[...])
        ckvn = _rms_norm(h[:, OFF_CKV:OFF_CKV + MLA_KV_RANK], gkv_ref[...]).astype(BF16)
        kn = _dot(ckvn, wuk_ref[...])
        vm_ref[rows, :] = _dot(ckvn, wuv_ref[...]).astype(BF16)
        kr = _rope_slot(h[:, OFF_KR:OFF_KR + LANES], cm, sma, smb, mh)
        for s in range(MLA_HEADS):
            sl = slice(s * LANES, (s + 1) * LANES)
            qm_ref[rows, sl] = (_rope_slot(qm[:, sl], cm, sma, smb, mh) * mla_scale).astype(BF16)
            km_ref[rows, sl] = (kn[:, sl] + kr).astype(BF16)


def _proj_call(x, sh, sc, w_all, gq, gkv, wuq, wuk, wuv):
    B, T, D = x.shape
    tm = min(T, 1024)
    G = NSA_KV_GROUPS
    base, tile_, signs = _rope_tables(T, tm)
    row = pl.BlockSpec((None, 1, D), lambda b, i: (b, 0, 0))

    def full(a):
        return pl.BlockSpec(a.shape, lambda b, i: (0,) * a.ndim)

    tile_spec = pl.BlockSpec((None, 1, tile_.shape[-1]), lambda b, i: (i, 0, 0))
    tok = lambda w: pl.BlockSpec((None, tm, w), lambda b, i: (b, i, 0))
    grp = pl.BlockSpec((G, None, tm, LANES), lambda b, i: (0, b, i, 0))
    sd = jax.ShapeDtypeStruct
    return pl.pallas_call(
        functools.partial(_proj_body, nsa_scale=NSA_HEAD_DIM ** -0.5 * LOG2E,
                          mla_scale=(MLA_NOPE_DIM + MLA_ROPE_DIM) ** -0.5 * LOG2E),
        grid=(B, T // tm),
        in_specs=[tok(D), row, row, full(w_all), full(gq), full(gkv), full(wuq), full(wuk), full(wuv)]
                 + [full(base), tile_spec, full(signs)],
        out_specs=[tok(NSA_HEADS * NSA_HEAD_DIM), grp, grp, grp, tok(G * LANES),
                   tok(MLA_HEADS * LANES), tok(MLA_HEADS * LANES), tok(MLA_HEADS * MLA_V_DIM)],
        out_shape=[sd((B, T, NSA_HEADS * NSA_HEAD_DIM), BF16), sd((G, B, T, LANES), BF16),
                   sd((G, B, T, LANES), BF16), sd((G, B, T, LANES), BF16), sd((B, T, G * LANES), F32),
                   sd((B, T, MLA_HEADS * LANES), BF16), sd((B, T, MLA_HEADS * LANES), BF16),
                   sd((B, T, MLA_HEADS * MLA_V_DIM), BF16)],
        compiler_params=_cparams(("parallel", "parallel")),
        name="proj",
    )(x, sh, sc, w_all, gq, gkv, wuq, wuk, wuv, base, tile_, signs)


def _gelu_tanh(x):
    return x * (0.5 * (1.0 + jnp.tanh(0.7978845608028654 * (x + 0.044715 * (x * x * x)))))


def _cmp_body(c_ref, w1a_ref, w1b_ref, pa_ref, pb_ref, w2_ref, o_ref):
    c = c_ref[...]
    nc = c.shape[0]
    w1a, w1b = w1a_ref[...], w1b_ref[...]
    a = _dot(c, w1a)
    b = _dot(c, w1b)
    bias = _dot(pa_ref[...], w1a) + _dot(pb_ref[...], w1b)
    pre = a + pltpu.roll(b, nc - 1, 0) + bias[0:1]
    o_ref[...] = _dot(_gelu_tanh(pre).astype(BF16), w2_ref[...]).astype(BF16)


def _cmp_call(kvc, w1a, w1b, pa, pb, w2):
    G, B, T, _ = kvc.shape
    nc = T // CMP_STRIDE
    c = kvc.reshape(G, B, nc, CMP_STRIDE * LANES)

    def full(a):
        return pl.BlockSpec(a.shape, lambda g, b: (0,) * a.ndim)

    return pl.pallas_call(
        _cmp_body,
        grid=(G, B),
        in_specs=[pl.BlockSpec((None, None, nc, CMP_STRIDE * LANES), lambda g, b: (g, b, 0, 0)),
                  full(w1a), full(w1b), full(pa), full(pb), full(w2)],
        out_specs=pl.BlockSpec((None, None, nc, LANES), lambda g, b: (g, b, 0, 0)),
        out_shape=jax.ShapeDtypeStruct((G, B, nc, LANES), BF16),
        compiler_params=_cparams(("parallel", "parallel")),
        name="cmp",
    )(c, w1a, w1b, pa, pb, w2)


def _nsa_body(q_ref, gate_ref, kvc_ref, kvs_ref, kvw_ref, ov_ref, et_ref, band_ref, o_ref,
              m_scr, l_scr, acc_scr, part_scr, *, n_slc):
    i = pl.program_id(2)
    nc = kvc_ref.shape[0]
    hpg, tq_n, rows, tk = NSA_HPG, NSA_TQ, NSA_HPG * NSA_TQ, NSA_TK
    t0 = pl.multiple_of(i * tq_n, tq_n)
    q = q_ref[...].astype(F32)
    low = _iota((tq_n, LANES), 1) < NSA_HEAD_DIM
    heads = []
    for h in range(hpg):
        tile_ = q[:, (h // 2) * LANES:(h // 2 + 1) * LANES]
        heads.append(jnp.where(low, tile_ if h % 2 == 0 else pltpu.roll(tile_, NSA_HEAD_DIM, 1), 0.0))
    qs = jnp.concatenate(heads, axis=0).astype(BF16)

    wk = WINDOW + tq_n
    start = pl.multiple_of(jnp.maximum(t0 - WINDOW, 0), tq_n)
    kvw = kvw_ref[pl.ds(start, wk), :]
    s_w = _dot_t(qs, kvw)
    s_w = s_w + _rep_rows(band_ref[...], hpg)
    e_w = jnp.exp2(s_w - jnp.max(s_w, -1, keepdims=True))
    o_win = _dot(e_w.astype(BF16), kvw) / jnp.sum(e_w, -1, keepdims=True)

    kvc = kvc_ref[...]
    s = _dot_t(qs, kvc)
    tq = t0 + _iota((tq_n, nc), 0)
    n = _iota((tq_n, nc), 1)
    valid = (n * CMP_STRIDE + (CMP_BLOCK - 1) <= tq) & (n < nc - 1)
    s = s + _rep_rows(jnp.where(valid, 0.0, NEG), hpg)
    e = jnp.exp2(s - jnp.maximum(jnp.max(s, -1, keepdims=True), M_INIT))
    p = e / jnp.maximum(jnp.sum(e, -1, keepdims=True), 1e-30)
    o_cmp = _dot(p.astype(BF16), kvc)

    ps = p[0:tq_n] + p[tq_n:2 * tq_n] + p[2 * tq_n:3 * tq_n] + p[3 * tq_n:4 * tq_n]
    ph = ps.astype(BF16)
    plo = (ps - ph.astype(F32)).astype(BF16)
    ov = ov_ref[...]
    imp_t = (_dot(ph, ov) + _dot(plo, ov)).T

    jr = _iota((LANES, tq_n), 0)
    tl = t0 + _iota((LANES, tq_n), 1)
    cur = lax.shift_right_logical(tl, SLC_SHIFT)
    causal = jr * SLC_BLOCK <= tl
    forced = ((jr == 0) | (jr == cur) | (jr == cur - 1)) & causal
    v = jnp.where(forced, PICKED, jnp.where(causal, imp_t, -1.0))
    if n_slc < LANES:
        v = jnp.where(jr < n_slc, v, PICKED)
    jf = jr.astype(F32)
    n_forced = 3
    for _ in range(SLC_TOPK - n_forced):
        mx = jnp.max(v, axis=0, keepdims=True)
        idx = jnp.min(jnp.where(v == mx, jf, float(LANES)), axis=0, keepdims=True)
        v = jnp.where(jf == idx, PICKED, v)
    taken = v < 0.5 * PICKED
    if n_slc < LANES:
        taken = taken & (jr < n_slc)
    sel = jnp.where(taken, 1.0, 0.0).T

    gt = gate_ref[...]
    gcol = lambda br: jnp.concatenate([gt[:, h * 3 + br:h * 3 + br + 1] for h in range(hpg)], axis=0)
    part_scr[...] = gcol(0) * o_cmp + gcol(2) * o_win

    upto = _iota((tq_n, LANES), 1) < lax.shift_right_logical(t0 + tq_n, SLC_SHIFT)
    lhs = jnp.concatenate([qs, _rep_rows(jnp.where((sel > 0.5) & upto, 0.0, NEG).astype(BF16), hpg)], axis=1)
    tk_shift = tk.bit_length() - 1
    n_full = lax.shift_right_logical(t0 + tq_n - 1, tk_shift)
    tail_idx = lax.shift_right_logical(t0 + tq_n - 1 - n_full * tk, tq_n.bit_length() - 1)

    m_scr[...] = jnp.full(m_scr.shape, M_INIT, F32)
    l_scr[...] = jnp.zeros(l_scr.shape, F32)
    acc_scr[...] = jnp.zeros(acc_scr.shape, F32)
    rc = ROW_CHUNK

    def tile(k, width, last):
        k0 = pl.multiple_of(k * tk, tk)
        kv = kvs_ref[pl.ds(k0, width), :]
        rhs = jnp.concatenate([kv, et_ref[pl.ds(k0, width), :]], axis=1)
        s_heads = [_dot_t(lhs[h * tq_n:(h + 1) * tq_n], rhs) for h in range(hpg)]
        for h in range(hpg):
            hs = slice(h * tq_n, (h + 1) * tq_n)
            ps, als = [], []
            for r in range(0, tq_n, rc):
                sl = slice(h * tq_n + r, h * tq_n + r + rc)
                sc = s_heads[h][r:r + rc]
                if last:
                    own = jnp.where(_iota((rc, tq_n), 1) <= r + _iota((rc, tq_n), 0), 0.0, NEG)
                    sc = sc + own if width == tq_n else jnp.concatenate(
                        [sc[:, :width - tq_n], sc[:, width - tq_n:] + own], axis=1)
                m_prev = m_scr[sl]
                m_new = jnp.maximum(m_prev, jnp.max(sc, -1, keepdims=True))
                alpha = jnp.exp2(m_prev - m_new)
                pp = jnp.exp2(sc - _rep_lanes(m_new, width // LANES))
                l_scr[sl] = alpha * l_scr[sl] + jnp.sum(pp, -1, keepdims=True)
                m_scr[sl] = m_new
                ps.append(pp.astype(BF16))
                als.append(alpha)
            acc_scr[hs] = jnp.concatenate(als, axis=0) * acc_scr[hs] + _dot(jnp.concatenate(ps, axis=0), kv)

    def body(k, carry):
        tile(k, tk, False)
        return carry

    lax.fori_loop(0, n_full, body, 0)
    for j in range(tk // tq_n):
        pl.when(tail_idx == j)(functools.partial(tile, n_full, (j + 1) * tq_n, True))
    acc = acc_scr[...]
    out = part_scr[...] + gcol(1) * (acc / l_scr[...])
    out = jnp.where(_iota((rows, LANES), 1) >= NSA_HEAD_DIM, out, 0.0)
    for hp in range(hpg // 2):
        even = out[2 * hp * tq_n:(2 * hp + 1) * tq_n]
        odd = out[(2 * hp + 1) * tq_n:(2 * hp + 2) * tq_n]
        o_ref[:, hp * LANES:(hp + 1) * LANES] = (pltpu.roll(even, NSA_HEAD_DIM, 1) + odd).astype(BF16)


def _nsa_call(qn, gates, kvcmp, kvs, kvw, ov, et, band):
    B, T, _ = qn.shape
    G = NSA_KV_GROUPS
    nc = kvcmp.shape[2]
    n_slc = T // SLC_BLOCK
    assert SLC_TOPK <= n_slc <= LANES and nc % LANES == 0 and T % NSA_TK == 0 and T >= WINDOW + NSA_TQ
    assert NSA_TK & (NSA_TK - 1) == 0 and NSA_TQ % SLC_BLOCK == 0 and NSA_TK % NSA_TQ == 0
    assert WINDOW % NSA_TQ == 0 and band.shape == (WINDOW // NSA_TQ + 1, NSA_TQ, WINDOW + NSA_TQ)
    gw = NSA_HPG * NSA_HEAD_DIM
    rows = NSA_HPG * NSA_TQ
    whole = lambda r: pl.BlockSpec((None, None, r, LANES), lambda b, g, i: (g, b, 0, 0))
    const = lambda a: pl.BlockSpec(a.shape, lambda b, g, i: (0, 0))
    return pl.pallas_call(
        functools.partial(_nsa_body, n_slc=n_slc),
        grid=(B, G, T // NSA_TQ),
        in_specs=[pl.BlockSpec((None, NSA_TQ, gw), lambda b, g, i: (b, i, g)),
                  pl.BlockSpec((None, NSA_TQ, LANES), lambda b, g, i: (b, i, g)),
                  whole(nc), whole(T), whole(T), const(ov), const(et),
                  pl.BlockSpec((None,) + band.shape[1:], lambda b, g, i: (jnp.minimum(i, band.shape[0] - 1), 0, 0))],
        out_specs=pl.BlockSpec((None, NSA_TQ, NSA_HPG * NSA_HEAD_DIM), lambda b, g, i: (b, i, g)),
        out_shape=jax.ShapeDtypeStruct((B, T, NSA_HEADS * NSA_HEAD_DIM), BF16),
        scratch_shapes=[pltpu.VMEM((rows, LANES), F32)] * 4,
        compiler_params=_cparams(("parallel", "parallel", "arbitrary")),
        name="nsa",
    )(qn, gates, kvcmp, kvs, kvw, ov, et, band)


def _mla_body(q_ref, k_ref, v_ref, o_ref, p_buf, a_buf, m_scr, l_scr, acc_scr):
    i = pl.program_id(2)
    tt, nh = MLA_T, MLA_HPS
    head = lambda hh: slice(hh * LANES, (hh + 1) * LANES)

    pair = lambda hh: slice((hh // 2) * LANES, (hh // 2 + 1) * LANES)

    def scores(kt, hh):
        k0 = pl.multiple_of(kt * tt, tt)
        return _dot_t(q_ref[:, head(hh)], k_ref[pl.ds(k0, tt), head(hh)])

    def values(kt, hh):
        return v_ref[pl.ds(pl.multiple_of(kt * tt, tt), tt), pair(hh)]

    m_scr[...] = jnp.full(m_scr.shape, M_INIT, F32)
    l_scr[...] = jnp.zeros(l_scr.shape, F32)
    acc_scr[...] = jnp.zeros(acc_scr.shape, F32)

    def stage(k, diagonal, with_prev):
        if diagonal:
            tri = jnp.where(_iota((tt, tt), 1) <= _iota((tt, tt), 0), 0.0, NEG)
        for hh in range(nh):
            s = scores(k, hh)
            if diagonal:
                s = s + tri
            if with_prev:
                acc_scr[hh] = a_buf[hh] * acc_scr[hh] + _dot(p_buf[hh], values(k - 1, hh))
            m_prev = m_scr[hh]
            m_new = jnp.maximum(m_prev, jnp.max(s, -1, keepdims=True))
            alpha = jnp.exp2(m_prev - m_new)
            p = jnp.exp2(s - _rep_lanes(m_new, tt // LANES))
            l_scr[hh] = alpha * l_scr[hh] + jnp.sum(p, -1, keepdims=True)
            m_scr[hh] = m_new
            a_buf[hh] = alpha
            p_buf[hh] = p.astype(BF16)

    def body(k, carry):
        stage(k, False, True)
        return carry

    @pl.when(i == 0)
    def _():
        stage(0, True, False)

    @pl.when(i > 0)
    def _():
        stage(0, False, False)
        lax.fori_loop(1, i, body, 0)
        stage(i, True, True)

    outs = [(a_buf[hh] * acc_scr[hh] + _dot(p_buf[hh], values(i, hh))) / l_scr[hh] for hh in range(nh)]
    first = _iota((tt, LANES), 1) < MLA_V_DIM
    for pp in range(nh // 2):
        o_ref[:, pp * LANES:(pp + 1) * LANES] = jnp.where(first, outs[2 * pp], outs[2 * pp + 1]).astype(BF16)


def _mla_call(qm, km, vm):
    B, T, _ = qm.shape
    nh = MLA_HPS
    assert T % MLA_T == 0 and MLA_HEADS % nh == 0 and nh % 2 == 0
    vw = nh * MLA_V_DIM
    return pl.pallas_call(
        _mla_body,
        grid=(B, MLA_HEADS // nh, T // MLA_T),
        in_specs=[pl.BlockSpec((None, MLA_T, nh * LANES), lambda b, p, i: (b, i, p)),
                  pl.BlockSpec((None, T, nh * LANES), lambda b, p, i: (b, 0, p)),
                  pl.BlockSpec((None, T, vw), lambda b, p, i: (b, 0, p))],
        out_specs=pl.BlockSpec((None, MLA_T, vw), lambda b, p, i: (b, i, p)),
        out_shape=jax.ShapeDtypeStruct((B, T, MLA_HEADS * MLA_V_DIM), BF16),
        scratch_shapes=[pltpu.VMEM((nh, MLA_T, MLA_T), BF16)] + [pltpu.VMEM((nh, MLA_T, LANES), F32)] * 4,
        compiler_params=_cparams(("parallel", "parallel", "arbitrary"), MLA_VMEM_LIMIT),
        name="mla",
    )(qm, km, vm)


def _out_body(x_ref, on_ref, om_ref, g_ref, wn_ref, wm_ref, lng_ref, lnb_ref, o_ref, *, alpha):
    tm = x_ref.shape[0]
    rc = min(tm, OUT_ROWS)
    ys = [_dot(on_ref[r:r + rc, :], wn_ref[...]) + _dot(om_ref[r:r + rc, :], wm_ref[...]) for r in range(0, tm, rc)]
    for c, r in enumerate(range(0, tm, rc)):
        z = alpha * x_ref[r:r + rc, :] + (1.0 + g_ref[...]) * ys[c]
        o_ref[r:r + rc, :] = _layer_norm(z, lng_ref[...], lnb_ref[...])


def _out_call(x, o_nsa, o_mla, g, w_on, w_om, lng, lnb, alpha):
    B, T, D = x.shape
    tm = min(T, 1024)
    tok = lambda w: pl.BlockSpec((None, tm, w), lambda b, i: (b, i, 0))
    full = lambda a: pl.BlockSpec(a.shape, lambda b, i: (0,) * a.ndim)
    lng, lnb = lng.reshape(1, D), lnb.reshape(1, D)
    return pl.pallas_call(
        functools.partial(_out_body, alpha=alpha),
        grid=(B, T // tm),
        in_specs=[tok(D), tok(o_nsa.shape[-1]), tok(o_mla.shape[-1]),
                  pl.BlockSpec((None, 1, D), lambda b, i: (b, 0, 0)),
                  full(w_on), full(w_om), full(lng), full(lnb)],
        out_specs=tok(D),
        out_shape=jax.ShapeDtypeStruct((B, T, D), F32),
        compiler_params=_cparams(("parallel", "parallel")),
        name="out",
    )(x, o_nsa, o_mla, g, w_on, w_om, lng, lnb)


ROPE_SETS = (((0, NSA_HEAD_DIM), NSA_ROPE_DIM),
             ((0,), NSA_ROPE_DIM),
             ((MLA_NOPE_DIM,), MLA_ROPE_DIM))
def _rope_tables(T, tm):
    inv_rows, sign_rows = [], []
    for lane0s, dim in ROPE_SETS:
        half = dim // 2
        inv = ROPE_THETA ** (-jnp.arange(half, dtype=F32) / half)
        lane_inv = jnp.zeros((LANES,), F32)
        first, second = np.zeros((LANES,), np.float32), np.zeros((LANES,), np.float32)
        for lane0 in lane0s:
            lane_inv = lane_inv.at[lane0:lane0 + half].set(inv).at[lane0 + half:lane0 + dim].set(inv)
            first[lane0:lane0 + half] = -1.0
            second[lane0 + half:lane0 + dim] = 1.0
        inv_rows.append(lane_inv)
        sign_rows += [first, second]
    inv_all = jnp.stack(inv_rows)

    def trig(pos):
        ang = pos.astype(F32)[:, None, None] * inv_all[None]
        return jnp.stack([jnp.cos(ang), jnp.sin(ang)], axis=2).reshape(pos.shape[0], -1)

    signs = jnp.broadcast_to(jnp.asarray(np.concatenate(sign_rows))[None, :], (8, 2 * len(ROPE_SETS) * LANES))
    return trig(jnp.arange(tm)), trig(jnp.arange(0, T, tm))[:, None, :], signs


def _pad_slots(w, n, width):
    K = w.shape[0]
    return jnp.pad(w.reshape(K, n, width), ((0, 0), (0, 0), (0, LANES - width))).reshape(K, n * LANES)


def _proj_weights(w_in):
    D = w_in.shape[0]
    hd, G = NSA_HEAD_DIM, NSA_KV_GROUPS
    nq, nkv = NSA_HEADS * hd, G * hd
    off = np.cumsum([0, nq] + [nkv] * 6 + [3 * NSA_HEADS, MLA_Q_RANK, MLA_KV_RANK, MLA_ROPE_DIM])
    cols = [w_in[:, :nq]]
    for pair in range(3):
        ko, vo = off[1 + 2 * pair], off[2 + 2 * pair]
        for g in range(G):
            cols += [w_in[:, ko + g * hd:ko + (g + 1) * hd], w_in[:, vo + g * hd:vo + (g + 1) * hd]]
    per_g = 3 * NSA_HPG
    for g in range(G):
        cols.append(jnp.pad(w_in[:, off[7] + g * per_g:off[7] + (g + 1) * per_g], ((0, 0), (0, LANES - per_g))))
    cols += [w_in[:, off[8]:off[9]], w_in[:, off[9]:off[10]]]
    cols.append(jnp.pad(w_in[:, off[10]:off[11]],
                        ((0, 0), (MLA_NOPE_DIM, LANES - MLA_NOPE_DIM - MLA_ROPE_DIM))))
    w_all = jnp.concatenate(cols, axis=1).astype(BF16)
    assert w_all.shape == (D, PROJ_W)
    return w_all


def _cmp_weights(k_pos, k_w1, k_w2, v_pos, v_w1, v_w2):
    hd, st, hid = NSA_HEAD_DIM, CMP_STRIDE, CMP_HIDDEN

    def half(lo):
        w = jnp.zeros((st, LANES, 2 * hid), F32)
        w = w.at[:, :hd, :hid].set(k_w1[lo * hd:(lo + st) * hd].reshape(st, hd, hid))
        w = w.at[:, hd:, hid:].set(v_w1[lo * hd:(lo + st) * hd].reshape(st, hd, hid))
        pos = jnp.concatenate([k_pos[lo:lo + st], v_pos[lo:lo + st]], axis=1).reshape(1, st * LANES)
        return w.reshape(st * LANES, 2 * hid).astype(BF16), jnp.broadcast_to(pos, (8, st * LANES)).astype(BF16)

    w1a, pa = half(0)
    w1b, pb = half(st)
    w2 = jnp.zeros((2 * hid, LANES), F32).at[:hid, :hd].set(k_w2).at[hid:, hd:].set(v_w2).astype(BF16)
    return w1a, w1b, pa, pb, w2


def _block_tables(T):
    nc, n_slc = T // CMP_STRIDE, T // SLC_BLOCK
    cs = np.arange(nc)[:, None] * CMP_STRIDE
    blk = np.arange(LANES)[None, :]
    ov = (cs < blk * SLC_BLOCK + SLC_BLOCK) & (cs + CMP_BLOCK > blk * SLC_BLOCK) & (np.arange(nc)[:, None] < nc - 1) & (blk < n_slc)
    et = (np.arange(T)[:, None] // SLC_BLOCK) == blk
    off = np.arange(WINDOW // NSA_TQ + 1)[:, None, None] * NSA_TQ
    diff = off + np.arange(NSA_TQ)[None, :, None] - np.arange(WINDOW + NSA_TQ)[None, None, :]
    band = np.where((diff >= 0) & (diff < WINDOW), 0.0, NEG).astype(np.float32)
    return jnp.asarray(ov.astype(np.float32), BF16), jnp.asarray(et.astype(np.float32), BF16), jnp.asarray(band)


def kernel(x, c, w_ada, b_ada, ln_g, ln_b, ffn1_wg, ffn1_wu, ffn1_wd, w_in, cmp_k_pos, cmp_k_w1, cmp_k_w2,
           cmp_v_pos, cmp_v_w1, cmp_v_w2, mla_q_norm, mla_kv_norm, mla_w_uq, mla_w_ukv, w_out,
           ffn2_wg, ffn2_wu, ffn2_wd):
    B, T, D = x.shape
    depth = w_ada.shape[0]
    alpha = (2.0 * depth) ** 0.25
    ov, et, band = _block_tables(T)
    for l in range(depth):
        mod = _mod_call(c, w_ada[l], b_ada[l])
        sh1, sc1, g1, sh2, sc2, g2, sh3, sc3, g3 = [mod[:, k] for k in range(N_MOD)]
        x = _ffn_call(x, sh1, sc1, g1, ffn1_wg[l], ffn1_wu[l], ffn1_wd[l],
                      ln_g[l, 0], ln_b[l, 0], alpha, 0.5)

        wuq = _pad_slots(mla_w_uq[l], MLA_HEADS, MLA_NOPE_DIM + MLA_ROPE_DIM).astype(BF16)
        wukv = mla_w_ukv[l].reshape(MLA_KV_RANK, MLA_HEADS, MLA_NOPE_DIM + MLA_V_DIM)
        wuk = _pad_slots(wukv[:, :, :MLA_NOPE_DIM].reshape(MLA_KV_RANK, -1), MLA_HEADS, MLA_NOPE_DIM).astype(BF16)
        wuv = wukv[:, :, MLA_NOPE_DIM:].reshape(MLA_KV_RANK, -1).astype(BF16)
        qn, kvc, kvs, kvw, gates, qm, km, vm = _proj_call(
            x, sh2, sc2, _proj_weights(w_in[l]), mla_q_norm[l].reshape(1, -1), mla_kv_norm[l].reshape(1, -1),
            wuq, wuk, wuv)
        kvcmp = _cmp_call(kvc, *_cmp_weights(cmp_k_pos[l], cmp_k_w1[l], cmp_k_w2[l],
                                             cmp_v_pos[l], cmp_v_w1[l], cmp_v_w2[l]))
        o_nsa = _nsa_call(qn, gates, kvcmp, kvs, kvw, ov, et, band)
        o_mla = _mla_call(qm, km, vm)
        n_nsa = NSA_HEADS * NSA_HEAD_DIM
        w_o = w_out[l].astype(BF16)
        x = _out_call(x, o_nsa, o_mla, g2, w_o[:n_nsa], w_o[n_nsa:], ln_g[l, 1], ln_b[l, 1], alpha)

        x = _ffn_call(x, sh3, sc3, g3, ffn2_wg[l], ffn2_wu[l], ffn2_wd[l],
                      ln_g[l, 2], ln_b[l, 2], alpha, 0.5)
    return x
```

```python
import functools

import numpy as np
import jax
import jax.numpy as jnp
from jax import lax
from jax.experimental import pallas as pl
from jax.experimental.pallas import tpu as pltpu

F32 = jnp.float32
BF16 = jnp.bfloat16

ROPE_THETA = 500000.0
NSA_HEADS = 8
NSA_KV_GROUPS = 2
NSA_HPG = NSA_HEADS // NSA_KV_GROUPS
NSA_HEAD_DIM = 64
NSA_ROPE_DIM = 16
CMP_BLOCK = 32
CMP_STRIDE = 16
CMP_HIDDEN = 256
SLC_BLOCK = 64
SLC_TOPK = 16
SLC_SHIFT = SLC_BLOCK.bit_length() - 1
WINDOW = 512
FORCE_SCORE = 1e9
MLA_HEADS = 8
MLA_NOPE_DIM = 64
MLA_ROPE_DIM = 32
MLA_V_DIM = 64
MLA_Q_RANK = 384
MLA_KV_RANK = 256
EPS = 1e-5
N_MOD = 9

LANES = 128
NEG = -1e30
M_INIT = -1e29
PICKED = -3e38
ROW_CHUNK = 128
LOG2E = 1.4426950408889634
VMEM_LIMIT = 48 * 1024 * 1024
MLA_VMEM_LIMIT = 56 * 1024 * 1024

NSA_TQ = 256
NSA_TK = 1024
MLA_T = 512
MLA_HPS = 4
OUT_ROWS = 256
PROJ_ROWS = 256

OFF_Q, OFF_KV, OFF_GATE, OFF_CQ, OFF_CKV, OFF_KR, PROJ_W = 0, 512, 1280, 1536, 1920, 2176, 2304


def _cparams(sem, vmem=VMEM_LIMIT):
    return pltpu.CompilerParams(dimension_semantics=sem, vmem_limit_bytes=vmem)


def _layer_norm(z, g, b):
    mu = jnp.mean(z, -1, keepdims=True)
    zc = z - mu
    var = jnp.mean(zc * zc, -1, keepdims=True)
    return zc * lax.rsqrt(var + EPS) * g + b


def _dot(a, b):
    return jnp.dot(a, b, preferred_element_type=F32)


def _dot_t(a, b):
    return lax.dot_general(a, b, (((1,), (1,)), ((), ())), preferred_element_type=F32)


def _rep_rows(a, n):
    return jnp.concatenate([a] * n, axis=0)


def _rep_lanes(a, n):
    return jnp.concatenate([a] * n, axis=1)


def _iota(shape, d):
    return lax.broadcasted_iota(jnp.int32, shape, d)


def _mod_body(c_ref, w_ref, b_ref, o_ref):
    c = c_ref[...]
    s = (c * jax.nn.sigmoid(c)).astype(BF16)
    o_ref[...] = _dot(s, w_ref[...].astype(BF16)) + b_ref[...]


def _mod_call(c, w_ada, b_ada):
    B, D = c.shape
    rows = 8
    c8 = jnp.zeros((rows, D), F32).at[:B].set(c)
    out = pl.pallas_call(
        _mod_body,
        grid=(N_MOD,),
        in_specs=[pl.BlockSpec((rows, D), lambda j: (0, 0)),
                  pl.BlockSpec((D, D), lambda j: (0, j)),
                  pl.BlockSpec((1, D), lambda j: (0, j))],
        out_specs=pl.BlockSpec((rows, D), lambda j: (0, j)),
        out_shape=jax.ShapeDtypeStruct((rows, N_MOD * D), F32),
        compiler_params=_cparams(("parallel",)),
        name="mod",
    )(c8, w_ada, b_ada.reshape(1, N_MOD * D))
    return out[:B].reshape(B, N_MOD, 1, D)


def _ffn_body(x_ref, sh_ref, sc_ref, g_ref, wg_ref, wu_ref, wd_ref, lng_ref, lnb_ref, o_ref,
              u_scr, acc_scr, *, alpha, coef, n_ff):
    j = pl.program_id(2)

    @pl.when(j == 0)
    def _():
        u_scr[...] = (x_ref[...] * (1.0 + sc_ref[...]) + sh_ref[...]).astype(BF16)
        acc_scr[...] = jnp.zeros_like(acc_scr)

    u = u_scr[...]
    a = _dot(u, wg_ref[...].astype(BF16))
    b = _dot(u, wu_ref[...].astype(BF16))
    h = (a * jax.nn.sigmoid(a)) * b
    acc_scr[...] += _dot(h.astype(BF16), wd_ref[...].astype(BF16))

    @pl.when(j == n_ff - 1)
    def _():
        z = alpha * x_ref[...] + coef * (1.0 + g_ref[...]) * acc_scr[...]
        o_ref[...] = _layer_norm(z, lng_ref[...], lnb_ref[...])


def _ffn_call(x, sh, sc, g, wg, wu, wd, lng, lnb, alpha, coef):
    B, T, D = x.shape
    FF = wg.shape[1]
    tm = min(T, 1024)
    tf = 256
    assert T % tm == 0 and FF % tf == 0
    n_ff = FF // tf
    row = pl.BlockSpec((None, 1, D), lambda b, i, j: (b, 0, 0))
    vec = pl.BlockSpec((1, D), lambda b, i, j: (0, 0))
    return pl.pallas_call(
        functools.partial(_ffn_body, alpha=alpha, coef=coef, n_ff=n_ff),
        grid=(B, T // tm, n_ff),
        in_specs=[pl.BlockSpec((None, tm, D), lambda b, i, j: (b, i, 0)), row, row, row,
                  pl.BlockSpec((D, tf), lambda b, i, j: (0, j)),
                  pl.BlockSpec((D, tf), lambda b, i, j: (0, j)),
                  pl.BlockSpec((tf, D), lambda b, i, j: (j, 0)), vec, vec],
        out_specs=pl.BlockSpec((None, tm, D), lambda b, i, j: (b, i, 0)),
        out_shape=jax.ShapeDtypeStruct((B, T, D), F32),
        scratch_shapes=[pltpu.VMEM((tm, D), BF16), pltpu.VMEM((tm, D), F32)],
        compiler_params=_cparams(("parallel", "parallel", "arbitrary")),
        name="ffn",
    )(x, sh, sc, g, wg, wu, wd, lng.reshape(1, D), lnb.reshape(1, D))


def _rope_slot(x, c, sa, sb, half):
    return x * c + pltpu.roll(x, LANES - half, 1) * sa + pltpu.roll(x, half, 1) * sb


def _rms_norm(x, g):
    return x * lax.rsqrt(jnp.mean(x * x, -1, keepdims=True) + EPS) * g


def _proj_body(x_ref, sh_ref, sc_ref, w_ref, gq_ref, gkv_ref, wuq_ref, wuk_ref, wuv_ref,
               base_ref, tile_ref, sgn_ref,
               qn_ref, kvc_ref, kvs_ref, kvw_ref, gate_ref, qm_ref, km_ref, vm_ref,
               *, nsa_scale, mla_scale):
    tm = x_ref.shape[0]
    rc = min(tm, PROJ_ROWS)
    starts = range(0, tm, rc)
    nh = NSA_ROPE_DIM // 2
    mh = MLA_ROPE_DIM // 2
    hs = [_dot((x_ref[r:r + rc, :] * (1.0 + sc_ref[...]) + sh_ref[...]).astype(BF16), w_ref[...])
          for r in starts]
    for c, r in enumerate(starts):
        h, rows = hs[c], slice(r, r + rc)
        tabs = []
        for st in range(len(ROPE_SETS)):
            lc, ls = slice(2 * st * LANES, (2 * st + 1) * LANES), slice((2 * st + 1) * LANES, (2 * st + 2) * LANES)
            cb, sb_, ca, sa_ = base_ref[rows, lc], base_ref[rows, ls], tile_ref[:, lc], tile_ref[:, ls]
            sin = sa_ * cb + ca * sb_
            tabs += [ca * cb - sa_ * sb_, sin * sgn_ref[0:1, lc], sin * sgn_ref[0:1, ls]]
        cq, sqa, sqb, cn, sna, snb, cm, sma, smb = tabs

        for s in range(NSA_HEADS * NSA_HEAD_DIM // LANES):
            lo = OFF_Q + s * LANES
            qn_ref[rows, s * LANES:(s + 1) * LANES] = (
                _rope_slot(h[:, lo:lo + LANES], cq, sqa, sqb, nh) * nsa_scale).astype(BF16)
        for idx, ref in enumerate((kvc_ref, kvs_ref, kvw_ref)):
            for g in range(NSA_KV_GROUPS):
                lo = OFF_KV + (idx * NSA_KV_GROUPS + g) * LANES
                ref[g, rows, :] = _rope_slot(h[:, lo:lo + LANES], cn, sna, snb, nh).astype(BF16)
        gate_ref[rows, :] = jax.nn.sigmoid(h[:, OFF_GATE:OFF_GATE + NSA_KV_GROUPS * LANES])

        cqn = _rms_norm(h[:, OFF_CQ:OFF_CQ + MLA_Q_RANK], gq_ref[...]).astype(BF16)
        qm = _dot(cqn, wuq_ref[...])
        ckvn = _rms_norm(h[:, OFF_CKV:OFF_CKV + MLA_KV_RANK], gkv_ref[...]).astype(BF16)
        kn = _dot(ckvn, wuk_ref[...])
        vm_ref[rows, :] = _dot(ckvn, wuv_ref[...]).astype(BF16)
        kr = _rope_slot(h[:, OFF_KR:OFF_KR + LANES], cm, sma, smb, mh)
        for s in range(MLA_HEADS):
            sl = slice(s * LANES, (s + 1) * LANES)
            qm_ref[rows, sl] = (_rope_slot(qm[:, sl], cm, sma, smb, mh) * mla_scale).astype(BF16)
            km_ref[rows, sl] = (kn[:, sl] + kr).astype(BF16)


def _proj_call(x, sh, sc, w_all, gq, gkv, wuq, wuk, wuv):
    B, T, D = x.shape
    tm = min(T, 1024)
    G = NSA_KV_GROUPS
    base, tile_, signs = _rope_tables(T, tm)
    row = pl.BlockSpec((None, 1, D), lambda b, i: (b, 0, 0))

    def full(a):
        return pl.BlockSpec(a.shape, lambda b, i: (0,) * a.ndim)

    tile_spec = pl.BlockSpec((None, 1, tile_.shape[-1]), lambda b, i: (i, 0, 0))
    tok = lambda w: pl.BlockSpec((None, tm, w), lambda b, i: (b, i, 0))
    grp = pl.BlockSpec((G, None, tm, LANES), lambda b, i: (0, b, i, 0))
    sd = jax.ShapeDtypeStruct
    return pl.pallas_call(
        functools.partial(_proj_body, nsa_scale=NSA_HEAD_DIM ** -0.5 * LOG2E,
                          mla_scale=(MLA_NOPE_DIM + MLA_ROPE_DIM) ** -0.5 * LOG2E),
        grid=(B, T // tm),
        in_specs=[tok(D), row, row, full(w_all), full(gq), full(gkv), full(wuq), full(wuk), full(wuv)]
                 + [full(base), tile_spec, full(signs)],
        out_specs=[tok(NSA_HEADS * NSA_HEAD_DIM), grp, grp, grp, tok(G * LANES),
                   tok(MLA_HEADS * LANES), tok(MLA_HEADS * LANES), tok(MLA_HEADS * MLA_V_DIM)],
        out_shape=[sd((B, T, NSA_HEADS * NSA_HEAD_DIM), BF16), sd((G, B, T, LANES), BF16),
                   sd((G, B, T, LANES), BF16), sd((G, B, T, LANES), BF16), sd((B, T, G * LANES), F32),
                   sd((B, T, MLA_HEADS * LANES), BF16), sd((B, T, MLA_HEADS * LANES), BF16),
                   sd((B, T, MLA_HEADS * MLA_V_DIM), BF16)],
        compiler_params=_cparams(("parallel", "parallel")),
        name="proj",
    )(x, sh, sc, w_all, gq, gkv, wuq, wuk, wuv, base, tile_, signs)


def _gelu_tanh(x):
    return x * (0.5 * (1.0 + jnp.tanh(0.7978845608028654 * (x + 0.044715 * (x * x * x)))))


def _cmp_body(c_ref, w1a_ref, w1b_ref, pa_ref, pb_ref, w2_ref, o_ref):
    c = c_ref[...]
    nc = c.shape[0]
    w1a, w1b = w1a_ref[...], w1b_ref[...]
    a = _dot(c, w1a)
    b = _dot(c, w1b)
    bias = _dot(pa_ref[...], w1a) + _dot(pb_ref[...], w1b)
    pre = a + pltpu.roll(b, nc - 1, 0) + bias[0:1]
    o_ref[...] = _dot(_gelu_tanh(pre).astype(BF16), w2_ref[...]).astype(BF16)


def _cmp_call(kvc, w1a, w1b, pa, pb, w2):
    G, B, T, _ = kvc.shape
    nc = T // CMP_STRIDE
    c = kvc.reshape(G, B, nc, CMP_STRIDE * LANES)

    def full(a):
        return pl.BlockSpec(a.shape, lambda g, b: (0,) * a.ndim)

    return pl.pallas_call(
        _cmp_body,
        grid=(G, B),
        in_specs=[pl.BlockSpec((None, None, nc, CMP_STRIDE * LANES), lambda g, b: (g, b, 0, 0)),
                  full(w1a), full(w1b), full(pa), full(pb), full(w2)],
        out_specs=pl.BlockSpec((None, None, nc, LANES), lambda g, b: (g, b, 0, 0)),
        out_shape=jax.ShapeDtypeStruct((G, B, nc, LANES), BF16),
        compiler_params=_cparams(("parallel", "parallel")),
        name="cmp",
    )(c, w1a, w1b, pa, pb, w2)


def _nsa_body(q_ref, gate_ref, kvc_ref, kvs_ref, kvw_ref, ov_ref, et_ref, band_ref, o_ref,
              m_scr, l_scr, acc_scr, part_scr, lhs_scr, *, n_slc):
    i = pl.program_id(2)
    nc = kvc_ref.shape[0]
    hpg, tq_n, rows, tk = NSA_HPG, NSA_TQ, NSA_HPG * NSA_TQ, NSA_TK
    t0 = pl.multiple_of(i * tq_n, tq_n)
    q = q_ref[...].astype(F32)
    low = _iota((tq_n, LANES), 1) < NSA_HEAD_DIM
    heads = []
    for h in range(hpg):
        tile_ = q[:, (h // 2) * LANES:(h // 2 + 1) * LANES]
        heads.append(jnp.where(low, tile_ if h % 2 == 0 else pltpu.roll(tile_, NSA_HEAD_DIM, 1), 0.0))
    qs = jnp.concatenate(heads, axis=0).astype(BF16)

    wk = WINDOW + tq_n
    start = pl.multiple_of(jnp.maximum(t0 - WINDOW, 0), tq_n)
    kvw = kvw_ref[pl.ds(start, wk), :]
    s_w = _dot_t(qs, kvw)
    s_w = s_w + _rep_rows(band_ref[...], hpg)
    e_w = jnp.exp2(s_w - jnp.max(s_w, -1, keepdims=True))
    o_win = _dot(e_w.astype(BF16), kvw) / jnp.sum(e_w, -1, keepdims=True)

    kvc = kvc_ref[...]
    s = _dot_t(qs, kvc)
    tq = t0 + _iota((tq_n, nc), 0)
    n = _iota((tq_n, nc), 1)
    valid = (n * CMP_STRIDE + (CMP_BLOCK - 1) <= tq) & (n < nc - 1)
    s = s + _rep_rows(jnp.where(valid, 0.0, NEG), hpg)
    e = jnp.exp2(s - jnp.maximum(jnp.max(s, -1, keepdims=True), M_INIT))
    p = e / jnp.maximum(jnp.sum(e, -1, keepdims=True), 1e-30)
    o_cmp = _dot(p.astype(BF16), kvc)

    ps = p[0:tq_n] + p[tq_n:2 * tq_n] + p[2 * tq_n:3 * tq_n] + p[3 * tq_n:4 * tq_n]
    ph = ps.astype(BF16)
    plo = (ps - ph.astype(F32)).astype(BF16)
    ov = ov_ref[...]
    imp_t = (_dot(ph, ov) + _dot(plo, ov)).T

    jr = _iota((LANES, tq_n), 0)
    tl = t0 + _iota((LANES, tq_n), 1)
    cur = lax.shift_right_logical(tl, SLC_SHIFT)
    causal = jr * SLC_BLOCK <= tl
    forced = ((jr == 0) | (jr == cur) | (jr == cur - 1)) & causal
    v = jnp.where(forced, PICKED, jnp.where(causal, imp_t, -1.0))
    if n_slc < LANES:
        v = jnp.where(jr < n_slc, v, PICKED)
    jf = jr.astype(F32)
    n_forced = 3
    for _ in range(SLC_TOPK - n_forced):
        mx = jnp.max(v, axis=0, keepdims=True)
        idx = jnp.min(jnp.where(v == mx, jf, float(LANES)), axis=0, keepdims=True)
        v = jnp.where(jf == idx, PICKED, v)
    taken = v < 0.5 * PICKED
    if n_slc < LANES:
        taken = taken & (jr < n_slc)
    sel = jnp.where(taken, 1.0, 0.0).T

    gt = gate_ref[...]
    gcol = lambda br: jnp.concatenate([gt[:, h * 3 + br:h * 3 + br + 1] for h in range(hpg)], axis=0)
    part_scr[...] = gcol(0) * o_cmp + gcol(2) * o_win

    upto = _iota((tq_n, LANES), 1) < lax.shift_right_logical(t0 + tq_n, SLC_SHIFT)
    lhs_scr[...] = jnp.concatenate(
        [qs, _rep_rows(jnp.where((sel > 0.5) & upto, 0.0, NEG).astype(BF16), hpg)], axis=1)
    tk_shift = tk.bit_length() - 1
    n_full = lax.shift_right_logical(t0 + tq_n - 1, tk_shift)
    tail_idx = lax.shift_right_logical(t0 + tq_n - 1 - n_full * tk, tq_n.bit_length() - 1)

    m_scr[...] = jnp.full(m_scr.shape, M_INIT, F32)
    l_scr[...] = jnp.zeros(l_scr.shape, F32)
    acc_scr[...] = jnp.zeros(acc_scr.shape, F32)
    rc = ROW_CHUNK

    def tile(k, width, last):
        k0 = pl.multiple_of(k * tk, tk)
        kv = kvs_ref[pl.ds(k0, width), :]
        rhs = jnp.concatenate([kv, et_ref[pl.ds(k0, width), :]], axis=1)
        s_heads = [_dot_t(lhs_scr[h * tq_n:(h + 1) * tq_n], rhs) for h in range(hpg)]
        for h in range(hpg):
            hs = slice(h * tq_n, (h + 1) * tq_n)
            ps, als = [], []
            for r in range(0, tq_n, rc):
                sl = slice(h * tq_n + r, h * tq_n + r + rc)
                sc = s_heads[h][r:r + rc]
                if last:
                    own = jnp.where(_iota((rc, tq_n), 1) <= r + _iota((rc, tq_n), 0), 0.0, NEG)
                    sc = sc + own if width == tq_n else jnp.concatenate(
                        [sc[:, :width - tq_n], sc[:, width - tq_n:] + own], axis=1)
                m_prev = m_scr[sl]
                m_new = jnp.maximum(m_prev, jnp.max(sc, -1, keepdims=True))
                alpha = jnp.exp2(m_prev - m_new)
                pp = jnp.exp2(sc - _rep_lanes(m_new, width // LANES))
                l_scr[sl] = alpha * l_scr[sl] + jnp.sum(pp, -1, keepdims=True)
                m_scr[sl] = m_new
                ps.append(pp.astype(BF16))
                als.append(alpha)
            acc_scr[hs] = jnp.concatenate(als, axis=0) * acc_scr[hs] + _dot(
                jnp.concatenate(ps, axis=0), kvs_ref[pl.ds(k0, width), :])

    def body(k, carry):
        tile(k, tk, False)
        return carry

    lax.fori_loop(0, n_full, body, 0)
    for j in range(tk // tq_n):
        pl.when(tail_idx == j)(functools.partial(tile, n_full, (j + 1) * tq_n, True))
    acc = acc_scr[...]
    out = part_scr[...] + gcol(1) * (acc / l_scr[...])
    out = jnp.where(_iota((rows, LANES), 1) >= NSA_HEAD_DIM, out, 0.0)
    for hp in range(hpg // 2):
        even = out[2 * hp * tq_n:(2 * hp + 1) * tq_n]
        odd = out[(2 * hp + 1) * tq_n:(2 * hp + 2) * tq_n]
        o_ref[:, hp * LANES:(hp + 1) * LANES] = (pltpu.roll(even, NSA_HEAD_DIM, 1) + odd).astype(BF16)


def _nsa_call(qn, gates, kvcmp, kvs, kvw, ov, et, band):
    B, T, _ = qn.shape
    G = NSA_KV_GROUPS
    nc = kvcmp.shape[2]
    n_slc = T // SLC_BLOCK
    assert SLC_TOPK <= n_slc <= LANES and nc % LANES == 0 and T % NSA_TK == 0 and T >= WINDOW + NSA_TQ
    assert NSA_TK & (NSA_TK - 1) == 0 and NSA_TQ % SLC_BLOCK == 0 and NSA_TK % NSA_TQ == 0
    assert WINDOW % NSA_TQ == 0 and band.shape == (WINDOW // NSA_TQ + 1, NSA_TQ, WINDOW + NSA_TQ)
    gw = NSA_HPG * NSA_HEAD_DIM
    rows = NSA_HPG * NSA_TQ
    whole = lambda r: pl.BlockSpec((None, None, r, LANES), lambda b, g, i: (g, b, 0, 0))
    const = lambda a: pl.BlockSpec(a.shape, lambda b, g, i: (0, 0))
    return pl.pallas_call(
        functools.partial(_nsa_body, n_slc=n_slc),
        grid=(B, G, T // NSA_TQ),
        in_specs=[pl.BlockSpec((None, NSA_TQ, gw), lambda b, g, i: (b, i, g)),
                  pl.BlockSpec((None, NSA_TQ, LANES), lambda b, g, i: (b, i, g)),
                  whole(nc), whole(T), whole(T), const(ov), const(et),
                  pl.BlockSpec((None,) + band.shape[1:], lambda b, g, i: (jnp.minimum(i, band.shape[0] - 1), 0, 0))],
        out_specs=pl.BlockSpec((None, NSA_TQ, NSA_HPG * NSA_HEAD_DIM), lambda b, g, i: (b, i, g)),
        out_shape=jax.ShapeDtypeStruct((B, T, NSA_HEADS * NSA_HEAD_DIM), BF16),
        scratch_shapes=[pltpu.VMEM((rows, LANES), F32)] * 4 + [pltpu.VMEM((rows, 2 * LANES), BF16)],
        compiler_params=_cparams(("parallel", "parallel", "arbitrary")),
        name="nsa",
    )(qn, gates, kvcmp, kvs, kvw, ov, et, band)


def _mla_body(q_ref, k_ref, v_ref, o_ref, p_buf, a_buf, m_scr, l_scr, acc_scr):
    i = pl.program_id(2)
    tt, nh = MLA_T, MLA_HPS
    head = lambda hh: slice(hh * LANES, (hh + 1) * LANES)

    pair = lambda hh: slice((hh // 2) * LANES, (hh // 2 + 1) * LANES)

    def scores(kt, hh):
        k0 = pl.multiple_of(kt * tt, tt)
        return _dot_t(q_ref[:, head(hh)], k_ref[pl.ds(k0, tt), head(hh)])

    def values(kt, hh):
        return v_ref[pl.ds(pl.multiple_of(kt * tt, tt), tt), pair(hh)]

    m_scr[...] = jnp.full(m_scr.shape, M_INIT, F32)
    l_scr[...] = jnp.zeros(l_scr.shape, F32)
    acc_scr[...] = jnp.zeros(acc_scr.shape, F32)

    def stage(k, diagonal, with_prev):
        if diagonal:
            tri = jnp.where(_iota((tt, tt), 1) <= _iota((tt, tt), 0), 0.0, NEG)
        for hh in range(nh):
            s = scores(k, hh)
            if diagonal:
                s = s + tri
            if with_prev:
                acc_scr[hh] = a_buf[hh] * acc_scr[hh] + _dot(p_buf[hh], values(k - 1, hh))
            m_prev = m_scr[hh]
            m_new = jnp.maximum(m_prev, jnp.max(s, -1, keepdims=True))
            alpha = jnp.exp2(m_prev - m_new)
            p = jnp.exp2(s - _rep_lanes(m_new, tt // LANES))
            l_scr[hh] = alpha * l_scr[hh] + jnp.sum(p, -1, keepdims=True)
            m_scr[hh] = m_new
            a_buf[hh] = alpha
            p_buf[hh] = p.astype(BF16)

    def body(k, carry):
        stage(k, False, True)
        return carry

    @pl.when(i == 0)
    def _():
        stage(0, True, False)

    @pl.when(i > 0)
    def _():
        stage(0, False, False)
        lax.fori_loop(1, i, body, 0)
        stage(i, True, True)

    outs = [(a_buf[hh] * acc_scr[hh] + _dot(p_buf[hh], values(i, hh))) / l_scr[hh] for hh in range(nh)]
    first = _iota((tt, LANES), 1) < MLA_V_DIM
    for pp in range(nh // 2):
        o_ref[:, pp * LANES:(pp + 1) * LANES] = jnp.where(first, outs[2 * pp], outs[2 * pp + 1]).astype(BF16)


def _mla_call(qm, km, vm):
    B, T, _ = qm.shape
    nh = MLA_HPS
    assert T % MLA_T == 0 and MLA_HEADS % nh == 0 and nh % 2 == 0
    vw = nh * MLA_V_DIM
    return pl.pallas_call(
        _mla_body,
        grid=(B, MLA_HEADS // nh, T // MLA_T),
        in_specs=[pl.BlockSpec((None, MLA_T, nh * LANES), lambda b, p, i: (b, i, p)),
                  pl.BlockSpec((None, T, nh * LANES), lambda b, p, i: (b, 0, p)),
                  pl.BlockSpec((None, T, vw), lambda b, p, i: (b, 0, p))],
        out_specs=pl.BlockSpec((None, MLA_T, vw), lambda b, p, i: (b, i, p)),
        out_shape=jax.ShapeDtypeStruct((B, T, MLA_HEADS * MLA_V_DIM), BF16),
        scratch_shapes=[pltpu.VMEM((nh, MLA_T, MLA_T), BF16)] + [pltpu.VMEM((nh, MLA_T, LANES), F32)] * 4,
        compiler_params=_cparams(("parallel", "parallel", "arbitrary"), MLA_VMEM_LIMIT),
        name="mla",
    )(qm, km, vm)


def _out_body(x_ref, on_ref, om_ref, g_ref, wn_ref, wm_ref, lng_ref, lnb_ref, o_ref, *, alpha):
    tm = x_ref.shape[0]
    rc = min(tm, OUT_ROWS)
    ys = [_dot(on_ref[r:r + rc, :], wn_ref[...]) + _dot(om_ref[r:r + rc, :], wm_ref[...]) for r in range(0, tm, rc)]
    for c, r in enumerate(range(0, tm, rc)):
        z = alpha * x_ref[r:r + rc, :] + (1.0 + g_ref[...]) * ys[c]
        o_ref[r:r + rc, :] = _layer_norm(z, lng_ref[...], lnb_ref[...])


def _out_call(x, o_nsa, o_mla, g, w_on, w_om, lng, lnb, alpha):
    B, T, D = x.shape
    tm = min(T, 1024)
    tok = lambda w: pl.BlockSpec((None, tm, w), lambda b, i: (b, i, 0))
    full = lambda a: pl.BlockSpec(a.shape, lambda b, i: (0,) * a.ndim)
    lng, lnb = lng.reshape(1, D), lnb.reshape(1, D)
    return pl.pallas_call(
        functools.partial(_out_body, alpha=alpha),
        grid=(B, T // tm),
        in_specs=[tok(D), tok(o_nsa.shape[-1]), tok(o_mla.shape[-1]),
                  pl.BlockSpec((None, 1, D), lambda b, i: (b, 0, 0)),
                  full(w_on), full(w_om), full(lng), full(lnb)],
        out_specs=tok(D),
        out_shape=jax.ShapeDtypeStruct((B, T, D), F32),
        compiler_params=_cparams(("parallel", "parallel")),
        name="out",
    )(x, o_nsa, o_mla, g, w_on, w_om, lng, lnb)


ROPE_SETS = (((0, NSA_HEAD_DIM), NSA_ROPE_DIM),
             ((0,), NSA_ROPE_DIM),
             ((MLA_NOPE_DIM,), MLA_ROPE_DIM))
def _rope_tables(T, tm):
    inv_rows, sign_rows = [], []
    for lane0s, dim in ROPE_SETS:
        half = dim // 2
        inv = ROPE_THETA ** (-jnp.arange(half, dtype=F32) / half)
        lane_inv = jnp.zeros((LANES,), F32)
        first, second = np.zeros((LANES,), np.float32), np.zeros((LANES,), np.float32)
        for lane0 in lane0s:
            lane_inv = lane_inv.at[lane0:lane0 + half].set(inv).at[lane0 + half:lane0 + dim].set(inv)
            first[lane0:lane0 + half] = -1.0
            second[lane0 + half:lane0 + dim] = 1.0
        inv_rows.append(lane_inv)
        sign_rows += [first, second]
    inv_all = jnp.stack(inv_rows)

    def trig(pos):
        ang = pos.astype(F32)[:, None, None] * inv_all[None]
        return jnp.stack([jnp.cos(ang), jnp.sin(ang)], axis=2).reshape(pos.shape[0], -1)

    signs = jnp.broadcast_to(jnp.asarray(np.concatenate(sign_rows))[None, :], (8, 2 * len(ROPE_SETS) * LANES))
    return trig(jnp.arange(tm)), trig(jnp.arange(0, T, tm))[:, None, :], signs


def _pad_slots(w, n, width):
    K = w.shape[0]
    return jnp.pad(w.reshape(K, n, width), ((0, 0), (0, 0), (0, LANES - width))).reshape(K, n * LANES)


def _proj_weights(w_in):
    D = w_in.shape[0]
    hd, G = NSA_HEAD_DIM, NSA_KV_GROUPS
    nq, nkv = NSA_HEADS * hd, G * hd
    off = np.cumsum([0, nq] + [nkv] * 6 + [3 * NSA_HEADS, MLA_Q_RANK, MLA_KV_RANK, MLA_ROPE_DIM])
    cols = [w_in[:, :nq]]
    for pair in range(3):
        ko, vo = off[1 + 2 * pair], off[2 + 2 * pair]
        for g in range(G):
            cols += [w_in[:, ko + g * hd:ko + (g + 1) * hd], w_in[:, vo + g * hd:vo + (g + 1) * hd]]
    per_g = 3 * NSA_HPG
    for g in range(G):
        cols.append(jnp.pad(w_in[:, off[7] + g * per_g:off[7] + (g + 1) * per_g], ((0, 0), (0, LANES - per_g))))
    cols += [w_in[:, off[8]:off[9]], w_in[:, off[9]:off[10]]]
    cols.append(jnp.pad(w_in[:, off[10]:off[11]],
                        ((0, 0), (MLA_NOPE_DIM, LANES - MLA_NOPE_DIM - MLA_ROPE_DIM))))
    w_all = jnp.concatenate(cols, axis=1).astype(BF16)
    assert w_all.shape == (D, PROJ_W)
    return w_all


def _cmp_weights(k_pos, k_w1, k_w2, v_pos, v_w1, v_w2):
    hd, st, hid = NSA_HEAD_DIM, CMP_STRIDE, CMP_HIDDEN

    def half(lo):
        w = jnp.zeros((st, LANES, 2 * hid), F32)
        w = w.at[:, :hd, :hid].set(k_w1[lo * hd:(lo + st) * hd].reshape(st, hd, hid))
        w = w.at[:, hd:, hid:].set(v_w1[lo * hd:(lo + st) * hd].reshape(st, hd, hid))
        pos = jnp.concatenate([k_pos[lo:lo + st], v_pos[lo:lo + st]], axis=1).reshape(1, st * LANES)
        return w.reshape(st * LANES, 2 * hid).astype(BF16), jnp.broadcast_to(pos, (8, st * LANES)).astype(BF16)

    w1a, pa = half(0)
    w1b, pb = half(st)
    w2 = jnp.zeros((2 * hid, LANES), F32).at[:hid, :hd].set(k_w2).at[hid:, hd:].set(v_w2).astype(BF16)
    return w1a, w1b, pa, pb, w2


def _block_tables(T):
    nc, n_slc = T // CMP_STRIDE, T // SLC_BLOCK
    cs = np.arange(nc)[:, None] * CMP_STRIDE
    blk = np.arange(LANES)[None, :]
    ov = (cs < blk * SLC_BLOCK + SLC_BLOCK) & (cs + CMP_BLOCK > blk * SLC_BLOCK) & (np.arange(nc)[:, None] < nc - 1) & (blk < n_slc)
    et = (np.arange(T)[:, None] // SLC_BLOCK) == blk
    off = np.arange(WINDOW // NSA_TQ + 1)[:, None, None] * NSA_TQ
    diff = off + np.arange(NSA_TQ)[None, :, None] - np.arange(WINDOW + NSA_TQ)[None, None, :]
    band = np.where((diff >= 0) & (diff < WINDOW), 0.0, NEG).astype(np.float32)
    return jnp.asarray(ov.astype(np.float32), BF16), jnp.asarray(et.astype(np.float32), BF16), jnp.asarray(band)


def kernel(x, c, w_ada, b_ada, ln_g, ln_b, ffn1_wg, ffn1_wu, ffn1_wd, w_in, cmp_k_pos, cmp_k_w1, cmp_k_w2,
           cmp_v_pos, cmp_v_w1, cmp_v_w2, mla_q_norm, mla_kv_norm, mla_w_uq, mla_w_ukv, w_out,
           ffn2_wg, ffn2_wu, ffn2_wd):
    B, T, D = x.shape
    depth = w_ada.shape[0]
    alpha = (2.0 * depth) ** 0.25
    ov, et, band = _block_tables(T)
    for l in range(depth):
        mod = _mod_call(c, w_ada[l], b_ada[l])
        sh1, sc1, g1, sh2, sc2, g2, sh3, sc3, g3 = [mod[:, k] for k in range(N_MOD)]
        x = _ffn_call(x, sh1, sc1, g1, ffn1_wg[l], ffn1_wu[l], ffn1_wd[l],
                      ln_g[l, 0], ln_b[l, 0], alpha, 0.5)

        wuq = _pad_slots(mla_w_uq[l], MLA_HEADS, MLA_NOPE_DIM + MLA_ROPE_DIM).astype(BF16)
        wukv = mla_w_ukv[l].reshape(MLA_KV_RANK, MLA_HEADS, MLA_NOPE_DIM + MLA_V_DIM)
        wuk = _pad_slots(wukv[:, :, :MLA_NOPE_DIM].reshape(MLA_KV_RANK, -1), MLA_HEADS, MLA_NOPE_DIM).astype(BF16)
        wuv = wukv[:, :, MLA_NOPE_DIM:].reshape(MLA_KV_RANK, -1).astype(BF16)
        qn, kvc, kvs, kvw, gates, qm, km, vm = _proj_call(
            x, sh2, sc2, _proj_weights(w_in[l]), mla_q_norm[l].reshape(1, -1), mla_kv_norm[l].reshape(1, -1),
            wuq, wuk, wuv)
        kvcmp = _cmp_call(kvc, *_cmp_weights(cmp_k_pos[l], cmp_k_w1[l], cmp_k_w2[l],
                                             cmp_v_pos[l], cmp_v_w1[l], cmp_v_w2[l]))
        o_nsa = _nsa_call(qn, gates, kvcmp, kvs, kvw, ov, et, band)
        o_mla = _mla_call(qm, km, vm)
        n_nsa = NSA_HEADS * NSA_HEAD_DIM
        w_o = w_out[l].astype(BF16)
        x = _out_call(x, o_nsa, o_mla, g2, w_o[:n_nsa], w_o[n_nsa:], ln_g[l, 1], ln_b[l, 1], alpha)

        x = _ffn_call(x, sh3, sc3, g3, ffn2_wg[l], ffn2_wu[l], ffn2_wd[l],
                      ln_g[l, 2], ln_b[l, 2], alpha, 0.5)
    return x
```
